```python
import jax, jax.numpy as jnp
from jax import lax
import numpy as np

D_MODEL = 1024
BATCH = 8
SEQ = 2048
DEPTH = 4
DEC_BATCH = 128
DEC_SEQ = 8
PAST_LEN = 16384
PAGE_SIZE = 128

N_EVEN = (DEPTH + 1) // 2
N_ODD = DEPTH // 2
CONV_A_DIM = D_MODEL // 2
CONV_A_W = 3
DN_HEADS = 4
DN_HEAD_DIM = (D_MODEL // 2) // DN_HEADS
DN_DIM = DN_HEADS * DN_HEAD_DIM
DN_CONV_W = 4
DN_CHUNK = 64
EVEN_IN = 3 * CONV_A_DIM + 4 * DN_DIM + 2 * DN_HEADS
RET_HEADS = 8
RET_QK_DIM = D_MODEL // RET_HEADS
RET_V_DIM = 2 * D_MODEL // RET_HEADS
RET_CHUNK = 64
ODD_IN = 2 * D_MODEL + 4 * D_MODEL
ROPE_BASE = 10000.0
D_FF = ((-(-8 * D_MODEL // 3) + 255) // 256) * 256
EPS = 1e-6
GN_EPS = 1e-5

kernel_name = "hybrid_conv_deltanet_retention_step"

F32 = jnp.float32


def rmsnorm(x, g):
    xf = x.astype(F32)
    y = xf * lax.rsqrt(jnp.mean(xf * xf, axis=-1, keepdims=True) + EPS)
    return (y * g.astype(F32)).astype(x.dtype)


def l2norm(x):
    return x * lax.rsqrt(jnp.sum(x * x, axis=-1, keepdims=True) + EPS)


def split_heads(t, h):
    return t.reshape(t.shape[:-1] + (h, t.shape[-1] // h))


def chunk_len(length, chunk):
    return chunk if length % chunk == 0 else length


def causal_dwconv(x, prefix, w):
    width = w.shape[0]
    length = x.shape[1]
    xp = jnp.concatenate([prefix.astype(x.dtype), x], axis=1)
    y = xp[:, 0:length] * w[0]
    for i in range(1, width):
        y = y + xp[:, i:i + length] * w[i]
    return y, xp[:, length:]


def to_blocks(t, n, c):
    b = t.shape[0]
    return jnp.moveaxis(t.reshape((b, n, c) + t.shape[2:]), 3, 2)


def from_blocks(o):
    o = jnp.moveaxis(o, 0, 1)
    o = jnp.moveaxis(o, 2, 3)
    b, n, c, h, d = o.shape
    return o.reshape(b, n * c, h, d)


def gated_delta_chunked(q, k, v, g, beta, s0, chunk):
    bsz, length, h, dk = q.shape
    c = chunk_len(length, chunk)
    n = length // c
    qc, kc, vc, gc, bc = [to_blocks(t, n, c) for t in (q, k, v, g, beta)]
    G = jnp.cumsum(gc, axis=-1)
    causal = jnp.tril(jnp.ones((c, c), dtype=bool))
    decay = jnp.exp(jnp.where(causal, G[..., :, None] - G[..., None, :], -jnp.inf))
    strict = jnp.tril(jnp.ones((c, c), F32), -1)
    kb = kc * bc[..., None]
    vb = vc * bc[..., None]
    a_mat = jnp.eye(c, dtype=F32) + jnp.einsum('bnhid,bnhjd->bnhij', kb, kc) * decay * strict
    u_base = lax.linalg.triangular_solve(a_mat, vb, left_side=True, lower=True, unit_diagonal=True)
    w_mat = lax.linalg.triangular_solve(a_mat, kb * jnp.exp(G)[..., None], left_side=True,
                                        lower=True, unit_diagonal=True)
    aqk = jnp.einsum('bnhid,bnhjd->bnhij', qc, kc) * decay
    xs = tuple(jnp.moveaxis(t, 1, 0) for t in (qc, kc, G, u_base, w_mat, aqk))

    def step(s, inp):
        q_, k_, g_, ub, w_, a_ = inp
        u = ub - jnp.einsum('bhck,bhkv->bhcv', w_, s)
        o = (jnp.einsum('bhck,bhkv->bhcv', q_ * jnp.exp(g_)[..., None], s)
             + jnp.einsum('bhij,bhjv->bhiv', a_, u))
        g_last = g_[..., -1:]
        s = s * jnp.exp(g_last)[..., None] + jnp.einsum(
            'bhck,bhcv->bhkv', k_ * jnp.exp(g_last - g_)[..., None], u)
        return s, o

    s, o = lax.scan(step, s0, xs)
    return from_blocks(o), s


def retention_chunked(q, k, v, log_gamma, r0, chunk):
    bsz, length, h, dk = q.shape
    c = chunk_len(length, chunk)
    n = length // c
    qc, kc, vc = [to_blocks(t, n, c) for t in (q, k, v)]
    idx = jnp.arange(c, dtype=F32)
    diff = idx[:, None] - idx[None, :]
    lg = log_gamma[:, None, None]
    decay = jnp.where(diff >= 0, jnp.exp(jnp.maximum(diff, 0.0) * lg), 0.0)
    inner = jnp.exp((idx + 1.0) * log_gamma[:, None])[..., None]
    kdec = jnp.exp((c - 1.0 - idx) * log_gamma[:, None])[..., None]
    cdec = jnp.exp(c * log_gamma)[:, None, None]
    xs = tuple(jnp.moveaxis(t, 1, 0) for t in (qc, kc, vc))

    def step(r, inp):
        q_, k_, v_ = inp
        att = jnp.einsum('bhid,bhjd->bhij', q_, k_) * decay
        o = jnp.einsum('bhij,bhjv->bhiv', att, v_) + jnp.einsum('bhck,bhkv->bhcv', q_, r) * inner
        r = r * cdec + jnp.einsum('bhck,bhcv->bhkv', k_ * kdec, v_)
        return r, o

    r, o = lax.scan(step, r0, xs)
    return from_blocks(o), r


def rotary(x, pos):
    half = x.shape[-1] // 2
    inv = ROPE_BASE ** (-jnp.arange(half, dtype=F32) / half)
    ang = pos.astype(F32)[:, None] * inv[None, :]
    cos = jnp.cos(ang)[:, None, :]
    sin = jnp.sin(ang)[:, None, :]
    x1, x2 = x[..., :half], x[..., half:]
    return jnp.concatenate([x1 * cos - x2 * sin, x1 * sin + x2 * cos], axis=-1)


def even_mixer(h, w_in, conv_a_w, conv_qkv_w, a_log, dt_bias, dn_norm_g, w_out,
               conv_a_prev, conv_qkv_prev, delta_prev):
    bsz, length, _ = h.shape
    p = h @ w_in
    cuts = [int(c) for c in np.cumsum([CONV_A_DIM] * 3 + [DN_DIM] * 4 + [DN_HEADS])]
    gate_b, gate_c, h_a, q, k, v, z, a, b = jnp.split(p, cuts, axis=-1)
    conv_out, conv_a_new = causal_dwconv(gate_c * h_a, conv_a_prev, conv_a_w)
    y_a = gate_b * conv_out
    qkv, conv_qkv_new = causal_dwconv(jnp.concatenate([q, k, v], axis=-1), conv_qkv_prev, conv_qkv_w)
    qkv = jax.nn.silu(qkv.astype(F32))
    q, k, v = jnp.split(qkv, 3, axis=-1)
    q = l2norm(split_heads(q, DN_HEADS)) * (DN_HEAD_DIM ** -0.5)
    k = l2norm(split_heads(k, DN_HEADS))
    v = split_heads(v, DN_HEADS)
    beta = jax.nn.sigmoid(b.astype(F32))
    g = -jnp.exp(a_log.astype(F32)) * jax.nn.softplus(a.astype(F32) + dt_bias.astype(F32))
    o, s_new = gated_delta_chunked(q, k, v, g, beta, delta_prev.astype(F32), DN_CHUNK)
    o = (o * lax.rsqrt(jnp.mean(o * o, axis=-1, keepdims=True) + EPS) * dn_norm_g.astype(F32)
         * jax.nn.silu(split_heads(z, DN_HEADS).astype(F32)))
    y_b = o.reshape(bsz, length, DN_DIM)
    y = jnp.concatenate([y_a.astype(h.dtype), y_b.astype(h.dtype)], axis=-1) @ w_out
    return y, conv_a_new, conv_qkv_new, s_new


def odd_mixer(h, w_in, gn_g, gn_b, w_out, ret_prev, pos):
    bsz, length, _ = h.shape
    p = h @ w_in
    q, k, v, gate = jnp.split(p, [D_MODEL, 2 * D_MODEL, 4 * D_MODEL], axis=-1)
    q = rotary(split_heads(q, RET_HEADS).astype(F32), pos)
    k = rotary(split_heads(k, RET_HEADS).astype(F32), pos) * (RET_QK_DIM ** -0.5)
    v = split_heads(v, RET_HEADS).astype(F32)
    log_gamma = jnp.log(1.0 - 2.0 ** (-5.0 - jnp.arange(RET_HEADS, dtype=F32)))
    o, r_new = retention_chunked(q, k, v, log_gamma, ret_prev.astype(F32), RET_CHUNK)
    mu = jnp.mean(o, axis=-1, keepdims=True)
    var = jnp.mean(jnp.square(o - mu), axis=-1, keepdims=True)
    o = ((o - mu) * lax.rsqrt(var + GN_EPS)).reshape(bsz, length, 2 * D_MODEL)
    o = o * gn_g.astype(F32) + gn_b.astype(F32)
    y = (jax.nn.silu(gate.astype(F32)) * o).astype(h.dtype) @ w_out
    return y, r_new


def run_trunk(x, conv_a_prev, conv_qkv_prev, delta_prev, ret_prev, pos, w):
    conv_a_out, conv_qkv_out, delta_out, ret_out = [], [], [], []
    for i in range(DEPTH):
        h = rmsnorm(x, w['norm_mix'][i])
        if i % 2 == 0:
            e = i // 2
            y, ca, cq, s = even_mixer(h, w['even_w_in'][e], w['even_conv_a'][e], w['even_conv_qkv'][e],
                                      w['even_a_log'][e], w['even_dt_bias'][e], w['even_dn_norm'][e],
                                      w['even_w_out'][e], conv_a_prev[e], conv_qkv_prev[e], delta_prev[e])
            conv_a_out.append(ca)
            conv_qkv_out.append(cq)
            delta_out.append(s)
        else:
            o = i // 2
            y, r = odd_mixer(h, w['odd_w_in'][o], w['odd_gn_g'][o], w['odd_gn_b'][o], w['odd_w_out'][o],
                             ret_prev[o], pos)
            ret_out.append(r)
        x = x + y.astype(x.dtype)
        h = rmsnorm(x, w['norm_ffn'][i])
        f = (jax.nn.silu(h @ w['ffn_w_gate'][i]) * (h @ w['ffn_w_up'][i])) @ w['ffn_w_down'][i]
        x = x + f.astype(x.dtype)
    y = rmsnorm(x, w['final_norm'])
    return y, jnp.stack(conv_a_out), jnp.stack(conv_qkv_out), jnp.stack(delta_out), jnp.stack(ret_out)


def setup_inputs(seed: int = 0) -> dict:
    key = jax.random.key(seed)
    ks = jax.random.split(key, 32)
    nrm = lambda k, shape, s: jax.random.normal(k, shape, F32) * s
    dt = jnp.exp(jax.random.uniform(ks[10], (N_EVEN, DN_HEADS), F32, np.log(1e-3), np.log(1e-1)))
    return {
        "x_prompt": nrm(ks[0], (BATCH, SEQ, D_MODEL), 1.0),
        "x_sample": nrm(ks[1], (DEC_BATCH, DEC_SEQ, D_MODEL), 1.0),
        "state_conv_a": nrm(ks[2], (N_EVEN, DEC_BATCH, CONV_A_W - 1, CONV_A_DIM), 1.0),
        "state_conv_qkv": nrm(ks[3], (N_EVEN, DEC_BATCH, DN_CONV_W - 1, 3 * DN_DIM), 1.0),
        "state_delta": nrm(ks[4], (N_EVEN, DEC_BATCH, DN_HEADS, DN_HEAD_DIM, DN_HEAD_DIM), 0.1),
        "state_ret": nrm(ks[5], (N_ODD, DEC_BATCH, RET_HEADS, RET_QK_DIM, RET_V_DIM), 1.0),
        "norm_mix": 1.0 + nrm(ks[6], (DEPTH, D_MODEL), 0.02),
        "norm_ffn": 1.0 + nrm(ks[7], (DEPTH, D_MODEL), 0.02),
        "final_norm": 1.0 + nrm(ks[8], (D_MODEL,), 0.02),
        "even_w_in": nrm(ks[9], (N_EVEN, D_MODEL, EVEN_IN), D_MODEL ** -0.5),
        "even_conv_a": nrm(ks[11], (N_EVEN, CONV_A_W, CONV_A_DIM), CONV_A_W ** -0.5),
        "even_conv_qkv": nrm(ks[12], (N_EVEN, DN_CONV_W, 3 * DN_DIM), DN_CONV_W ** -0.5),
        "even_a_log": jnp.log(jax.random.uniform(ks[13], (N_EVEN, DN_HEADS), F32, 1.0, 16.0)),
        "even_dt_bias": dt + jnp.log(-jnp.expm1(-dt)),
        "even_dn_norm": 1.0 + nrm(ks[14], (N_EVEN, DN_HEAD_DIM), 0.02),
        "even_w_out": nrm(ks[15], (N_EVEN, D_MODEL, D_MODEL), D_MODEL ** -0.5),
        "odd_w_in": nrm(ks[16], (N_ODD, D_MODEL, ODD_IN), D_MODEL ** -0.5),
        "odd_gn_g": 1.0 + nrm(ks[17], (N_ODD, 2 * D_MODEL), 0.02),
        "odd_gn_b": nrm(ks[18], (N_ODD, 2 * D_MODEL), 0.02),
        "odd_w_out": nrm(ks[19], (N_ODD, 2 * D_MODEL, D_MODEL), (2 * D_MODEL) ** -0.5),
        "ffn_w_gate": nrm(ks[20], (DEPTH, D_MODEL, D_FF), D_MODEL ** -0.5),
        "ffn_w_up": nrm(ks[21], (DEPTH, D_MODEL, D_FF), D_MODEL ** -0.5),
        "ffn_w_down": nrm(ks[22], (DEPTH, D_FF, D_MODEL), D_FF ** -0.5),
    }


def reference(x_prompt, x_sample, state_conv_a, state_conv_qkv, state_delta, state_ret,
              norm_mix, norm_ffn, final_norm, even_w_in, even_conv_a, even_conv_qkv, even_a_log,
              even_dt_bias, even_dn_norm, even_w_out, odd_w_in, odd_gn_g, odd_gn_b, odd_w_out,
              ffn_w_gate, ffn_w_up, ffn_w_down):
    w = dict(norm_mix=norm_mix, norm_ffn=norm_ffn, final_norm=final_norm, even_w_in=even_w_in,
             even_conv_a=even_conv_a, even_conv_qkv=even_conv_qkv, even_a_log=even_a_log,
             even_dt_bias=even_dt_bias, even_dn_norm=even_dn_norm, even_w_out=even_w_out,
             odd_w_in=odd_w_in, odd_gn_g=odd_gn_g, odd_gn_b=odd_gn_b, odd_w_out=odd_w_out,
             ffn_w_gate=ffn_w_gate, ffn_w_up=ffn_w_up, ffn_w_down=ffn_w_down)
    bp, lp, _ = x_prompt.shape
    ls = x_sample.shape[1]
    zc_a = jnp.zeros((N_EVEN, bp, CONV_A_W - 1, CONV_A_DIM), x_prompt.dtype)
    zc_qkv = jnp.zeros((N_EVEN, bp, DN_CONV_W - 1, 3 * DN_DIM), x_prompt.dtype)
    z_delta = jnp.zeros((N_EVEN, bp, DN_HEADS, DN_HEAD_DIM, DN_HEAD_DIM), F32)
    z_ret = jnp.zeros((N_ODD, bp, RET_HEADS, RET_QK_DIM, RET_V_DIM), F32)
    pos_p = jnp.arange(lp, dtype=jnp.int32)
    y_prompt, ca_p, cq_p, d_p, r_p = run_trunk(x_prompt, zc_a, zc_qkv, z_delta, z_ret, pos_p, w)
    pos_s = PAST_LEN + jnp.arange(ls, dtype=jnp.int32)
    y_sample, ca_s, cq_s, d_s, r_s = run_trunk(x_sample, state_conv_a, state_conv_qkv, state_delta,
                                               state_ret, pos_s, w)
    return (y_prompt, y_sample, ca_p, cq_p, d_p, r_p, ca_s, cq_s, d_s, r_s)
```

```python
import functools
import math

import jax
import jax.numpy as jnp
from jax import lax
from jax.experimental import pallas as pl
from jax.experimental.pallas import tpu as pltpu

F32 = jnp.float32
BF16 = jnp.bfloat16

EPS = 1e-6
GN_EPS = 1e-5
ROPE_BASE = 10000.0
PAST_LEN = 16384
CHUNK = 64

CONV_A_W = 3
DN_CONV_W = 4
DN_HEADS = 4
DN_HEAD_DIM = 128
RET_HEADS = 8
RET_QK_DIM = 128
RET_V_DIM = 256

LANES = 128
SUBLANES = 8
VMEM_LIMIT_BYTES = 56 * 1024 * 1024


def _compiler_params(semantics):
    return pltpu.CompilerParams(dimension_semantics=semantics,
                                vmem_limit_bytes=VMEM_LIMIT_BYTES)


def _resident(shape):
    nd = len(shape)
    return pl.BlockSpec(shape, lambda *_: (0,) * nd, pipeline_mode=pl.Buffered(1))


def _mm(a, b):
    return jnp.dot(a.astype(BF16), b.astype(BF16), preferred_element_type=F32)


def _mm_nt(a, b):
    return lax.dot_general(a.astype(BF16), b.astype(BF16), (((1,), (1,)), ((), ())),
                           preferred_element_type=F32)


def _mm_tn(a, b):
    return lax.dot_general(a.astype(BF16), b.astype(BF16), (((0,), (0,)), ((), ())),
                           preferred_element_type=F32)


def _sigmoid(x):
    return 1.0 / (1.0 + jnp.exp(-x))


def _silu(x):
    return x * _sigmoid(x)


def _rmsnorm(x, g):
    return x * lax.rsqrt(jnp.mean(x * x, axis=-1, keepdims=True) + EPS) * g


def _norm_matmul_kernel(x_ref, g_ref, *refs):
    nw = len(refs) // 2
    h = _rmsnorm(x_ref[...], g_ref[...]).astype(BF16)
    for w_ref, o_ref in zip(refs[:nw], refs[nw:]):
        o_ref[...] = jnp.dot(h, w_ref[...], preferred_element_type=F32)


def _norm_matmul(x, g, weights, tm):
    m, d = x.shape
    grid = (m // tm,)
    in_specs = [pl.BlockSpec((tm, d), lambda i: (i, 0)), _resident((1, d))]
    in_specs += [_resident(w.shape) for w in weights]
    out_specs = [pl.BlockSpec((tm, w.shape[1]), lambda i: (i, 0)) for w in weights]
    out_shape = [jax.ShapeDtypeStruct((m, w.shape[1]), F32) for w in weights]
    return pl.pallas_call(
        _norm_matmul_kernel, grid=grid, in_specs=in_specs, out_specs=out_specs,
        out_shape=out_shape, compiler_params=_compiler_params(("arbitrary",)),
        name="norm_matmul")(x, g.reshape(1, d), *weights)


def _out_ffn_kernel(x_ref, y_ref, wo_ref, g_ref, wg_ref, wu_ref, wd_ref, gf_ref, o_ref, *, final):
    x1 = x_ref[...] + jnp.dot(y_ref[...], wo_ref[...], preferred_element_type=F32)
    h = _rmsnorm(x1, g_ref[...]).astype(BF16)
    gate = jnp.dot(h, wg_ref[...], preferred_element_type=F32)
    up = jnp.dot(h, wu_ref[...], preferred_element_type=F32)
    a = (_silu(gate) * up).astype(BF16)
    x2 = x1 + jnp.dot(a, wd_ref[...], preferred_element_type=F32)
    if final:
        x2 = _rmsnorm(x2, gf_ref[...])
    o_ref[...] = x2


def _out_ffn(x, y, wo, g, wg, wu, wd, gf, final, tm):
    m, d = x.shape
    grid = (m // tm,)
    in_specs = [pl.BlockSpec((tm, d), lambda i: (i, 0)),
                pl.BlockSpec((tm, y.shape[1]), lambda i: (i, 0)),
                _resident(wo.shape), _resident((1, d)), _resident(wg.shape),
                _resident(wu.shape), _resident(wd.shape), _resident((1, d))]
    return pl.pallas_call(
        functools.partial(_out_ffn_kernel, final=final), grid=grid, in_specs=in_specs,
        out_specs=pl.BlockSpec((tm, d), lambda i: (i, 0)),
        out_shape=jax.ShapeDtypeStruct((m, d), F32),
        compiler_params=_compiler_params(("arbitrary",)),
        name="out_ffn")(x, y, wo, g.reshape(1, d), wg, wu, wd, gf.reshape(1, d))


def _even_mix_kernel(*refs, bt, lt, c, has_state):
    if has_state:
        (pm_ref, pab_ref, caw_ref, cqw_ref, alog_ref, dt_ref, dng_ref,
         ca_prev_ref, cq_prev_ref, s_prev_ref,
         y_ref, ca_new_ref, cq_new_ref, s_ref,
         ua_s, qkv_s, q_s, k_s, v_s, g_s, b_s, o_s) = refs
    else:
        (pm_ref, pab_ref, caw_ref, cqw_ref, alog_ref, dt_ref, dng_ref,
         y_ref, ca_new_ref, cq_new_ref, s_ref,
         ua_s, qkv_s, q_s, k_s, v_s, g_s, b_s, o_s) = refs
    dn = DN_HEADS * DN_HEAD_DIM
    ca = ua_s.shape[-1]
    l = pl.program_id(1)
    nl = pl.num_programs(1)
    rows = bt * lt
    pad = SUBLANES

    @pl.when(l == 0)
    def _():
        ua_s[:, 0:pad, :] = jnp.zeros((bt, pad, ca), F32)
        qkv_s[:, 0:pad, :] = jnp.zeros((bt, pad, 3 * dn), F32)
        if has_state:
            ua_s[:, pad - (CONV_A_W - 1):pad, :] = ca_prev_ref[...]
            qkv_s[:, pad - (DN_CONV_W - 1):pad, :] = cq_prev_ref[...]
            s_ref[...] = s_prev_ref[...]
        else:
            s_ref[...] = jnp.zeros(s_ref.shape, F32)

    @pl.when(l > 0)
    def _():
        ua_s[:, 0:pad, :] = ua_s[:, lt:lt + pad, :]
        qkv_s[:, 0:pad, :] = qkv_s[:, lt:lt + pad, :]

    gate_b = pm_ref[:, 0:ca]
    ua_s[:, pad:pad + lt, :] = (pm_ref[:, ca:2 * ca] * pm_ref[:, 2 * ca:3 * ca]).reshape(bt, lt, ca)
    conv = caw_ref[0:1, :] * ua_s[:, pad - 2:pad - 2 + lt, :]
    for i in range(1, CONV_A_W):
        conv = conv + caw_ref[i:i + 1, :] * ua_s[:, pad - 2 + i:pad - 2 + i + lt, :]
    y_ref[:, 0:ca] = (gate_b * conv.reshape(rows, ca)).astype(y_ref.dtype)

    qkv_s[:, pad:pad + lt, :] = pm_ref[:, 3 * ca:3 * ca + 3 * dn].reshape(bt, lt, 3 * dn)
    off = pad - (DN_CONV_W - 1)
    conv = cqw_ref[0:1, :] * qkv_s[:, off:off + lt, :]
    for i in range(1, DN_CONV_W):
        conv = conv + cqw_ref[i:i + 1, :] * qkv_s[:, off + i:off + i + lt, :]
    qkv = _silu(conv).reshape(rows, 3 * dn)
    for h in range(DN_HEADS):
        sl = slice(h * DN_HEAD_DIM, (h + 1) * DN_HEAD_DIM)
        qh = qkv[:, h * DN_HEAD_DIM:(h + 1) * DN_HEAD_DIM]
        kh = qkv[:, dn + h * DN_HEAD_DIM:dn + (h + 1) * DN_HEAD_DIM]
        q_s[:, sl] = qh * (lax.rsqrt(jnp.sum(qh * qh, axis=-1, keepdims=True) + EPS)
                           * (DN_HEAD_DIM ** -0.5))
        k_s[:, sl] = kh * lax.rsqrt(jnp.sum(kh * kh, axis=-1, keepdims=True) + EPS)
    v_s[...] = qkv[:, 2 * dn:3 * dn]
    pab = pab_ref[...]
    sp = jnp.maximum(pab + dt_ref[...], 0.0) + jnp.log(1.0 + jnp.exp(-jnp.abs(pab + dt_ref[...])))
    g_s[...] = -jnp.exp(alog_ref[...]) * sp
    b_s[...] = _sigmoid(pab)

    ri = lax.broadcasted_iota(jnp.int32, (c, c), 0)
    ci = lax.broadcasted_iota(jnp.int32, (c, c), 1)
    upper = ri <= ci
    causal = ri >= ci
    strict = ri > ci
    eye = ri == ci
    eye_f = eye.astype(F32)
    nc = lt // c
    n_sq = int(math.log2(c)) - 1

    def chunk_body(i, carry):
        b = i // nc if nc > 1 and bt > 1 else (i if nc == 1 else 0)
        r0 = pl.multiple_of(i * c, c)
        rs = pl.ds(r0, c)
        gch = g_s[rs, :]
        bch = b_s[rs, :]
        for h in range(DN_HEADS):
            sl = slice(h * DN_HEAD_DIM, (h + 1) * DN_HEAD_DIM)
            q = q_s[rs, sl]
            k = k_s[rs, sl]
            v = v_s[rs, sl]
            beta = bch[:, DN_HEADS + h:DN_HEADS + h + 1]
            g_col = gch[:, h:h + 1]
            g_row = jnp.sum(jnp.where(upper, g_col, 0.0), axis=0, keepdims=True)
            g_rows = jnp.broadcast_to(g_row, (c, c))
            g_cum = jnp.sum(jnp.where(eye, g_rows, 0.0), axis=1, keepdims=True)
            diff = g_cum - g_rows
            decay = jnp.exp(jnp.where(causal, diff, -jnp.inf))
            kb = k * beta
            vb = v * beta
            x = -(_mm_nt(kb, k) * jnp.where(strict, decay, 0.0))
            t_inv = eye_f + x
            for _ in range(n_sq):
                x = _mm(x, x)
                t_inv = t_inv + _mm(t_inv, x)
            e_g = jnp.exp(g_cum)
            aqk = _mm_nt(q, k) * decay
            s = s_ref[b, h]
            u = _mm(t_inv, vb - _mm(kb * e_g, s))
            o_s[rs, sl] = _mm(q * e_g, s) + _mm(aqk, u)
            g_last = g_cum[c - 1:c, :]
            s_ref[b, h] = s * jnp.exp(g_last) + _mm_tn(k * jnp.exp(g_last - g_cum), u)
        return carry

    lax.fori_loop(0, bt * nc, chunk_body, 0)

    for h in range(DN_HEADS):
        sl = slice(h * DN_HEAD_DIM, (h + 1) * DN_HEAD_DIM)
        o = o_s[:, sl]
        z = pm_ref[:, 3 * ca + 3 * dn + h * DN_HEAD_DIM:3 * ca + 3 * dn + (h + 1) * DN_HEAD_DIM]
        o = o * lax.rsqrt(jnp.mean(o * o, axis=-1, keepdims=True) + EPS) * dng_ref[...] * _silu(z)
        y_ref[:, ca + h * DN_HEAD_DIM:ca + (h + 1) * DN_HEAD_DIM] = o.astype(y_ref.dtype)

    @pl.when(l == nl - 1)
    def _():
        ca_new_ref[...] = ua_s[:, pad + lt - (CONV_A_W - 1):pad + lt, :]
        cq_new_ref[...] = qkv_s[:, pad + lt - (DN_CONV_W - 1):pad + lt, :]


def _even_mix(p_main, p_ab, conv_a_w, conv_qkv_w, a_log, dt_bias, dn_norm, states, layer,
              batch, length, bt, lt):
    c = CHUNK if length % CHUNK == 0 else length
    ca = conv_a_w.shape[1]
    dn = DN_HEADS * DN_HEAD_DIM
    nl = length // lt
    rows = bt * lt
    grid = (batch // bt, nl)
    has_state = states is not None
    row_map = lambda b, l: (b * nl + l, 0)
    alog_row = jnp.zeros((1, LANES), F32).at[0, :DN_HEADS].set(a_log)
    dt_row = jnp.zeros((1, LANES), F32).at[0, :DN_HEADS].set(dt_bias)
    in_specs = [pl.BlockSpec((rows, p_main.shape[1]), row_map),
                pl.BlockSpec((rows, LANES), row_map),
                _resident(conv_a_w.shape), _resident(conv_qkv_w.shape),
                _resident((1, LANES)), _resident((1, LANES)), _resident((1, DN_HEAD_DIM))]
    args = [p_main, p_ab, conv_a_w, conv_qkv_w, alog_row, dt_row, dn_norm.reshape(1, DN_HEAD_DIM)]
    if has_state:
        ca_prev, cq_prev, s_prev = states
        in_specs += [
            pl.BlockSpec((None, bt, CONV_A_W - 1, ca), lambda b, l: (layer, b, 0, 0)),
            pl.BlockSpec((None, bt, DN_CONV_W - 1, 3 * dn), lambda b, l: (layer, b, 0, 0)),
            pl.BlockSpec((None, bt, DN_HEADS, DN_HEAD_DIM, DN_HEAD_DIM),
                         lambda b, l: (layer, b, 0, 0, 0))]
        args += [ca_prev, cq_prev, s_prev]
    out_specs = [pl.BlockSpec((rows, ca + dn), row_map),
                 pl.BlockSpec((bt, CONV_A_W - 1, ca), lambda b, l: (b, 0, 0)),
                 pl.BlockSpec((bt, DN_CONV_W - 1, 3 * dn), lambda b, l: (b, 0, 0)),
                 pl.BlockSpec((bt, DN_HEADS, DN_HEAD_DIM, DN_HEAD_DIM), lambda b, l: (b, 0, 0, 0))]
    out_shape = [jax.ShapeDtypeStruct((batch * length, ca + dn), BF16),
                 jax.ShapeDtypeStruct((batch, CONV_A_W - 1, ca), F32),
                 jax.ShapeDtypeStruct((batch, DN_CONV_W - 1, 3 * dn), F32),
                 jax.ShapeDtypeStruct((batch, DN_HEADS, DN_HEAD_DIM, DN_HEAD_DIM), F32)]
    scratch = [pltpu.VMEM((bt, lt + SUBLANES, ca), F32),
               pltpu.VMEM((bt, lt + SUBLANES, 3 * dn), F32),
               pltpu.VMEM((rows, dn), F32), pltpu.VMEM((rows, dn), F32), pltpu.VMEM((rows, dn), F32),
               pltpu.VMEM((rows, LANES), F32), pltpu.VMEM((rows, LANES), F32),
               pltpu.VMEM((rows, dn), F32)]
    return pl.pallas_call(
        functools.partial(_even_mix_kernel, bt=bt, lt=lt, c=c, has_state=has_state),
        grid=grid, in_specs=in_specs, out_specs=out_specs, out_shape=out_shape,
        scratch_shapes=scratch, compiler_params=_compiler_params(("arbitrary", "arbitrary")),
        name="even_mix")(*args)


def _odd_mix_kernel(*refs, bt, lt, c, has_state):
    if has_state:
        (p_ref, cos_ref, sin_ref, gng_ref, gnb_ref, r_prev_ref, y_ref, r_ref, q_s, k_s, o_s) = refs
    else:
        (p_ref, cos_ref, sin_ref, gng_ref, gnb_ref, y_ref, r_ref, q_s, k_s, o_s) = refs
    dq = RET_HEADS * RET_QK_DIM
    dv = RET_HEADS * RET_V_DIM
    l = pl.program_id(1)

    @pl.when(l == 0)
    def _():
        if has_state:
            r_ref[...] = r_prev_ref[...]
        else:
            r_ref[...] = jnp.zeros(r_ref.shape, F32)

    cos2 = cos_ref[...]
    sin2 = sin_ref[...]
    for h in range(RET_HEADS):
        sl = slice(h * RET_QK_DIM, (h + 1) * RET_QK_DIM)
        q = p_ref[:, h * RET_QK_DIM:(h + 1) * RET_QK_DIM]
        k = p_ref[:, dq + h * RET_QK_DIM:dq + (h + 1) * RET_QK_DIM]
        q_s[:, sl] = q * cos2 + pltpu.roll(q, RET_QK_DIM // 2, 1) * sin2
        k_s[:, sl] = (k * cos2 + pltpu.roll(k, RET_QK_DIM // 2, 1) * sin2) * (RET_QK_DIM ** -0.5)

    ri = lax.broadcasted_iota(jnp.int32, (c, c), 0)
    ci = lax.broadcasted_iota(jnp.int32, (c, c), 1)
    diff = (ri - ci).astype(F32)
    idx_k = lax.broadcasted_iota(jnp.int32, (c, RET_QK_DIM), 0).astype(F32)
    idx_v = lax.broadcasted_iota(jnp.int32, (c, RET_V_DIM), 0).astype(F32)
    decay, inner, kdec, cdec = [], [], [], []
    for h in range(RET_HEADS):
        lg = math.log(1.0 - 2.0 ** (-5.0 - h))
        decay.append(jnp.where(diff >= 0, jnp.exp(jnp.maximum(diff, 0.0) * lg), 0.0))
        inner.append(jnp.exp((idx_v + 1.0) * lg))
        kdec.append(jnp.exp((c - 1.0 - idx_k) * lg))
        cdec.append(math.exp(c * lg))
    nc = lt // c

    def chunk_body(i, carry):
        b = i // nc if nc > 1 and bt > 1 else (i if nc == 1 else 0)
        r0 = pl.multiple_of(i * c, c)
        rs = pl.ds(r0, c)
        for h in range(RET_HEADS):
            q = q_s[rs, h * RET_QK_DIM:(h + 1) * RET_QK_DIM]
            k = k_s[rs, h * RET_QK_DIM:(h + 1) * RET_QK_DIM]
            v = p_ref[rs, 2 * dq + h * RET_V_DIM:2 * dq + (h + 1) * RET_V_DIM]
            att = _mm_nt(q, k) * decay[h]
            r = r_ref[b, h]
            o_s[rs, h * RET_V_DIM:(h + 1) * RET_V_DIM] = _mm(att, v) + _mm(q, r) * inner[h]
            r_ref[b, h] = r * cdec[h] + _mm_tn(k * kdec[h], v)
        return carry

    lax.fori_loop(0, bt * nc, chunk_body, 0)

    for h in range(RET_HEADS):
        sl = slice(h * RET_V_DIM, (h + 1) * RET_V_DIM)
        o = o_s[:, sl]
        mu = jnp.mean(o, axis=-1, keepdims=True)
        var = jnp.mean(jnp.square(o - mu), axis=-1, keepdims=True)
        o = (o - mu) * lax.rsqrt(var + GN_EPS) * gng_ref[:, sl] + gnb_ref[:, sl]
        gate = p_ref[:, 2 * dq + dv + h * RET_V_DIM:2 * dq + dv + (h + 1) * RET_V_DIM]
        y_ref[:, sl] = (_silu(gate) * o).astype(y_ref.dtype)


def _rope_tables(pos):
    half = RET_QK_DIM // 2
    inv = ROPE_BASE ** (-jnp.arange(half, dtype=F32) / half)
    ang = pos.astype(F32)[:, None] * inv[None, :]
    cos = jnp.cos(ang)
    sin = jnp.sin(ang)
    return jnp.concatenate([cos, cos], axis=-1), jnp.concatenate([-sin, sin], axis=-1)


def _odd_mix(p, gn_g, gn_b, r_prev, layer, pos, batch, length, bt, lt):
    c = CHUNK if length % CHUNK == 0 else length
    dq = RET_HEADS * RET_QK_DIM
    dv = RET_HEADS * RET_V_DIM
    nl = length // lt
    rows = bt * lt
    grid = (batch // bt, nl)
    has_state = r_prev is not None
    row_map = lambda b, l: (b * nl + l, 0)
    cos2, sin2 = _rope_tables(pos)
    if bt > 1:
        cos2 = jnp.tile(cos2, (bt, 1))
        sin2 = jnp.tile(sin2, (bt, 1))
    tab_spec = pl.BlockSpec((rows, RET_QK_DIM), lambda b, l: (l, 0))
    in_specs = [pl.BlockSpec((rows, p.shape[1]), row_map), tab_spec, tab_spec,
                _resident((1, dv)), _resident((1, dv))]
    args = [p, cos2, sin2, gn_g.reshape(1, dv), gn_b.reshape(1, dv)]
    if has_state:
        in_specs.append(pl.BlockSpec((None, bt, RET_HEADS, RET_QK_DIM, RET_V_DIM),
                                     lambda b, l: (layer, b, 0, 0, 0)))
        args.append(r_prev)
    out_specs = [pl.BlockSpec((rows, dv), row_map),
                 pl.BlockSpec((bt, RET_HEADS, RET_QK_DIM, RET_V_DIM), lambda b, l: (b, 0, 0, 0))]
    out_shape = [jax.ShapeDtypeStruct((batch * length, dv), BF16),
                 jax.ShapeDtypeStruct((batch, RET_HEADS, RET_QK_DIM, RET_V_DIM), F32)]
    scratch = [pltpu.VMEM((rows, dq), F32), pltpu.VMEM((rows, dq), F32), pltpu.VMEM((rows, dv), F32)]
    return pl.pallas_call(
        functools.partial(_odd_mix_kernel, bt=bt, lt=lt, c=c, has_state=has_state),
        grid=grid, in_specs=in_specs, out_specs=out_specs, out_shape=out_shape,
        scratch_shapes=scratch, compiler_params=_compiler_params(("arbitrary", "arbitrary")),
        name="odd_mix")(*args)


def _tiles(batch, length):
    if length % CHUNK == 0:
        lt = min(length, 256)
        return dict(bt_even=1, bt_odd=1, lt=lt, tm=min(batch * length, 256))
    return dict(bt_even=min(batch, 16), bt_odd=min(batch, 4), lt=length,
                tm=min(batch * length, 256))


def _run_trunk(x, states, pos, w):
    batch, length, d = x.shape
    t = _tiles(batch, length)
    depth = w["norm_mix"].shape[0]
    x = x.reshape(batch * length, d)
    conv_a_out, conv_qkv_out, delta_out, ret_out = [], [], [], []
    for i in range(depth):
        if i % 2 == 0:
            e = i // 2
            w_in = w["even_w_in"][e]
            n_main = w_in.shape[1] - 2 * DN_HEADS
            w_main = w_in[:, :n_main].astype(BF16)
            w_ab = jnp.pad(w_in[:, n_main:], ((0, 0), (0, LANES - 2 * DN_HEADS))).astype(BF16)
            p_main, p_ab = _norm_matmul(x, w["norm_mix"][i], [w_main, w_ab], t["tm"])
            st = None if states is None else (states[0], states[1], states[2])
            y, ca, cq, s = _even_mix(p_main, p_ab, w["even_conv_a"][e], w["even_conv_qkv"][e],
                                     w["even_a_log"][e], w["even_dt_bias"][e], w["even_dn_norm"][e],
                                     st, e, batch, length, t["bt_even"], t["lt"])
            conv_a_out.append(ca)
            conv_qkv_out.append(cq)
            delta_out.append(s)
            wo = w["even_w_out"][e].astype(BF16)
        else:
            o = i // 2
            (p,) = _norm_matmul(x, w["norm_mix"][i], [w["odd_w_in"][o].astype(BF16)], t["tm"])
            y, r = _odd_mix(p, w["odd_gn_g"][o], w["odd_gn_b"][o],
                            None if states is None else states[3], o, pos,
                            batch, length, t["bt_odd"], t["lt"])
            ret_out.append(r)
            wo = w["odd_w_out"][o].astype(BF16)
        x = _out_ffn(x, y, wo, w["norm_ffn"][i], w["ffn_w_gate"][i].astype(BF16),
                     w["ffn_w_up"][i].astype(BF16), w["ffn_w_down"][i].astype(BF16),
                     w["final_norm"], i == depth - 1, t["tm"])
    return (x.reshape(batch, length, d), jnp.stack(conv_a_out), jnp.stack(conv_qkv_out),
            jnp.stack(delta_out), jnp.stack(ret_out))


def kernel(x_prompt, x_sample, state_conv_a, state_conv_qkv, state_delta, state_ret, norm_mix,
           norm_ffn, final_norm, even_w_in, even_conv_a, even_conv_qkv, even_a_log, even_dt_bias,
           even_dn_norm, even_w_out, odd_w_in, odd_gn_g, odd_gn_b, odd_w_out, ffn_w_gate, ffn_w_up,
           ffn_w_down):
    w = dict(norm_mix=norm_mix, norm_ffn=norm_ffn, final_norm=final_norm, even_w_in=even_w_in,
             even_conv_a=even_conv_a, even_conv_qkv=even_conv_qkv, even_a_log=even_a_log,
             even_dt_bias=even_dt_bias, even_dn_norm=even_dn_norm, even_w_out=even_w_out,
             odd_w_in=odd_w_in, odd_gn_g=odd_gn_g, odd_gn_b=odd_gn_b, odd_w_out=odd_w_out,
             ffn_w_gate=ffn_w_gate, ffn_w_up=ffn_w_up, ffn_w_down=ffn_w_down)
    lp = x_prompt.shape[1]
    ls = x_sample.shape[1]
    pos_p = jnp.arange(lp, dtype=jnp.int32)
    y_prompt, ca_p, cq_p, d_p, r_p = _run_trunk(x_prompt, None, pos_p, w)
    pos_s = PAST_LEN + jnp.arange(ls, dtype=jnp.int32)
    y_sample, ca_s, cq_s, d_s, r_s = _run_trunk(
        x_sample, (state_conv_a, state_conv_qkv, state_delta, state_ret), pos_s, w)
    return (y_prompt, y_sample, ca_p, cq_p, d_p, r_p, ca_s, cq_s, d_s, r_s)
```

```python
import functools
import math

import jax
import jax.numpy as jnp
from jax import lax
from jax.experimental import pallas as pl
from jax.experimental.pallas import tpu as pltpu

F32 = jnp.float32
BF16 = jnp.bfloat16

EPS = 1e-6
GN_EPS = 1e-5
ROPE_BASE = 10000.0
PAST_LEN = 16384
CHUNK = 64

CONV_A_W = 3
DN_CONV_W = 4
DN_HEADS = 4
DN_HEAD_DIM = 128
RET_HEADS = 8
RET_QK_DIM = 128
RET_V_DIM = 256

LANES = 128
SUBLANES = 8
VMEM_LIMIT_BYTES = 56 * 1024 * 1024
DN_ROWS = 64


def _compiler_params(semantics):
    return pltpu.CompilerParams(dimension_semantics=semantics,
                                vmem_limit_bytes=VMEM_LIMIT_BYTES)


def _resident(shape):
    nd = len(shape)
    return pl.BlockSpec(shape, lambda *_: (0,) * nd, pipeline_mode=pl.Buffered(1))


def _mm(a, b):
    return jnp.dot(a.astype(BF16), b.astype(BF16), preferred_element_type=F32)


def _mm_nt(a, b):
    return lax.dot_general(a.astype(BF16), b.astype(BF16), (((1,), (1,)), ((), ())),
                           preferred_element_type=F32)


def _mm_tn(a, b):
    return lax.dot_general(a.astype(BF16), b.astype(BF16), (((0,), (0,)), ((), ())),
                           preferred_element_type=F32)


def _sigmoid(x):
    return 1.0 / (1.0 + jnp.exp(-x))


def _silu(x):
    return x * _sigmoid(x)


def _rmsnorm(x, g):
    return x * lax.rsqrt(jnp.mean(x * x, axis=-1, keepdims=True) + EPS) * g


def _norm_matmul_kernel(x_ref, g_ref, *refs):
    nw = len(refs) // 2
    h = _rmsnorm(x_ref[...], g_ref[...]).astype(BF16)
    for w_ref, o_ref in zip(refs[:nw], refs[nw:]):
        o_ref[...] = jnp.dot(h, w_ref[...], preferred_element_type=F32)


def _norm_matmul(x, g, weights, tm):
    m, d = x.shape
    grid = (m // tm,)
    in_specs = [pl.BlockSpec((tm, d), lambda i: (i, 0)), _resident((1, d))]
    in_specs += [_resident(w.shape) for w in weights]
    out_specs = [pl.BlockSpec((tm, w.shape[1]), lambda i: (i, 0)) for w in weights]
    out_shape = [jax.ShapeDtypeStruct((m, w.shape[1]), F32) for w in weights]
    return pl.pallas_call(
        _norm_matmul_kernel, grid=grid, in_specs=in_specs, out_specs=out_specs,
        out_shape=out_shape, compiler_params=_compiler_params(("arbitrary",)),
        name="norm_matmul")(x, g.reshape(1, d), *weights)


def _out_ffn_kernel(x_ref, y_ref, wo_ref, g_ref, wg_ref, wu_ref, wd_ref, gf_ref, o_ref, *, final):
    x1 = x_ref[...] + jnp.dot(y_ref[...], wo_ref[...], preferred_element_type=F32)
    h = _rmsnorm(x1, g_ref[...]).astype(BF16)
    gate = jnp.dot(h, wg_ref[...], preferred_element_type=F32)
    up = jnp.dot(h, wu_ref[...], preferred_element_type=F32)
    a = (_silu(gate) * up).astype(BF16)
    x2 = x1 + jnp.dot(a, wd_ref[...], preferred_element_type=F32)
    if final:
        x2 = _rmsnorm(x2, gf_ref[...])
    o_ref[...] = x2


def _out_ffn(x, y, wo, g, wg, wu, wd, gf, final, tm):
    m, d = x.shape
    grid = (m // tm,)
    in_specs = [pl.BlockSpec((tm, d), lambda i: (i, 0)),
                pl.BlockSpec((tm, y.shape[1]), lambda i: (i, 0)),
                _resident(wo.shape), _resident((1, d)), _resident(wg.shape),
                _resident(wu.shape), _resident(wd.shape), _resident((1, d))]
    return pl.pallas_call(
        functools.partial(_out_ffn_kernel, final=final), grid=grid, in_specs=in_specs,
        out_specs=pl.BlockSpec((tm, d), lambda i: (i, 0)),
        out_shape=jax.ShapeDtypeStruct((m, d), F32),
        compiler_params=_compiler_params(("arbitrary",)),
        name="out_ffn")(x, y, wo, g.reshape(1, d), wg, wu, wd, gf.reshape(1, d))


def _even_mix_kernel(*refs, bt, lt, c, has_state, n_alias):
    n_in = 7 + (3 if has_state else 0)
    (pm_ref, pab_ref, caw_ref, cqw_ref, alog_ref, dt_ref, dng_ref) = refs[:7]
    if has_state:
        ca_prev_ref, cq_prev_ref, s_prev_ref = refs[7:10]
    refs = refs[n_in + n_alias:]
    (y_ref, ca_new_ref, cq_new_ref, s_ref,
     ua_s, qkv_s, q_s, k_s, v_s, g_s, b_s, o_s, wq_s, u_s, kd_s, aqk_s, egl_s) = refs
    dn = DN_HEADS * DN_HEAD_DIM
    hd = DN_HEAD_DIM
    ca = ua_s.shape[-1]
    l = pl.program_id(1)
    nl = pl.num_programs(1)
    rows = bt * lt
    pad = SUBLANES

    @pl.when(l == 0)
    def _():
        ua_s[:, 0:pad, :] = jnp.zeros((bt, pad, ca), F32)
        qkv_s[:, 0:pad, :] = jnp.zeros((bt, pad, 3 * dn), F32)
        if has_state:
            ua_s[:, pad - (CONV_A_W - 1):pad, :] = ca_prev_ref[...]
            qkv_s[:, pad - (DN_CONV_W - 1):pad, :] = cq_prev_ref[...]
            s_ref[...] = s_prev_ref[...]
        else:
            s_ref[...] = jnp.zeros(s_ref.shape, F32)

    @pl.when(l > 0)
    def _():
        ua_s[:, 0:pad, :] = ua_s[:, lt:lt + pad, :]
        qkv_s[:, 0:pad, :] = qkv_s[:, lt:lt + pad, :]

    gate_b = pm_ref[:, 0:ca]
    ua_s[:, pad:pad + lt, :] = (pm_ref[:, ca:2 * ca] * pm_ref[:, 2 * ca:3 * ca]).reshape(bt, lt, ca)
    conv = caw_ref[0:1, :] * ua_s[:, pad - 2:pad - 2 + lt, :]
    for i in range(1, CONV_A_W):
        conv = conv + caw_ref[i:i + 1, :] * ua_s[:, pad - 2 + i:pad - 2 + i + lt, :]
    y_ref[:, 0:ca] = (gate_b * conv.reshape(rows, ca)).astype(y_ref.dtype)

    qkv_s[:, pad:pad + lt, :] = pm_ref[:, 3 * ca:3 * ca + 3 * dn].reshape(bt, lt, 3 * dn)
    off = pad - (DN_CONV_W - 1)
    conv = cqw_ref[0:1, :] * qkv_s[:, off:off + lt, :]
    for i in range(1, DN_CONV_W):
        conv = conv + cqw_ref[i:i + 1, :] * qkv_s[:, off + i:off + i + lt, :]
    qkv = _silu(conv).reshape(rows, 3 * dn)
    for h in range(DN_HEADS):
        sl = slice(h * hd, (h + 1) * hd)
        qh = qkv[:, h * hd:(h + 1) * hd]
        kh = qkv[:, dn + h * hd:dn + (h + 1) * hd]
        q_s[:, sl] = qh * (lax.rsqrt(jnp.sum(qh * qh, axis=-1, keepdims=True) + EPS) * (hd ** -0.5))
        k_s[:, sl] = kh * lax.rsqrt(jnp.sum(kh * kh, axis=-1, keepdims=True) + EPS)
    v_s[...] = qkv[:, 2 * dn:3 * dn]
    pab = pab_ref[...]
    sp = jnp.maximum(pab + dt_ref[...], 0.0) + jnp.log(1.0 + jnp.exp(-jnp.abs(pab + dt_ref[...])))
    g_s[...] = -jnp.exp(alog_ref[...]) * sp
    b_s[...] = _sigmoid(pab)

    sc = DN_ROWS
    nb = sc // c
    nsc = rows // sc
    shift = int(math.log2(c))
    ri = lax.broadcasted_iota(jnp.int32, (sc, sc), 0)
    ci = lax.broadcasted_iota(jnp.int32, (sc, sc), 1)
    rblk = lax.shift_right_logical(ri, shift)
    same = rblk == lax.shift_right_logical(ci, shift)
    upper = (ri <= ci) & same
    causal = (ri >= ci) & same
    strict = (ri > ci) & same
    eye = ri == ci
    last = ci == (lax.shift_left(rblk, shift) + (c - 1))
    eye_f = eye.astype(F32)
    n_sq = shift - 1

    def intra_all():
        units = [(i, h) for i in range(nsc) for h in range(DN_HEADS)]
        xs, ts, rhs, qes = {}, {}, {}, {}
        for (i, h) in units:
            rs = slice(i * sc, (i + 1) * sc)
            sl = slice(h * hd, (h + 1) * hd)
            q = q_s[rs, sl]
            k = k_s[rs, sl]
            v = v_s[rs, sl]
            beta = b_s[rs, DN_HEADS + h:DN_HEADS + h + 1]
            g_col = g_s[rs, h:h + 1]
            g_row = jnp.sum(jnp.where(upper, g_col, 0.0), axis=0, keepdims=True)
            g_rows = jnp.broadcast_to(g_row, (sc, sc))
            g_cum = jnp.sum(jnp.where(eye, g_rows, 0.0), axis=1, keepdims=True)
            g_last = jnp.sum(jnp.where(last, g_rows, 0.0), axis=1, keepdims=True)
            decay = jnp.exp(jnp.where(causal, g_cum - g_rows, -jnp.inf))
            kb = k * beta
            kq = _mm_nt(jnp.concatenate([kb, q], axis=0), k)
            e_g = jnp.exp(g_cum)
            xs[i, h] = -(kq[0:sc] * jnp.where(strict, decay, 0.0))
            ts[i, h] = eye_f + xs[i, h]
            rhs[i, h] = jnp.concatenate([kb * e_g, v * beta], axis=1)
            qes[i, h] = q * e_g
            kd_s[rs, sl] = k * jnp.exp(g_last - g_cum)
            aqk_s[i, h] = kq[sc:2 * sc] * decay
            egl_s[i, h] = jnp.broadcast_to(jnp.exp(g_last), (sc, hd))
        for _ in range(n_sq):
            for u in units:
                xs[u] = _mm(xs[u], xs[u])
            for u in units:
                ts[u] = ts[u] + _mm(ts[u], xs[u])
        for (i, h) in units:
            rs = slice(i * sc, (i + 1) * sc)
            sl = slice(h * hd, (h + 1) * hd)
            wu = _mm(ts[i, h], rhs[i, h])
            w = wu[:, 0:hd]
            qe = qes[i, h]
            for j in range(nb):
                wq_s[i * nb + j, h, 0:c] = w[j * c:(j + 1) * c]
                wq_s[i * nb + j, h, c:2 * c] = qe[j * c:(j + 1) * c]
            u_s[rs, sl] = wu[:, hd:2 * hd]

    def inter(i):
        rs = slice(i * sc, (i + 1) * sc)
        units = [(h, j) for h in range(DN_HEADS) for j in range(nb)]
        bidx = lambda j: (i * nb + j) if nb > 1 else 0
        rows_of = lambda j: slice(i * sc + j * c, i * sc + (j + 1) * c)
        cols_of = lambda h: slice(h * hd, (h + 1) * hd)
        ss = {(h, j): s_ref[bidx(j), h] for (h, j) in units}
        tqs = {(h, j): _mm(wq_s[i * nb + j, h], ss[h, j]) for (h, j) in units}
        us = {(h, j): u_s[rows_of(j), cols_of(h)] - tqs[h, j][0:c] for (h, j) in units}
        upd = {(h, j): _mm_tn(kd_s[rows_of(j), cols_of(h)], us[h, j]) for (h, j) in units}
        for (h, j) in units:
            s_ref[bidx(j), h] = ss[h, j] * egl_s[i, h, j * c:j * c + 1, :] + upd[h, j]
        for h in range(DN_HEADS):
            u = jnp.concatenate([us[h, j] for j in range(nb)], axis=0) if nb > 1 else us[h, 0]
            qs = (jnp.concatenate([tqs[h, j][c:2 * c] for j in range(nb)], axis=0) if nb > 1
                  else tqs[h, 0][c:2 * c])
            o_s[rs, cols_of(h)] = qs + _mm(aqk_s[i, h], u)

    intra_all()
    for i in range(nsc):
        inter(i)

    for h in range(DN_HEADS):
        sl = slice(h * hd, (h + 1) * hd)
        o = o_s[:, sl]
        z = pm_ref[:, 3 * ca + 3 * dn + h * hd:3 * ca + 3 * dn + (h + 1) * hd]
        o = o * lax.rsqrt(jnp.mean(o * o, axis=-1, keepdims=True) + EPS) * dng_ref[...] * _silu(z)
        y_ref[:, ca + h * hd:ca + (h + 1) * hd] = o.astype(y_ref.dtype)

    @pl.when(l == nl - 1)
    def _():
        ca_new_ref[...] = ua_s[:, pad + lt - (CONV_A_W - 1):pad + lt, :]
        cq_new_ref[...] = qkv_s[:, pad + lt - (DN_CONV_W - 1):pad + lt, :]


def _even_mix(p_main, p_ab, conv_a_w, conv_qkv_w, a_log, dt_bias, dn_norm, states, prev_out,
              layer, n_layers, batch, length, bt, lt):
    c = CHUNK if length % CHUNK == 0 else length
    ca = conv_a_w.shape[1]
    dn = DN_HEADS * DN_HEAD_DIM
    hd = DN_HEAD_DIM
    nl = length // lt
    rows = bt * lt
    assert rows % DN_ROWS == 0 and DN_ROWS % c == 0 and (c == DN_ROWS or lt == c)
    nsc = rows // DN_ROWS
    nb = DN_ROWS // c
    grid = (batch // bt, nl)
    has_state = states is not None
    row_map = lambda b, l: (b * nl + l, 0)
    alog_row = jnp.zeros((1, LANES), F32).at[0, :DN_HEADS].set(a_log)
    dt_row = jnp.zeros((1, LANES), F32).at[0, :DN_HEADS].set(dt_bias)
    in_specs = [pl.BlockSpec((rows, p_main.shape[1]), row_map),
                pl.BlockSpec((rows, LANES), row_map),
                _resident(conv_a_w.shape), _resident(conv_qkv_w.shape),
                _resident((1, LANES)), _resident((1, LANES)), _resident((1, hd))]
    args = [p_main, p_ab, conv_a_w, conv_qkv_w, alog_row, dt_row, dn_norm.reshape(1, hd)]
    if has_state:
        ca_prev, cq_prev, s_prev = states
        in_specs += [
            pl.BlockSpec((None, bt, CONV_A_W - 1, ca), lambda b, l: (layer, b, 0, 0)),
            pl.BlockSpec((None, bt, DN_CONV_W - 1, 3 * dn), lambda b, l: (layer, b, 0, 0)),
            pl.BlockSpec((None, bt, DN_HEADS, hd, hd), lambda b, l: (layer, b, 0, 0, 0))]
        args += [ca_prev, cq_prev, s_prev]
    aliases = {}
    if prev_out is not None:
        for k, a in enumerate(prev_out):
            aliases[len(args)] = 1 + k
            in_specs.append(pl.BlockSpec(memory_space=pl.ANY))
            args.append(a)
    out_specs = [pl.BlockSpec((rows, ca + dn), row_map),
                 pl.BlockSpec((None, bt, CONV_A_W - 1, ca), lambda b, l: (layer, b, 0, 0)),
                 pl.BlockSpec((None, bt, DN_CONV_W - 1, 3 * dn), lambda b, l: (layer, b, 0, 0)),
                 pl.BlockSpec((None, bt, DN_HEADS, hd, hd), lambda b, l: (layer, b, 0, 0, 0))]
    out_shape = [jax.ShapeDtypeStruct((batch * length, ca + dn), BF16),
                 jax.ShapeDtypeStruct((n_layers, batch, CONV_A_W - 1, ca), F32),
                 jax.ShapeDtypeStruct((n_layers, batch, DN_CONV_W - 1, 3 * dn), F32),
                 jax.ShapeDtypeStruct((n_layers, batch, DN_HEADS, hd, hd), F32)]
    scratch = [pltpu.VMEM((bt, lt + SUBLANES, ca), F32),
               pltpu.VMEM((bt, lt + SUBLANES, 3 * dn), F32),
               pltpu.VMEM((rows, dn), F32), pltpu.VMEM((rows, dn), F32), pltpu.VMEM((rows, dn), F32),
               pltpu.VMEM((rows, LANES), F32), pltpu.VMEM((rows, LANES), F32),
               pltpu.VMEM((rows, dn), F32),
               pltpu.VMEM((nsc * nb, DN_HEADS, 2 * c, hd), F32),
               pltpu.VMEM((rows, dn), F32), pltpu.VMEM((rows, dn), F32),
               pltpu.VMEM((nsc, DN_HEADS, DN_ROWS, DN_ROWS), F32),
               pltpu.VMEM((nsc, DN_HEADS, DN_ROWS, hd), F32)]
    outs = pl.pallas_call(
        functools.partial(_even_mix_kernel, bt=bt, lt=lt, c=c, has_state=has_state,
                          n_alias=len(aliases)),
        grid=grid, in_specs=in_specs, out_specs=out_specs, out_shape=out_shape,
        scratch_shapes=scratch, input_output_aliases=aliases,
        compiler_params=_compiler_params(("arbitrary", "arbitrary")),
        name="even_mix")(*args)
    return outs[0], tuple(outs[1:])


def _odd_mix_kernel(*refs, bt, lt, c, has_state, n_alias):
    n_in = 5 + (1 if has_state else 0)
    (p_ref, cos_ref, sin_ref, gng_ref, gnb_ref) = refs[:5]
    if has_state:
        r_prev_ref = refs[5]
    (y_ref, r_ref, q_s, k_s, o_s) = refs[n_in + n_alias:]
    dq = RET_HEADS * RET_QK_DIM
    dv = RET_HEADS * RET_V_DIM
    l = pl.program_id(1)

    @pl.when(l == 0)
    def _():
        if has_state:
            r_ref[...] = r_prev_ref[...]
        else:
            r_ref[...] = jnp.zeros(r_ref.shape, F32)

    cos2 = cos_ref[...]
    sin2 = sin_ref[...]
    for h in range(RET_HEADS):
        sl = slice(h * RET_QK_DIM, (h + 1) * RET_QK_DIM)
        q = p_ref[:, h * RET_QK_DIM:(h + 1) * RET_QK_DIM]
        k = p_ref[:, dq + h * RET_QK_DIM:dq + (h + 1) * RET_QK_DIM]
        q_s[:, sl] = q * cos2 + pltpu.roll(q, RET_QK_DIM // 2, 1) * sin2
        k_s[:, sl] = (k * cos2 + pltpu.roll(k, RET_QK_DIM // 2, 1) * sin2) * (RET_QK_DIM ** -0.5)

    ri = lax.broadcasted_iota(jnp.int32, (c, c), 0)
    ci = lax.broadcasted_iota(jnp.int32, (c, c), 1)
    diff = (ri - ci).astype(F32)
    idx_k = lax.broadcasted_iota(jnp.int32, (c, RET_QK_DIM), 0).astype(F32)
    idx_v = lax.broadcasted_iota(jnp.int32, (c, RET_V_DIM), 0).astype(F32)
    decay, inner, kdec, cdec = [], [], [], []
    for h in range(RET_HEADS):
        lg = math.log(1.0 - 2.0 ** (-5.0 - h))
        decay.append(jnp.where(diff >= 0, jnp.exp(jnp.maximum(diff, 0.0) * lg), 0.0))
        inner.append(jnp.exp((idx_v + 1.0) * lg))
        kdec.append(jnp.exp((c - 1.0 - idx_k) * lg))
        cdec.append(math.exp(c * lg))
    nc = lt // c

    heads = range(RET_HEADS)
    for i in range(bt * nc):
        b = i // nc
        rs = slice(i * c, (i + 1) * c)
        qs = [q_s[rs, h * RET_QK_DIM:(h + 1) * RET_QK_DIM] for h in heads]
        ks = [k_s[rs, h * RET_QK_DIM:(h + 1) * RET_QK_DIM] for h in heads]
        vs = [p_ref[rs, 2 * dq + h * RET_V_DIM:2 * dq + (h + 1) * RET_V_DIM] for h in heads]
        rr = [r_ref[b, h] for h in heads]
        att = [_mm_nt(qs[h], ks[h]) * decay[h] for h in heads]
        qr = [_mm(qs[h], rr[h]) * inner[h] for h in heads]
        kv = [_mm_tn(ks[h] * kdec[h], vs[h]) for h in heads]
        for h in heads:
            r_ref[b, h] = rr[h] * cdec[h] + kv[h]
        for h in heads:
            o_s[rs, h * RET_V_DIM:(h + 1) * RET_V_DIM] = _mm(att[h], vs[h]) + qr[h]

    for h in range(RET_HEADS):
        sl = slice(h * RET_V_DIM, (h + 1) * RET_V_DIM)
        o = o_s[:, sl]
        mu = jnp.mean(o, axis=-1, keepdims=True)
        var = jnp.mean(jnp.square(o - mu), axis=-1, keepdims=True)
        o = (o - mu) * lax.rsqrt(var + GN_EPS) * gng_ref[:, sl] + gnb_ref[:, sl]
        gate = p_ref[:, 2 * dq + dv + h * RET_V_DIM:2 * dq + dv + (h + 1) * RET_V_DIM]
        y_ref[:, sl] = (_silu(gate) * o).astype(y_ref.dtype)


def _rope_tables(pos):
    half = RET_QK_DIM // 2
    inv = ROPE_BASE ** (-jnp.arange(half, dtype=F32) / half)
    ang = pos.astype(F32)[:, None] * inv[None, :]
    cos = jnp.cos(ang)
    sin = jnp.sin(ang)
    return jnp.concatenate([cos, cos], axis=-1), jnp.concatenate([-sin, sin], axis=-1)


def _odd_mix(p, gn_g, gn_b, r_prev, prev_out, layer, n_layers, pos, batch, length, bt, lt):
    c = CHUNK if length % CHUNK == 0 else length
    dq = RET_HEADS * RET_QK_DIM
    dv = RET_HEADS * RET_V_DIM
    nl = length // lt
    rows = bt * lt
    grid = (batch // bt, nl)
    has_state = r_prev is not None
    row_map = lambda b, l: (b * nl + l, 0)
    cos2, sin2 = _rope_tables(pos)
    if bt > 1:
        cos2 = jnp.tile(cos2, (bt, 1))
        sin2 = jnp.tile(sin2, (bt, 1))
    tab_spec = pl.BlockSpec((rows, RET_QK_DIM), lambda b, l: (l, 0))
    in_specs = [pl.BlockSpec((rows, p.shape[1]), row_map), tab_spec, tab_spec,
                _resident((1, dv)), _resident((1, dv))]
    args = [p, cos2, sin2, gn_g.reshape(1, dv), gn_b.reshape(1, dv)]
    state_block = (None, bt, RET_HEADS, RET_QK_DIM, RET_V_DIM)
    state_map = lambda b, l: (layer, b, 0, 0, 0)
    if has_state:
        in_specs.append(pl.BlockSpec(state_block, state_map))
        args.append(r_prev)
    aliases = {}
    if prev_out is not None:
        aliases[len(args)] = 1
        in_specs.append(pl.BlockSpec(memory_space=pl.ANY))
        args.append(prev_out)
    out_specs = [pl.BlockSpec((rows, dv), row_map), pl.BlockSpec(state_block, state_map)]
    out_shape = [jax.ShapeDtypeStruct((batch * length, dv), BF16),
                 jax.ShapeDtypeStruct((n_layers, batch, RET_HEADS, RET_QK_DIM, RET_V_DIM), F32)]
    scratch = [pltpu.VMEM((rows, dq), F32), pltpu.VMEM((rows, dq), F32), pltpu.VMEM((rows, dv), F32)]
    return pl.pallas_call(
        functools.partial(_odd_mix_kernel, bt=bt, lt=lt, c=c, has_state=has_state,
                          n_alias=len(aliases)),
        grid=grid, in_specs=in_specs, out_specs=out_specs, out_shape=out_shape,
        scratch_shapes=scratch, input_output_aliases=aliases,
        compiler_params=_compiler_params(("arbitrary", "arbitrary")),
        name="odd_mix")(*args)


def _tiles(batch, length):
    if length % CHUNK == 0:
        lt = min(length, 256)
        return dict(bt_even=1, bt_odd=1, lt=lt, tm=min(batch * length, 256))
    return dict(bt_even=min(batch, 16), bt_odd=min(batch, 4), lt=length,
                tm=min(batch * length, 256))


def _run_trunk(x, states, pos, w):
    batch, length, d = x.shape
    t = _tiles(batch, length)
    depth = w["norm_mix"].shape[0]
    n_even = (depth + 1) // 2
    n_odd = depth // 2
    x = x.reshape(batch * length, d)
    even_out = None
    ret_out = None
    for i in range(depth):
        if i % 2 == 0:
            e = i // 2
            w_in = w["even_w_in"][e]
            n_main = w_in.shape[1] - 2 * DN_HEADS
            w_main = w_in[:, :n_main].astype(BF16)
            w_ab = jnp.pad(w_in[:, n_main:], ((0, 0), (0, LANES - 2 * DN_HEADS))).astype(BF16)
            p_main, p_ab = _norm_matmul(x, w["norm_mix"][i], [w_main, w_ab], t["tm"])
            st = None if states is None else (states[0], states[1], states[2])
            y, even_out = _even_mix(p_main, p_ab, w["even_conv_a"][e], w["even_conv_qkv"][e],
                                    w["even_a_log"][e], w["even_dt_bias"][e], w["even_dn_norm"][e],
                                    st, even_out, e, n_even, batch, length, t["bt_even"], t["lt"])
            wo = w["even_w_out"][e].astype(BF16)
        else:
            o = i // 2
            (p,) = _norm_matmul(x, w["norm_mix"][i], [w["odd_w_in"][o].astype(BF16)], t["tm"])
            y, ret_out = _odd_mix(p, w["odd_gn_g"][o], w["odd_gn_b"][o],
                                  None if states is None else states[3], ret_out, o, n_odd, pos,
                                  batch, length, t["bt_odd"], t["lt"])
            wo = w["odd_w_out"][o].astype(BF16)
        x = _out_ffn(x, y, wo, w["norm_ffn"][i], w["ffn_w_gate"][i].astype(BF16),
                     w["ffn_w_up"][i].astype(BF16), w["ffn_w_down"][i].astype(BF16),
                     w["final_norm"], i == depth - 1, t["tm"])
    return (x.reshape(batch, length, d),) + even_out + (ret_out,)


def kernel(x_prompt, x_sample, state_conv_a, state_conv_qkv, state_delta, state_ret, norm_mix,
           norm_ffn, final_norm, even_w_in, even_conv_a, even_conv_qkv, even_a_log, even_dt_bias,
           even_dn_norm, even_w_out, odd_w_in, odd_gn_g, odd_gn_b, odd_w_out, ffn_w_gate, ffn_w_up,
           ffn_w_down):
    w = dict(norm_mix=norm_mix, norm_ffn=norm_ffn, final_norm=final_norm, even_w_in=even_w_in,
             even_conv_a=even_conv_a, even_conv_qkv=even_conv_qkv, even_a_log=even_a_log,
             even_dt_bias=even_dt_bias, even_dn_norm=even_dn_norm, even_w_out=even_w_out,
             odd_w_in=odd_w_in, odd_gn_g=odd_gn_g, odd_gn_b=odd_gn_b, odd_w_out=odd_w_out,
             ffn_w_gate=ffn_w_gate, ffn_w_up=ffn_w_up, ffn_w_down=ffn_w_down)
    lp = x_prompt.shape[1]
    ls = x_sample.shape[1]
    pos_p = jnp.arange(lp, dtype=jnp.int32)
    y_prompt, ca_p, cq_p, d_p, r_p = _run_trunk(x_prompt, None, pos_p, w)
    pos_s = PAST_LEN + jnp.arange(ls, dtype=jnp.int32)
    y_sample, ca_s, cq_s, d_s, r_s = _run_trunk(
        x_sample, (state_conv_a, state_conv_qkv, state_delta, state_ret), pos_s, w)
    return (y_prompt, y_sample, ca_p, cq_p, d_p, r_p, ca_s, cq_s, d_s, r_s)
```

```python
import functools
import math

import jax
import jax.numpy as jnp
from jax import lax
from jax.experimental import pallas as pl
from jax.experimental.pallas import tpu as pltpu

F32 = jnp.float32
BF16 = jnp.bfloat16

EPS = 1e-6
GN_EPS = 1e-5
ROPE_BASE = 10000.0
PAST_LEN = 16384
CHUNK = 64

CONV_A_W = 3
DN_CONV_W = 4
DN_HEADS = 4
DN_HEAD_DIM = 128
RET_HEADS = 8
RET_QK_DIM = 128
RET_V_DIM = 256

LANES = 128
SUBLANES = 8
VMEM_LIMIT_BYTES = 56 * 1024 * 1024
DN_ROWS = 64


def _compiler_params(semantics):
    return pltpu.CompilerParams(dimension_semantics=semantics,
                                vmem_limit_bytes=VMEM_LIMIT_BYTES)


def _resident(shape):
    nd = len(shape)
    return pl.BlockSpec(shape, lambda *_: (0,) * nd, pipeline_mode=pl.Buffered(1))


def _mm(a, b):
    return jnp.dot(a.astype(BF16), b.astype(BF16), preferred_element_type=F32)


def _mm_nt(a, b):
    return lax.dot_general(a.astype(BF16), b.astype(BF16), (((1,), (1,)), ((), ())),
                           preferred_element_type=F32)


def _mm_tn(a, b):
    return lax.dot_general(a.astype(BF16), b.astype(BF16), (((0,), (0,)), ((), ())),
                           preferred_element_type=F32)


def _sigmoid(x):
    return 1.0 / (1.0 + jnp.exp(-x))


def _silu(x):
    return x * _sigmoid(x)


def _rmsnorm(x, g):
    return x * lax.rsqrt(jnp.mean(x * x, axis=-1, keepdims=True) + EPS) * g


def _norm_matmul_kernel(x_ref, g_ref, *refs):
    nw = len(refs) // 2
    h = _rmsnorm(x_ref[...], g_ref[...]).astype(BF16)
    for w_ref, o_ref in zip(refs[:nw], refs[nw:]):
        o_ref[...] = jnp.dot(h, w_ref[...], preferred_element_type=F32)


def _norm_matmul(x, g, weights, tm):
    m, d = x.shape
    grid = (m // tm,)
    in_specs = [pl.BlockSpec((tm, d), lambda i: (i, 0)), _resident((1, d))]
    in_specs += [_resident(w.shape) for w in weights]
    out_specs = [pl.BlockSpec((tm, w.shape[1]), lambda i: (i, 0)) for w in weights]
    out_shape = [jax.ShapeDtypeStruct((m, w.shape[1]), F32) for w in weights]
    return pl.pallas_call(
        _norm_matmul_kernel, grid=grid, in_specs=in_specs, out_specs=out_specs,
        out_shape=out_shape, compiler_params=_compiler_params(("arbitrary",)),
        name="norm_matmul")(x, g.reshape(1, d), *weights)


def _out_ffn_kernel(x_ref, y_ref, wo_ref, g_ref, wg_ref, wu_ref, wd_ref, gf_ref, o_ref, *, final):
    x1 = x_ref[...] + jnp.dot(y_ref[...].astype(BF16), wo_ref[...], preferred_element_type=F32)
    h = _rmsnorm(x1, g_ref[...]).astype(BF16)
    gate = jnp.dot(h, wg_ref[...], preferred_element_type=F32)
    up = jnp.dot(h, wu_ref[...], preferred_element_type=F32)
    a = (_silu(gate) * up).astype(BF16)
    x2 = x1 + jnp.dot(a, wd_ref[...], preferred_element_type=F32)
    if final:
        x2 = _rmsnorm(x2, gf_ref[...])
    o_ref[...] = x2


def _out_ffn(x, y, wo, g, wg, wu, wd, gf, final, tm):
    m, d = x.shape
    grid = (m // tm,)
    in_specs = [pl.BlockSpec((tm, d), lambda i: (i, 0)),
                pl.BlockSpec((tm, y.shape[1]), lambda i: (i, 0)),
                _resident(wo.shape), _resident((1, d)), _resident(wg.shape),
                _resident(wu.shape), _resident(wd.shape), _resident((1, d))]
    return pl.pallas_call(
        functools.partial(_out_ffn_kernel, final=final), grid=grid, in_specs=in_specs,
        out_specs=pl.BlockSpec((tm, d), lambda i: (i, 0)),
        out_shape=jax.ShapeDtypeStruct((m, d), F32),
        compiler_params=_compiler_params(("arbitrary",)),
        name="out_ffn")(x, y, wo, g.reshape(1, d), wg, wu, wd, gf.reshape(1, d))


def _even_mix_kernel(*refs, bt, lt, c, has_state, single_step, n_alias):
    n_in = 7 + (3 if has_state else 0)
    (pm_ref, pab_ref, caw_ref, cqw_ref, alog_ref, dt_ref, dng_ref) = refs[:7]
    if has_state:
        ca_prev_ref, cq_prev_ref, s_prev_ref = refs[7:10]
    refs = refs[n_in + n_alias:]
    (y_ref, ca_new_ref, cq_new_ref, s_ref,
     ua_s, qkv_s, q_s, k_s, v_s, g_s, b_s, o_s, wq_s, u_s, kd_s, aqk_s, egl_s) = refs
    dn = DN_HEADS * DN_HEAD_DIM
    hd = DN_HEAD_DIM
    ca = ua_s.shape[-1]
    l = pl.program_id(1)
    nl = pl.num_programs(1)
    rows = bt * lt
    pad = SUBLANES

    @pl.when(l == 0)
    def _():
        ua_s[:, 0:pad, :] = jnp.zeros((bt, pad, ca), F32)
        qkv_s[:, 0:pad, :] = jnp.zeros((bt, pad, 3 * dn), F32)
        if has_state:
            ua_s[:, pad - (CONV_A_W - 1):pad, :] = ca_prev_ref[...]
            qkv_s[:, pad - (DN_CONV_W - 1):pad, :] = cq_prev_ref[...]
            if not single_step:
                s_ref[...] = s_prev_ref[...]
        else:
            s_ref[...] = jnp.zeros(s_ref.shape, F32)

    @pl.when(l > 0)
    def _():
        ua_s[:, 0:pad, :] = ua_s[:, lt:lt + pad, :]
        qkv_s[:, 0:pad, :] = qkv_s[:, lt:lt + pad, :]

    gate_b = pm_ref[:, :, 0:ca]
    ua_s[:, pad:pad + lt, :] = pm_ref[:, :, ca:2 * ca] * pm_ref[:, :, 2 * ca:3 * ca]
    conv = caw_ref[0:1, :] * ua_s[:, pad - 2:pad - 2 + lt, :]
    for i in range(1, CONV_A_W):
        conv = conv + caw_ref[i:i + 1, :] * ua_s[:, pad - 2 + i:pad - 2 + i + lt, :]
    y_ref[:, :, 0:ca] = (gate_b * conv).astype(y_ref.dtype)

    qkv_s[:, pad:pad + lt, :] = pm_ref[:, :, 3 * ca:3 * ca + 3 * dn]
    off = pad - (DN_CONV_W - 1)
    conv = cqw_ref[0:1, :] * qkv_s[:, off:off + lt, :]
    for i in range(1, DN_CONV_W):
        conv = conv + cqw_ref[i:i + 1, :] * qkv_s[:, off + i:off + i + lt, :]
    qkv = _silu(conv).reshape(rows, 3 * dn)
    for h in range(DN_HEADS):
        sl = slice(h * hd, (h + 1) * hd)
        qh = qkv[:, h * hd:(h + 1) * hd]
        kh = qkv[:, dn + h * hd:dn + (h + 1) * hd]
        q_s[:, sl] = qh * (lax.rsqrt(jnp.sum(qh * qh, axis=-1, keepdims=True) + EPS) * (hd ** -0.5))
        k_s[:, sl] = kh * lax.rsqrt(jnp.sum(kh * kh, axis=-1, keepdims=True) + EPS)
    v_s[...] = qkv[:, 2 * dn:3 * dn]
    pab = pab_ref[...].reshape(rows, LANES)
    sp = jnp.maximum(pab + dt_ref[...], 0.0) + jnp.log(1.0 + jnp.exp(-jnp.abs(pab + dt_ref[...])))
    g_s[...] = -jnp.exp(alog_ref[...]) * sp
    b_s[...] = _sigmoid(pab)

    sc = DN_ROWS
    nb = sc // c
    nsc = rows // sc
    shift = int(math.log2(c))
    ri = lax.broadcasted_iota(jnp.int32, (sc, sc), 0)
    ci = lax.broadcasted_iota(jnp.int32, (sc, sc), 1)
    rblk = lax.shift_right_logical(ri, shift)
    same = rblk == lax.shift_right_logical(ci, shift)
    upper = (ri <= ci) & same
    causal = (ri >= ci) & same
    strict = (ri > ci) & same
    eye = ri == ci
    last = ci == (lax.shift_left(rblk, shift) + (c - 1))
    eye_f = eye.astype(F32)
    n_sq = shift - 1

    def intra_all():
        units = [(i, h) for i in range(nsc) for h in range(DN_HEADS)]
        xs, ts, rhs, qes = {}, {}, {}, {}
        for (i, h) in units:
            rs = slice(i * sc, (i + 1) * sc)
            sl = slice(h * hd, (h + 1) * hd)
            q = q_s[rs, sl]
            k = k_s[rs, sl]
            v = v_s[rs, sl]
            beta = b_s[rs, DN_HEADS + h:DN_HEADS + h + 1]
            g_col = g_s[rs, h:h + 1]
            g_row = jnp.sum(jnp.where(upper, g_col, 0.0), axis=0, keepdims=True)
            g_rows = jnp.broadcast_to(g_row, (sc, sc))
            g_cum = jnp.sum(jnp.where(eye, g_rows, 0.0), axis=1, keepdims=True)
            g_last = jnp.sum(jnp.where(last, g_rows, 0.0), axis=1, keepdims=True)
            decay = jnp.exp(jnp.where(causal, g_cum - g_rows, -jnp.inf))
            kb = k * beta
            kq = _mm_nt(jnp.concatenate([kb, q], axis=0), k)
            e_g = jnp.exp(g_cum)
            xs[i, h] = -(kq[0:sc] * jnp.where(strict, decay, 0.0))
            ts[i, h] = eye_f + xs[i, h]
            rhs[i, h] = jnp.concatenate([kb * e_g, v * beta], axis=1)
            qes[i, h] = q * e_g
            kd_s[rs, sl] = k * jnp.exp(g_last - g_cum)
            aqk_s[i, h] = kq[sc:2 * sc] * decay
            egl_s[i, h] = jnp.broadcast_to(jnp.exp(g_last), (sc, hd))
        for _ in range(n_sq):
            for u in units:
                xs[u] = _mm(xs[u], xs[u])
            for u in units:
                ts[u] = ts[u] + _mm(ts[u], xs[u])
        for (i, h) in units:
            rs = slice(i * sc, (i + 1) * sc)
            sl = slice(h * hd, (h + 1) * hd)
            wu = _mm(ts[i, h], rhs[i, h])
            w = wu[:, 0:hd]
            qe = qes[i, h]
            for j in range(nb):
                wq_s[i * nb + j, h, 0:c] = w[j * c:(j + 1) * c]
                wq_s[i * nb + j, h, c:2 * c] = qe[j * c:(j + 1) * c]
            u_s[rs, sl] = wu[:, hd:2 * hd]

    groups_per_seq = max(lt // sc, 1)
    s_in = s_prev_ref if (has_state and single_step) else s_ref

    def inter(groups):
        units = [(i, h, j) for i in groups for h in range(DN_HEADS) for j in range(nb)]
        bidx = lambda i, j: (i * nb + j) if nb > 1 else i // groups_per_seq
        rows_of = lambda i, j: slice(i * sc + j * c, i * sc + (j + 1) * c)
        cols_of = lambda h: slice(h * hd, (h + 1) * hd)
        ss = {(i, h, j): s_in[bidx(i, j), h] for (i, h, j) in units}
        tqs = {(i, h, j): _mm(wq_s[i * nb + j, h], ss[i, h, j]) for (i, h, j) in units}
        us = {(i, h, j): u_s[rows_of(i, j), cols_of(h)] - tqs[i, h, j][0:c] for (i, h, j) in units}
        upd = {(i, h, j): _mm_tn(kd_s[rows_of(i, j), cols_of(h)], us[i, h, j])
               for (i, h, j) in units}
        for (i, h, j) in units:
            s_ref[bidx(i, j), h] = (ss[i, h, j] * egl_s[i, h, j * c:j * c + 1, :] + upd[i, h, j])
        for i in groups:
            for h in range(DN_HEADS):
                if nb > 1:
                    u = jnp.concatenate([us[i, h, j] for j in range(nb)], axis=0)
                    qs = jnp.concatenate([tqs[i, h, j][c:2 * c] for j in range(nb)], axis=0)
                else:
                    u = us[i, h, 0]
                    qs = tqs[i, h, 0][c:2 * c]
                o_s[i * sc:(i + 1) * sc, cols_of(h)] = qs + _mm(aqk_s[i, h], u)

    intra_all()
    if nb > 1:
        for i in range(nsc):
            inter([i])
    else:
        for w in range(groups_per_seq):
            inter([b * groups_per_seq + w for b in range(bt)])

    for h in range(DN_HEADS):
        sl = slice(h * hd, (h + 1) * hd)
        o = o_s[:, sl]
        z = pm_ref[:, :, 3 * ca + 3 * dn + h * hd:3 * ca + 3 * dn + (h + 1) * hd].reshape(rows, hd)
        o = o * lax.rsqrt(jnp.mean(o * o, axis=-1, keepdims=True) + EPS) * dng_ref[...] * _silu(z)
        y_ref[:, :, ca + h * hd:ca + (h + 1) * hd] = o.reshape(bt, lt, hd).astype(y_ref.dtype)

    @pl.when(l == nl - 1)
    def _():
        ca_new_ref[...] = ua_s[:, pad + lt - (CONV_A_W - 1):pad + lt, :]
        cq_new_ref[...] = qkv_s[:, pad + lt - (DN_CONV_W - 1):pad + lt, :]


def _even_mix(p_main, p_ab, conv_a_w, conv_qkv_w, a_log, dt_bias, dn_norm, states, prev_out,
              layer, n_layers, batch, length, bt, lt):
    c = CHUNK if length % CHUNK == 0 else length
    ca = conv_a_w.shape[1]
    dn = DN_HEADS * DN_HEAD_DIM
    hd = DN_HEAD_DIM
    nl = length // lt
    rows = bt * lt
    assert rows % DN_ROWS == 0 and DN_ROWS % c == 0 and (c == DN_ROWS or lt == c)
    nsc = rows // DN_ROWS
    nb = DN_ROWS // c
    grid = (batch // bt, nl)
    has_state = states is not None
    tok_map = lambda b, l: (b, l, 0)
    y_dtype = BF16 if lt % (2 * SUBLANES) == 0 else F32
    alog_row = jnp.zeros((1, LANES), F32).at[0, :DN_HEADS].set(a_log)
    dt_row = jnp.zeros((1, LANES), F32).at[0, :DN_HEADS].set(dt_bias)
    in_specs = [pl.BlockSpec((bt, lt, p_main.shape[1]), tok_map),
                pl.BlockSpec((bt, lt, LANES), tok_map),
                _resident(conv_a_w.shape), _resident(conv_qkv_w.shape),
                _resident((1, LANES)), _resident((1, LANES)), _resident((1, hd))]
    args = [p_main.reshape(batch, length, -1), p_ab.reshape(batch, length, LANES), conv_a_w,
            conv_qkv_w, alog_row, dt_row, dn_norm.reshape(1, hd)]
    if has_state:
        ca_prev, cq_prev, s_prev = states
        in_specs += [
            pl.BlockSpec((None, bt, CONV_A_W - 1, ca), lambda b, l: (layer, b, 0, 0)),
            pl.BlockSpec((None, bt, DN_CONV_W - 1, 3 * dn), lambda b, l: (layer, b, 0, 0)),
            pl.BlockSpec((None, bt, DN_HEADS, hd, hd), lambda b, l: (layer, b, 0, 0, 0))]
        args += [ca_prev, cq_prev, s_prev]
    aliases = {}
    if prev_out is not None:
        for k, a in enumerate(prev_out):
            aliases[len(args)] = 1 + k
            in_specs.append(pl.BlockSpec(memory_space=pl.ANY))
            args.append(a)
    out_specs = [pl.BlockSpec((bt, lt, ca + dn), tok_map),
                 pl.BlockSpec((None, bt, CONV_A_W - 1, ca), lambda b, l: (layer, b, 0, 0)),
                 pl.BlockSpec((None, bt, DN_CONV_W - 1, 3 * dn), lambda b, l: (layer, b, 0, 0)),
                 pl.BlockSpec((None, bt, DN_HEADS, hd, hd), lambda b, l: (layer, b, 0, 0, 0))]
    out_shape = [jax.ShapeDtypeStruct((batch, length, ca + dn), y_dtype),
                 jax.ShapeDtypeStruct((n_layers, batch, CONV_A_W - 1, ca), F32),
                 jax.ShapeDtypeStruct((n_layers, batch, DN_CONV_W - 1, 3 * dn), F32),
                 jax.ShapeDtypeStruct((n_layers, batch, DN_HEADS, hd, hd), F32)]
    scratch = [pltpu.VMEM((bt, lt + SUBLANES, ca), F32),
               pltpu.VMEM((bt, lt + SUBLANES, 3 * dn), F32),
               pltpu.VMEM((rows, dn), F32), pltpu.VMEM((rows, dn), F32), pltpu.VMEM((rows, dn), F32),
               pltpu.VMEM((rows, LANES), F32), pltpu.VMEM((rows, LANES), F32),
               pltpu.VMEM((rows, dn), F32),
               pltpu.VMEM((nsc * nb, DN_HEADS, 2 * c, hd), F32),
               pltpu.VMEM((rows, dn), F32), pltpu.VMEM((rows, dn), F32),
               pltpu.VMEM((nsc, DN_HEADS, DN_ROWS, DN_ROWS), F32),
               pltpu.VMEM((nsc, DN_HEADS, DN_ROWS, hd), F32)]
    outs = pl.pallas_call(
        functools.partial(_even_mix_kernel, bt=bt, lt=lt, c=c, has_state=has_state,
                          single_step=nl == 1, n_alias=len(aliases)),
        grid=grid, in_specs=in_specs, out_specs=out_specs, out_shape=out_shape,
        scratch_shapes=scratch, input_output_aliases=aliases,
        compiler_params=_compiler_params(("arbitrary", "arbitrary")),
        name="even_mix")(*args)
    return outs[0].reshape(batch * length, ca + dn), tuple(outs[1:])


def _odd_mix_kernel(*refs, bt, lt, has_state, single_step, n_alias):
    n_in = 5 + (1 if has_state else 0)
    (p_ref, cos_ref, sin_ref, gng_ref, gnb_ref) = refs[:5]
    if has_state:
        r_prev_ref = refs[5]
    (y_ref, r_ref, q_s, k_s, o_s, dec_s, inn_s, kdec_s) = refs[n_in + n_alias:]
    c = lt
    dq = RET_HEADS * RET_QK_DIM
    dv = RET_HEADS * RET_V_DIM
    l = pl.program_id(1)

    lgs = [math.log(1.0 - 2.0 ** (-5.0 - h)) for h in range(RET_HEADS)]

    @pl.when((pl.program_id(0) == 0) & (l == 0))
    def _():
        ri = lax.broadcasted_iota(jnp.int32, (c, c), 0)
        ci = lax.broadcasted_iota(jnp.int32, (c, c), 1)
        diff = (ri - ci).astype(F32)
        idx = lax.broadcasted_iota(jnp.int32, (c, RET_QK_DIM), 0).astype(F32)
        for h in range(RET_HEADS):
            dec_s[h] = jnp.where(diff >= 0, jnp.exp(jnp.maximum(diff, 0.0) * lgs[h]), 0.0)
            inn_s[h] = jnp.exp((idx + 1.0) * lgs[h])
            kdec_s[h] = jnp.exp((c - 1.0 - idx) * lgs[h])

    @pl.when(l == 0)
    def _():
        if not has_state:
            r_ref[...] = jnp.zeros(r_ref.shape, F32)
        elif not single_step:
            r_ref[...] = r_prev_ref[...]

    cos2 = cos_ref[...]
    sin2 = sin_ref[...]
    for h in range(RET_HEADS):
        sl = slice(h * RET_QK_DIM, (h + 1) * RET_QK_DIM)
        q = p_ref[:, h * RET_QK_DIM:(h + 1) * RET_QK_DIM]
        k = p_ref[:, dq + h * RET_QK_DIM:dq + (h + 1) * RET_QK_DIM]
        q_s[:, sl] = q * cos2 + pltpu.roll(q, RET_QK_DIM // 2, 1) * sin2
        k_s[:, sl] = (k * cos2 + pltpu.roll(k, RET_QK_DIM // 2, 1) * sin2) * (RET_QK_DIM ** -0.5)

    hg = 2 if c >= LANES else RET_HEADS
    r_in = r_prev_ref if (has_state and single_step) else r_ref
    for b in range(bt):
        rs = slice(b * c, (b + 1) * c)
        for h0 in range(0, RET_HEADS, hg):
            heads = range(h0, h0 + hg)
            qs = {h: q_s[rs, h * RET_QK_DIM:(h + 1) * RET_QK_DIM] for h in heads}
            ks = {h: k_s[rs, h * RET_QK_DIM:(h + 1) * RET_QK_DIM] for h in heads}
            vs = {h: p_ref[rs, 2 * dq + h * RET_V_DIM:2 * dq + (h + 1) * RET_V_DIM] for h in heads}
            rr = {h: r_in[b, h] for h in heads}
            att = {h: _mm_nt(qs[h], ks[h]) * dec_s[h] for h in heads}
            qr = {h: _mm(qs[h] * inn_s[h], rr[h]) for h in heads}
            kv = {h: _mm_tn(ks[h] * kdec_s[h], vs[h]) for h in heads}
            for h in heads:
                r_ref[b, h] = rr[h] * math.exp(c * lgs[h]) + kv[h]
            for h in heads:
                o_s[rs, h * RET_V_DIM:(h + 1) * RET_V_DIM] = _mm(att[h], vs[h]) + qr[h]

    for h in range(RET_HEADS):
        sl = slice(h * RET_V_DIM, (h + 1) * RET_V_DIM)
        o = o_s[:, sl]
        mu = jnp.mean(o, axis=-1, keepdims=True)
        var = jnp.mean(jnp.square(o - mu), axis=-1, keepdims=True)
        o = (o - mu) * lax.rsqrt(var + GN_EPS) * gng_ref[:, sl] + gnb_ref[:, sl]
        gate = p_ref[:, 2 * dq + dv + h * RET_V_DIM:2 * dq + dv + (h + 1) * RET_V_DIM]
        y_ref[:, sl] = (_silu(gate) * o).astype(y_ref.dtype)


def _rope_tables(pos):
    half = RET_QK_DIM // 2
    inv = ROPE_BASE ** (-jnp.arange(half, dtype=F32) / half)
    ang = pos.astype(F32)[:, None] * inv[None, :]
    cos = jnp.cos(ang)
    sin = jnp.sin(ang)
    return jnp.concatenate([cos, cos], axis=-1), jnp.concatenate([-sin, sin], axis=-1)


def _odd_mix(p, gn_g, gn_b, r_prev, prev_out, layer, n_layers, pos, batch, length, bt, lt):
    dq = RET_HEADS * RET_QK_DIM
    dv = RET_HEADS * RET_V_DIM
    nl = length // lt
    rows = bt * lt
    assert bt == 1 or nl == 1
    grid = (batch // bt, nl)
    has_state = r_prev is not None
    row_map = lambda b, l: (b * nl + l, 0)
    cos2, sin2 = _rope_tables(pos)
    if bt > 1:
        cos2 = jnp.tile(cos2, (bt, 1))
        sin2 = jnp.tile(sin2, (bt, 1))
    tab_spec = pl.BlockSpec((rows, RET_QK_DIM), lambda b, l: (l, 0))
    in_specs = [pl.BlockSpec((rows, p.shape[1]), row_map), tab_spec, tab_spec,
                _resident((1, dv)), _resident((1, dv))]
    args = [p, cos2, sin2, gn_g.reshape(1, dv), gn_b.reshape(1, dv)]
    state_block = (None, bt, RET_HEADS, RET_QK_DIM, RET_V_DIM)
    state_map = lambda b, l: (layer, b, 0, 0, 0)
    if has_state:
        in_specs.append(pl.BlockSpec(state_block, state_map))
        args.append(r_prev)
    aliases = {}
    if prev_out is not None:
        aliases[len(args)] = 1
        in_specs.append(pl.BlockSpec(memory_space=pl.ANY))
        args.append(prev_out)
    out_specs = [pl.BlockSpec((rows, dv), row_map), pl.BlockSpec(state_block, state_map)]
    out_shape = [jax.ShapeDtypeStruct((batch * length, dv), BF16),
                 jax.ShapeDtypeStruct((n_layers, batch, RET_HEADS, RET_QK_DIM, RET_V_DIM), F32)]
    scratch = [pltpu.VMEM((rows, dq), F32), pltpu.VMEM((rows, dq), F32), pltpu.VMEM((rows, dv), F32),
               pltpu.VMEM((RET_HEADS, lt, lt), F32),
               pltpu.VMEM((RET_HEADS, lt, RET_QK_DIM), F32),
               pltpu.VMEM((RET_HEADS, lt, RET_QK_DIM), F32)]
    return pl.pallas_call(
        functools.partial(_odd_mix_kernel, bt=bt, lt=lt, has_state=has_state,
                          single_step=nl == 1, n_alias=len(aliases)),
        grid=grid, in_specs=in_specs, out_specs=out_specs, out_shape=out_shape,
        scratch_shapes=scratch, input_output_aliases=aliases,
        compiler_params=_compiler_params(("arbitrary", "arbitrary")),
        name="odd_mix")(*args)


def _tiles(batch, length):
    if length % CHUNK == 0:
        lt = min(length, 256)
        return dict(bt_even=2 if batch % 2 == 0 else 1, bt_odd=1, lt=lt,
                    tm=min(batch * length, 512))
    return dict(bt_even=min(batch, 16), bt_odd=min(batch, 4), lt=length,
                tm=min(batch * length, 512))


def _run_trunk(x, states, pos, w):
    batch, length, d = x.shape
    t = _tiles(batch, length)
    depth = w["norm_mix"].shape[0]
    n_even = (depth + 1) // 2
    n_odd = depth // 2
    x = x.reshape(batch * length, d)
    even_out = None
    ret_out = None
    for i in range(depth):
        if i % 2 == 0:
            e = i // 2
            w_in = w["even_w_in"][e]
            n_main = w_in.shape[1] - 2 * DN_HEADS
            w_main = w_in[:, :n_main].astype(BF16)
            w_ab = jnp.pad(w_in[:, n_main:], ((0, 0), (0, LANES - 2 * DN_HEADS))).astype(BF16)
            p_main, p_ab = _norm_matmul(x, w["norm_mix"][i], [w_main, w_ab], t["tm"])
            st = None if states is None else (states[0], states[1], states[2])
            y, even_out = _even_mix(p_main, p_ab, w["even_conv_a"][e], w["even_conv_qkv"][e],
                                    w["even_a_log"][e], w["even_dt_bias"][e], w["even_dn_norm"][e],
                                    st, even_out, e, n_even, batch, length, t["bt_even"], t["lt"])
            wo = w["even_w_out"][e].astype(BF16)
        else:
            o = i // 2
            (p,) = _norm_matmul(x, w["norm_mix"][i], [w["odd_w_in"][o].astype(BF16)], t["tm"])
            y, ret_out = _odd_mix(p, w["odd_gn_g"][o], w["odd_gn_b"][o],
                                  None if states is None else states[3], ret_out, o, n_odd, pos,
                                  batch, length, t["bt_odd"], t["lt"])
            wo = w["odd_w_out"][o].astype(BF16)
        x = _out_ffn(x, y, wo, w["norm_ffn"][i], w["ffn_w_gate"][i].astype(BF16),
                     w["ffn_w_up"][i].astype(BF16), w["ffn_w_down"][i].astype(BF16),
                     w["final_norm"], i == depth - 1, t["tm"])
    return (x.reshape(batch, length, d),) + even_out + (ret_out,)


def kernel(x_prompt, x_sample, state_conv_a, state_conv_qkv, state_delta, state_ret, norm_mix,
           norm_ffn, final_norm, even_w_in, even_conv_a, even_conv_qkv, even_a_log, even_dt_bias,
           even_dn_norm, even_w_out, odd_w_in, odd_gn_g, odd_gn_b, odd_w_out, ffn_w_gate, ffn_w_up,
           ffn_w_down):
    w = dict(norm_mix=norm_mix, norm_ffn=norm_ffn, final_norm=final_norm, even_w_in=even_w_in,
             even_conv_a=even_conv_a, even_conv_qkv=even_conv_qkv, even_a_log=even_a_log,
             even_dt_bias=even_dt_bias, even_dn_norm=even_dn_norm, even_w_out=even_w_out,
             odd_w_in=odd_w_in, odd_gn_g=odd_gn_g, odd_gn_b=odd_gn_b, odd_w_out=odd_w_out,
             ffn_w_gate=ffn_w_gate, ffn_w_up=ffn_w_up, ffn_w_down=ffn_w_down)
    lp = x_prompt.shape[1]
    ls = x_sample.shape[1]
    pos_p = jnp.arange(lp, dtype=jnp.int32)
    y_prompt, ca_p, cq_p, d_p, r_p = _run_trunk(x_prompt, None, pos_p, w)
    pos_s = PAST_LEN + jnp.arange(ls, dtype=jnp.int32)
    y_sample, ca_s, cq_s, d_s, r_s = _run_trunk(
        x_sample, (state_conv_a, state_conv_qkv, state_delta, state_ret), pos_s, w)
    return (y_prompt, y_sample, ca_p, cq_p, d_p, r_p, ca_s, cq_s, d_s, r_s)
```

```python
import functools
import math

import jax
import jax.numpy as jnp
from jax import lax
from jax.experimental import pallas as pl
from jax.experimental.pallas import tpu as pltpu

F32 = jnp.float32
BF16 = jnp.bfloat16

EPS = 1e-6
GN_EPS = 1e-5
ROPE_BASE = 10000.0
PAST_LEN = 16384
CHUNK = 64

CONV_A_W = 3
DN_CONV_W = 4
DN_HEADS = 4
DN_HEAD_DIM = 128
RET_HEADS = 8
RET_QK_DIM = 128
RET_V_DIM = 256

LANES = 128
SUBLANES = 8
VMEM_LIMIT_BYTES = 56 * 1024 * 1024
DN_ROWS = 64
ROW_TILE = 512


def _compiler_params(semantics):
    return pltpu.CompilerParams(dimension_semantics=semantics,
                                vmem_limit_bytes=VMEM_LIMIT_BYTES)


def _resident(shape):
    nd = len(shape)
    return pl.BlockSpec(shape, lambda *_: (0,) * nd, pipeline_mode=pl.Buffered(1))


def _mm(a, b):
    return jnp.dot(a.astype(BF16), b.astype(BF16), preferred_element_type=F32)


def _mm_nt(a, b):
    return lax.dot_general(a.astype(BF16), b.astype(BF16), (((1,), (1,)), ((), ())),
                           preferred_element_type=F32)


def _mm_tn(a, b):
    return lax.dot_general(a.astype(BF16), b.astype(BF16), (((0,), (0,)), ((), ())),
                           preferred_element_type=F32)


def _sigmoid(x):
    return 0.5 * jnp.tanh(0.5 * x) + 0.5


def _silu(x):
    return x * _sigmoid(x)


def _rmsnorm(x, g):
    return x * lax.rsqrt(jnp.mean(x * x, axis=-1, keepdims=True) + EPS) * g


def _token_dtype(lt):
    return BF16 if lt % (2 * SUBLANES) == 0 else F32


def _in_even_kernel(*refs, bt, lt, has_state, n_alias):
    n_in = 8 + (2 if has_state else 0)
    (x_ref, g_ref, wm_ref, wab_ref, caw_ref, cqw_ref, alog_ref, dt_ref) = refs[:8]
    if has_state:
        ca_prev_ref, cq_prev_ref = refs[8:10]
    (ya_ref, q_ref, k_ref, v_ref, sz_ref, gb_ref, ca_new_ref, cq_new_ref,
     ua_s, qkv_s) = refs[n_in + n_alias:]
    hd = DN_HEAD_DIM
    dn = DN_HEADS * hd
    ca = ua_s.shape[-1]
    d = x_ref.shape[-1]
    l = pl.program_id(1)
    nl = pl.num_programs(1)
    rows = bt * lt
    pad = SUBLANES

    @pl.when(l == 0)
    def _():
        ua_s[:, 0:pad, :] = jnp.zeros((bt, pad, ca), F32)
        qkv_s[:, 0:pad, :] = jnp.zeros((bt, pad, 3 * dn), F32)
        if has_state:
            ua_s[:, pad - (CONV_A_W - 1):pad, :] = ca_prev_ref[...]
            qkv_s[:, pad - (DN_CONV_W - 1):pad, :] = cq_prev_ref[...]

    @pl.when(l > 0)
    def _():
        ua_s[:, 0:pad, :] = ua_s[:, lt:lt + pad, :]
        qkv_s[:, 0:pad, :] = qkv_s[:, lt:lt + pad, :]

    h = _rmsnorm(x_ref[...].reshape(rows, d), g_ref[...]).astype(BF16)

    def proj(lo, hi):
        return jnp.dot(h, wm_ref[:, lo:hi], preferred_element_type=F32)

    off = pad - (DN_CONV_W - 1)
    qkv0 = 3 * ca

    def proj_qkv(part):
        cols = slice(part * dn, (part + 1) * dn)
        qkv_s[:, pad:pad + lt, cols] = proj(qkv0 + part * dn, qkv0 + (part + 1) * dn).reshape(
            bt, lt, dn)

    def delayed(scr, back, cols):
        if back == 0 or bt > 1:
            return scr[:, pad - back:pad - back + lt, cols]
        rolled = pltpu.roll(scr[0, :, cols], back, 0)
        return rolled[pad:pad + lt].reshape(1, lt, rolled.shape[-1])

    def conv_silu(part):
        cols = slice(part * dn, (part + 1) * dn)
        conv = cqw_ref[0:1, cols] * delayed(qkv_s, DN_CONV_W - 1, cols)
        for i in range(1, DN_CONV_W):
            conv = conv + cqw_ref[i:i + 1, cols] * delayed(qkv_s, DN_CONV_W - 1 - i, cols)
        return _silu(conv)

    def l2norm_to(t, out_ref, scale):
        for hh in range(DN_HEADS):
            sl = slice(hh * hd, (hh + 1) * hd)
            th = t[:, :, sl]
            out_ref[:, :, sl] = th * (lax.rsqrt(jnp.sum(th * th, axis=-1, keepdims=True) + EPS)
                                      * scale)

    proj_qkv(0)
    proj_qkv(1)
    l2norm_to(conv_silu(0), q_ref, hd ** -0.5)
    proj_qkv(2)
    l2norm_to(conv_silu(1), k_ref, 1.0)
    cu = proj(ca, 3 * ca)
    v_ref[...] = conv_silu(2)
    gate_b = proj(0, ca)

    ua_s[:, pad:pad + lt, :] = (cu[:, 0:ca] * cu[:, ca:2 * ca]).reshape(bt, lt, ca)
    conv = caw_ref[0:1, :] * delayed(ua_s, CONV_A_W - 1, slice(0, ca))
    for i in range(1, CONV_A_W):
        conv = conv + caw_ref[i:i + 1, :] * delayed(ua_s, CONV_A_W - 1 - i, slice(0, ca))

    z = proj(3 * ca + 3 * dn, 3 * ca + 4 * dn)
    pab = jnp.dot(h, wab_ref[...], preferred_element_type=F32)

    ya_ref[...] = (gate_b.reshape(bt, lt, ca) * conv).astype(ya_ref.dtype)
    sz_ref[...] = _silu(z).reshape(bt, lt, dn)
    sp = jnp.maximum(pab + dt_ref[...], 0.0) + jnp.log(1.0 + jnp.exp(-jnp.abs(pab + dt_ref[...])))
    col = lax.broadcasted_iota(jnp.int32, pab.shape, 1)
    gb = jnp.where(col < DN_HEADS, -jnp.exp(alog_ref[...]) * sp, _sigmoid(pab))
    gb_ref[...] = gb.reshape(bt, lt, LANES)

    @pl.when(l == nl - 1)
    def _():
        ca_new_ref[...] = ua_s[:, pad + lt - (CONV_A_W - 1):pad + lt, :]
        cq_new_ref[...] = qkv_s[:, pad + lt - (DN_CONV_W - 1):pad + lt, :]


def _in_even(x, g, w_main, w_ab, conv_a_w, conv_qkv_w, a_log, dt_bias, states, prev_out,
             layer, n_layers, bt, lt):
    batch, length, d = x.shape
    ca = conv_a_w.shape[1]
    dn = DN_HEADS * DN_HEAD_DIM
    nl = length // lt
    grid = (batch // bt, nl)
    has_state = states is not None
    tok_map = lambda b, l: (b, l, 0)
    tok = lambda width: pl.BlockSpec((bt, lt, width), tok_map)
    alog_row = jnp.zeros((1, LANES), F32).at[0, :DN_HEADS].set(a_log)
    dt_row = jnp.zeros((1, LANES), F32).at[0, :DN_HEADS].set(dt_bias)
    in_specs = [tok(d), _resident((1, d)), _resident(w_main.shape), _resident(w_ab.shape),
                _resident(conv_a_w.shape), _resident(conv_qkv_w.shape),
                _resident((1, LANES)), _resident((1, LANES))]
    args = [x, g.reshape(1, d), w_main, w_ab, conv_a_w, conv_qkv_w, alog_row, dt_row]
    ca_block = pl.BlockSpec((None, bt, CONV_A_W - 1, ca), lambda b, l: (layer, b, 0, 0))
    cq_block = pl.BlockSpec((None, bt, DN_CONV_W - 1, 3 * dn), lambda b, l: (layer, b, 0, 0))
    if has_state:
        in_specs += [ca_block, cq_block]
        args += list(states)
    aliases = {}
    if prev_out is not None:
        for j, a in enumerate(prev_out):
            aliases[len(args)] = 6 + j
            in_specs.append(pl.BlockSpec(memory_space=pl.ANY))
            args.append(a)
    out_specs = [tok(ca), tok(dn), tok(dn), tok(dn), tok(dn), tok(LANES), ca_block, cq_block]
    tokshape = lambda width, dt: jax.ShapeDtypeStruct((batch, length, width), dt)
    out_shape = [tokshape(ca, _token_dtype(lt)), tokshape(dn, F32), tokshape(dn, F32),
                 tokshape(dn, F32), tokshape(dn, F32), tokshape(LANES, F32),
                 jax.ShapeDtypeStruct((n_layers, batch, CONV_A_W - 1, ca), F32),
                 jax.ShapeDtypeStruct((n_layers, batch, DN_CONV_W - 1, 3 * dn), F32)]
    scratch = [pltpu.VMEM((bt, lt + SUBLANES, ca), F32),
               pltpu.VMEM((bt, lt + SUBLANES, 3 * dn), F32)]
    outs = pl.pallas_call(
        functools.partial(_in_even_kernel, bt=bt, lt=lt, has_state=has_state,
                          n_alias=len(aliases)),
        grid=grid, in_specs=in_specs, out_specs=out_specs, out_shape=out_shape,
        scratch_shapes=scratch, input_output_aliases=aliases,
        compiler_params=_compiler_params(("arbitrary", "arbitrary")),
        name="in_even")(*args)
    return outs[:6], tuple(outs[6:])


def _even_mix_kernel(*refs, bt, lt, c, has_state, single_step, n_alias):
    n_in = 6 + (1 if has_state else 0)
    (q_ref, k_ref, v_ref, sz_ref, gb_ref, dng_ref) = refs[:6]
    if has_state:
        s_prev_ref = refs[6]
    (y_ref, s_ref, o_s, wq_s, u_s, kd_s, aqk_s, egl_s) = refs[n_in + n_alias:]
    hd = DN_HEAD_DIM
    l = pl.program_id(1)
    rows = bt * lt

    @pl.when(l == 0)
    def _():
        if not has_state:
            s_ref[...] = jnp.zeros(s_ref.shape, F32)
        elif not single_step:
            s_ref[...] = s_prev_ref[...]

    sc = DN_ROWS
    nb = sc // c
    nsc = rows // sc
    groups_per_seq = max(lt // sc, 1)
    shift = int(math.log2(c))
    ri = lax.broadcasted_iota(jnp.int32, (sc, sc), 0)
    ci = lax.broadcasted_iota(jnp.int32, (sc, sc), 1)
    rblk = lax.shift_right_logical(ri, shift)
    same = rblk == lax.shift_right_logical(ci, shift)
    upper = (ri <= ci) & same
    causal = (ri >= ci) & same
    strict = (ri > ci) & same
    eye = ri == ci
    last = ci == (lax.shift_left(rblk, shift) + (c - 1))
    eye_f = eye.astype(F32)
    n_sq = shift - 1

    def group(ref, i, lo, hi):
        if nb == 1:
            b, w = divmod(i, groups_per_seq)
            return ref[b, w * sc:(w + 1) * sc, lo:hi]
        return ref[i * nb:(i + 1) * nb, :, lo:hi].reshape(sc, hi - lo)

    def intra_all():
        units = [(i, h) for i in range(nsc) for h in range(DN_HEADS)]
        xs, ts, rhs, qes = {}, {}, {}, {}
        for (i, h) in units:
            rs = slice(i * sc, (i + 1) * sc)
            sl = slice(h * hd, (h + 1) * hd)
            q = group(q_ref, i, h * hd, (h + 1) * hd)
            k = group(k_ref, i, h * hd, (h + 1) * hd)
            v = group(v_ref, i, h * hd, (h + 1) * hd)
            gb = group(gb_ref, i, 0, LANES)
            beta = gb[:, DN_HEADS + h:DN_HEADS + h + 1]
            g_col = gb[:, h:h + 1]
            g_row = jnp.sum(jnp.where(upper, g_col, 0.0), axis=0, keepdims=True)
            g_rows = jnp.broadcast_to(g_row, (sc, sc))
            g_cum = jnp.sum(jnp.where(eye, g_rows, 0.0), axis=1, keepdims=True)
            g_last = jnp.sum(jnp.where(last, g_rows, 0.0), axis=1, keepdims=True)
            decay = jnp.exp(jnp.where(causal, g_cum - g_rows, -jnp.inf))
            kb = k * beta
            kq = _mm_nt(jnp.concatenate([kb, q], axis=0), k)
            e_g = jnp.exp(g_cum)
            xs[i, h] = -(kq[0:sc] * jnp.where(strict, decay, 0.0))
            ts[i, h] = eye_f + xs[i, h]
            rhs[i, h] = jnp.concatenate([kb * e_g, v * beta], axis=1)
            qes[i, h] = q * e_g
            kd_s[rs, sl] = k * jnp.exp(g_last - g_cum)
            aqk_s[i, h] = kq[sc:2 * sc] * decay
            egl_s[i, h] = jnp.broadcast_to(jnp.exp(g_last), (sc, hd))
        for _ in range(n_sq):
            for u in units:
                xs[u] = _mm(xs[u], xs[u])
            for u in units:
                ts[u] = ts[u] + _mm(ts[u], xs[u])
        for (i, h) in units:
            rs = slice(i * sc, (i + 1) * sc)
            sl = slice(h * hd, (h + 1) * hd)
            wu = _mm(ts[i, h], rhs[i, h])
            w = wu[:, 0:hd]
            qe = qes[i, h]
            for j in range(nb):
                wq_s[i * nb + j, h, 0:c] = w[j * c:(j + 1) * c]
                wq_s[i * nb + j, h, c:2 * c] = qe[j * c:(j + 1) * c]
            u_s[rs, sl] = wu[:, hd:2 * hd]

    s_in = s_prev_ref if (has_state and single_step) else s_ref

    def inter(groups):
        units = [(i, h, j) for i in groups for h in range(DN_HEADS) for j in range(nb)]
        bidx = lambda i, j: (i * nb + j) if nb > 1 else i // groups_per_seq
        rows_of = lambda i, j: slice(i * sc + j * c, i * sc + (j + 1) * c)
        cols_of = lambda h: slice(h * hd, (h + 1) * hd)
        ss = {(i, h, j): s_in[bidx(i, j), h] for (i, h, j) in units}
        tqs = {(i, h, j): _mm(wq_s[i * nb + j, h], ss[i, h, j]) for (i, h, j) in units}
        us = {(i, h, j): u_s[rows_of(i, j), cols_of(h)] - tqs[i, h, j][0:c] for (i, h, j) in units}
        upd = {(i, h, j): _mm_tn(kd_s[rows_of(i, j), cols_of(h)], us[i, h, j])
               for (i, h, j) in units}
        for (i, h, j) in units:
            s_ref[bidx(i, j), h] = (ss[i, h, j] * egl_s[i, h, j * c:j * c + 1, :] + upd[i, h, j])
        for i in groups:
            for h in range(DN_HEADS):
                if nb > 1:
                    u = jnp.concatenate([us[i, h, j] for j in range(nb)], axis=0)
                    qs = jnp.concatenate([tqs[i, h, j][c:2 * c] for j in range(nb)], axis=0)
                else:
                    u = us[i, h, 0]
                    qs = tqs[i, h, 0][c:2 * c]
                o_s[i * sc:(i + 1) * sc, cols_of(h)] = qs + _mm(aqk_s[i, h], u)

    intra_all()
    if nb > 1:
        for i in range(nsc):
            inter([i])
    else:
        for w in range(groups_per_seq):
            inter([b * groups_per_seq + w for b in range(bt)])

    for h in range(DN_HEADS):
        sl = slice(h * hd, (h + 1) * hd)
        o = o_s[:, sl]
        o = (o * lax.rsqrt(jnp.mean(o * o, axis=-1, keepdims=True) + EPS) * dng_ref[...]
             * sz_ref[:, :, sl].reshape(rows, hd))
        y_ref[:, :, sl] = o.reshape(bt, lt, hd).astype(y_ref.dtype)


def _even_mix(q, k, v, sz, gb, dn_norm, s_prev, prev_out, layer, n_layers, bt, lt):
    batch, length, dn = q.shape
    c = CHUNK if length % CHUNK == 0 else length
    hd = DN_HEAD_DIM
    nl = length // lt
    rows = bt * lt
    assert rows % DN_ROWS == 0 and DN_ROWS % c == 0 and (c == DN_ROWS or lt == c)
    nsc = rows // DN_ROWS
    nb = DN_ROWS // c
    grid = (batch // bt, nl)
    has_state = s_prev is not None
    tok_map = lambda b, l: (b, l, 0)
    tok = lambda width: pl.BlockSpec((bt, lt, width), tok_map)
    in_specs = [tok(dn), tok(dn), tok(dn), tok(dn), tok(LANES), _resident((1, hd))]
    args = [q, k, v, sz, gb, dn_norm.reshape(1, hd)]
    state_block = pl.BlockSpec((None, bt, DN_HEADS, hd, hd), lambda b, l: (layer, b, 0, 0, 0))
    if has_state:
        in_specs.append(state_block)
        args.append(s_prev)
    aliases = {}
    if prev_out is not None:
        aliases[len(args)] = 1
        in_specs.append(pl.BlockSpec(memory_space=pl.ANY))
        args.append(prev_out)
    out_specs = [tok(dn), state_block]
    out_shape = [jax.ShapeDtypeStruct((batch, length, dn), _token_dtype(lt)),
                 jax.ShapeDtypeStruct((n_layers, batch, DN_HEADS, hd, hd), F32)]
    scratch = [pltpu.VMEM((rows, dn), F32),
               pltpu.VMEM((nsc * nb, DN_HEADS, 2 * c, hd), F32),
               pltpu.VMEM((rows, dn), F32), pltpu.VMEM((rows, dn), F32),
               pltpu.VMEM((nsc, DN_HEADS, DN_ROWS, DN_ROWS), F32),
               pltpu.VMEM((nsc, DN_HEADS, DN_ROWS, hd), F32)]
    return pl.pallas_call(
        functools.partial(_even_mix_kernel, bt=bt, lt=lt, c=c, has_state=has_state,
                          single_step=nl == 1, n_alias=len(aliases)),
        grid=grid, in_specs=in_specs, out_specs=out_specs, out_shape=out_shape,
        scratch_shapes=scratch, input_output_aliases=aliases,
        compiler_params=_compiler_params(("arbitrary", "arbitrary")),
        name="even_mix")(*args)


def _in_odd_kernel(x_ref, g_ref, w_ref, cos_ref, sin_ref, q_ref, k_ref, v_ref, sg_ref):
    dq = RET_HEADS * RET_QK_DIM
    dv = RET_HEADS * RET_V_DIM
    h = _rmsnorm(x_ref[...], g_ref[...]).astype(BF16)

    def proj(lo, hi):
        return jnp.dot(h, w_ref[:, lo:hi], preferred_element_type=F32)

    cos2 = cos_ref[...]
    sin2 = sin_ref[...]

    def rotary(t, out_ref, scale):
        for hh in range(RET_HEADS):
            sl = slice(hh * RET_QK_DIM, (hh + 1) * RET_QK_DIM)
            th = t[:, sl]
            r = th * cos2 + pltpu.roll(th, RET_QK_DIM // 2, 1) * sin2
            out_ref[:, sl] = r if scale is None else r * scale

    q = proj(0, dq)
    k = proj(dq, 2 * dq)
    rotary(q, q_ref, None)
    v = proj(2 * dq, 2 * dq + dv)
    rotary(k, k_ref, RET_QK_DIM ** -0.5)
    gate = proj(2 * dq + dv, 2 * dq + 2 * dv)
    v_ref[...] = v
    sg_ref[...] = _silu(gate)


def _in_odd(x, g, w, pos, seq_len, tm):
    m, d = x.shape
    dq = RET_HEADS * RET_QK_DIM
    dv = RET_HEADS * RET_V_DIM
    half = RET_QK_DIM // 2
    inv = ROPE_BASE ** (-jnp.arange(half, dtype=F32) / half)
    ang = pos.astype(F32)[:, None] * inv[None, :]
    cos = jnp.cos(ang)
    sin = jnp.sin(ang)
    cos2 = jnp.concatenate([cos, cos], axis=-1)
    sin2 = jnp.concatenate([-sin, sin], axis=-1)
    if seq_len < tm:
        cos2 = jnp.tile(cos2, (tm // seq_len, 1))
        sin2 = jnp.tile(sin2, (tm // seq_len, 1))
    n_tab = cos2.shape[0] // tm
    row = lambda width: pl.BlockSpec((tm, width), lambda i: (i, 0))
    tab = pl.BlockSpec((tm, RET_QK_DIM), lambda i: (i % n_tab, 0))
    return pl.pallas_call(
        _in_odd_kernel, grid=(m // tm,),
        in_specs=[row(d), _resident((1, d)), _resident(w.shape), tab, tab],
        out_specs=[row(dq), row(dq), row(dv), row(dv)],
        out_shape=[jax.ShapeDtypeStruct((m, dq), F32), jax.ShapeDtypeStruct((m, dq), F32),
                   jax.ShapeDtypeStruct((m, dv), F32), jax.ShapeDtypeStruct((m, dv), F32)],
        compiler_params=_compiler_params(("arbitrary",)),
        name="in_odd")(x, g.reshape(1, d), w, cos2, sin2)


def _odd_mix_kernel(*refs, bt, lt, has_state, single_step, n_alias):
    n_in = 6 + (1 if has_state else 0)
    (q_ref, k_ref, v_ref, sg_ref, gng_ref, gnb_ref) = refs[:6]
    if has_state:
        r_prev_ref = refs[6]
    (y_ref, r_ref, o_s, dec_s, inn_s, kdec_s) = refs[n_in + n_alias:]
    c = lt
    l = pl.program_id(1)

    lgs = [math.log(1.0 - 2.0 ** (-5.0 - h)) for h in range(RET_HEADS)]

    @pl.when((pl.program_id(0) == 0) & (l == 0))
    def _():
        ri = lax.broadcasted_iota(jnp.int32, (c, c), 0)
        ci = lax.broadcasted_iota(jnp.int32, (c, c), 1)
        diff = (ri - ci).astype(F32)
        idx = lax.broadcasted_iota(jnp.int32, (c, RET_QK_DIM), 0).astype(F32)
        for h in range(RET_HEADS):
            dec_s[h] = jnp.where(diff >= 0, jnp.exp(jnp.maximum(diff, 0.0) * lgs[h]), 0.0)
            inn_s[h] = jnp.exp((idx + 1.0) * lgs[h])
            kdec_s[h] = jnp.exp((c - 1.0 - idx) * lgs[h])

    @pl.when(l == 0)
    def _():
        if not has_state:
            r_ref[...] = jnp.zeros(r_ref.shape, F32)
        elif not single_step:
            r_ref[...] = r_prev_ref[...]

    hg = 2 if c >= LANES else RET_HEADS
    r_in = r_prev_ref if (has_state and single_step) else r_ref
    for b in range(bt):
        rs = slice(b * c, (b + 1) * c)
        for h0 in range(0, RET_HEADS, hg):
            heads = range(h0, h0 + hg)
            qs = {h: q_ref[rs, h * RET_QK_DIM:(h + 1) * RET_QK_DIM] for h in heads}
            ks = {h: k_ref[rs, h * RET_QK_DIM:(h + 1) * RET_QK_DIM] for h in heads}
            vs = {h: v_ref[rs, h * RET_V_DIM:(h + 1) * RET_V_DIM] for h in heads}
            rr = {h: r_in[b, h] for h in heads}
            att = {h: _mm_nt(qs[h], ks[h]) * dec_s[h] for h in heads}
            qr = {h: _mm(qs[h] * inn_s[h], rr[h]) for h in heads}
            kv = {h: _mm_tn(ks[h] * kdec_s[h], vs[h]) for h in heads}
            for h in heads:
                r_ref[b, h] = rr[h] * math.exp(c * lgs[h]) + kv[h]
            for h in heads:
                o_s[rs, h * RET_V_DIM:(h + 1) * RET_V_DIM] = _mm(att[h], vs[h]) + qr[h]

    for h in range(RET_HEADS):
        sl = slice(h * RET_V_DIM, (h + 1) * RET_V_DIM)
        o = o_s[:, sl]
        mu = jnp.mean(o, axis=-1, keepdims=True)
        var = jnp.mean(jnp.square(o - mu), axis=-1, keepdims=True)
        o = (o - mu) * lax.rsqrt(var + GN_EPS) * gng_ref[:, sl] + gnb_ref[:, sl]
        y_ref[:, sl] = (sg_ref[:, sl] * o).astype(y_ref.dtype)


def _odd_mix(q, k, v, sg, gn_g, gn_b, r_prev, prev_out, layer, n_layers, batch, length, bt, lt):
    dq = RET_HEADS * RET_QK_DIM
    dv = RET_HEADS * RET_V_DIM
    nl = length // lt
    rows = bt * lt
    assert bt == 1 or nl == 1
    grid = (batch // bt, nl)
    has_state = r_prev is not None
    row = lambda width: pl.BlockSpec((rows, width), lambda b, l: (b * nl + l, 0))
    in_specs = [row(dq), row(dq), row(dv), row(dv), _resident((1, dv)), _resident((1, dv))]
    args = [q, k, v, sg, gn_g.reshape(1, dv), gn_b.reshape(1, dv)]
    state_block = pl.BlockSpec((None, bt, RET_HEADS, RET_QK_DIM, RET_V_DIM),
                               lambda b, l: (layer, b, 0, 0, 0))
    if has_state:
        in_specs.append(state_block)
        args.append(r_prev)
    aliases = {}
    if prev_out is not None:
        aliases[len(args)] = 1
        in_specs.append(pl.BlockSpec(memory_space=pl.ANY))
        args.append(prev_out)
    out_specs = [row(dv), state_block]
    out_shape = [jax.ShapeDtypeStruct((batch * length, dv), BF16),
                 jax.ShapeDtypeStruct((n_layers, batch, RET_HEADS, RET_QK_DIM, RET_V_DIM), F32)]
    scratch = [pltpu.VMEM((rows, dv), F32),
               pltpu.VMEM((RET_HEADS, lt, lt), F32),
               pltpu.VMEM((RET_HEADS, lt, RET_QK_DIM), F32),
               pltpu.VMEM((RET_HEADS, lt, RET_QK_DIM), F32)]
    return pl.pallas_call(
        functools.partial(_odd_mix_kernel, bt=bt, lt=lt, has_state=has_state,
                          single_step=nl == 1, n_alias=len(aliases)),
        grid=grid, in_specs=in_specs, out_specs=out_specs, out_shape=out_shape,
        scratch_shapes=scratch, input_output_aliases=aliases,
        compiler_params=_compiler_params(("arbitrary", "arbitrary")),
        name="odd_mix")(*args)


def _out_ffn_kernel(*refs, n_y, final):
    x_ref = refs[0]
    y_refs = refs[1:1 + n_y]
    wo_refs = refs[1 + n_y:1 + 2 * n_y]
    g_ref, wg_ref, wu_ref, wd_ref, gf_ref, o_ref = refs[1 + 2 * n_y:]
    x1 = x_ref[...]
    for y_ref, wo_ref in zip(y_refs, wo_refs):
        x1 = x1 + jnp.dot(y_ref[...].astype(BF16), wo_ref[...], preferred_element_type=F32)
    h = _rmsnorm(x1, g_ref[...]).astype(BF16)
    gate = jnp.dot(h, wg_ref[...], preferred_element_type=F32)
    up = jnp.dot(h, wu_ref[...], preferred_element_type=F32)
    a = (_silu(gate) * up).astype(BF16)
    x2 = x1 + jnp.dot(a, wd_ref[...], preferred_element_type=F32)
    if final:
        x2 = _rmsnorm(x2, gf_ref[...])
    o_ref[...] = x2


def _out_ffn(x, ys, wos, g, wg, wu, wd, gf, final, tm):
    m, d = x.shape
    row = lambda width: pl.BlockSpec((tm, width), lambda i: (i, 0))
    in_specs = ([row(d)] + [row(y.shape[1]) for y in ys] + [_resident(wo.shape) for wo in wos]
                + [_resident((1, d)), _resident(wg.shape), _resident(wu.shape),
                   _resident(wd.shape), _resident((1, d))])
    return pl.pallas_call(
        functools.partial(_out_ffn_kernel, n_y=len(ys), final=final), grid=(m // tm,),
        in_specs=in_specs, out_specs=row(d), out_shape=jax.ShapeDtypeStruct((m, d), F32),
        compiler_params=_compiler_params(("arbitrary",)),
        name="out_ffn")(x, *ys, *wos, g.reshape(1, d), wg, wu, wd, gf.reshape(1, d))


def _tiles(batch, length):
    tm = min(batch * length, ROW_TILE)
    if length % CHUNK == 0:
        lt = min(length, 256)
        return dict(tm=tm, in_even=(1, min(length, tm)), even=(2 if batch % 2 == 0 else 1, lt),
                    odd=(1, lt))
    return dict(tm=tm, in_even=(max(tm // length, 1), length), even=(min(batch, 16), length),
                odd=(min(batch, 4), length))


def _run_trunk(x, states, pos, w):
    batch, length, d = x.shape
    t = _tiles(batch, length)
    depth = w["norm_mix"].shape[0]
    n_even = (depth + 1) // 2
    n_odd = depth // 2
    x = x.reshape(batch * length, d)
    conv_out = None
    delta_out = None
    ret_out = None
    for i in range(depth):
        if i % 2 == 0:
            e = i // 2
            w_in = w["even_w_in"][e]
            n_main = w_in.shape[1] - 2 * DN_HEADS
            w_main = w_in[:, :n_main].astype(BF16)
            w_ab = jnp.pad(w_in[:, n_main:], ((0, 0), (0, LANES - 2 * DN_HEADS))).astype(BF16)
            (ya, q, k, v, sz, gb), conv_out = _in_even(
                x.reshape(batch, length, d), w["norm_mix"][i], w_main, w_ab,
                w["even_conv_a"][e], w["even_conv_qkv"][e], w["even_a_log"][e],
                w["even_dt_bias"][e], None if states is None else (states[0], states[1]),
                conv_out, e, n_even, *t["in_even"])
            yb, delta_out = _even_mix(q, k, v, sz, gb, w["even_dn_norm"][e],
                                      None if states is None else states[2], delta_out,
                                      e, n_even, *t["even"])
            n_a = ya.shape[-1]
            wo = w["even_w_out"][e].astype(BF16)
            ys = [ya.reshape(batch * length, n_a), yb.reshape(batch * length, -1)]
            wos = [wo[:n_a], wo[n_a:]]
        else:
            o = i // 2
            q, k, v, sg = _in_odd(x, w["norm_mix"][i], w["odd_w_in"][o].astype(BF16), pos,
                                  length, t["tm"])
            y, ret_out = _odd_mix(q, k, v, sg, w["odd_gn_g"][o], w["odd_gn_b"][o],
                                  None if states is None else states[3], ret_out, o, n_odd,
                                  batch, length, *t["odd"])
            ys = [y]
            wos = [w["odd_w_out"][o].astype(BF16)]
        x = _out_ffn(x, ys, wos, w["norm_ffn"][i], w["ffn_w_gate"][i].astype(BF16),
                     w["ffn_w_up"][i].astype(BF16), w["ffn_w_down"][i].astype(BF16),
                     w["final_norm"], i == depth - 1, t["tm"])
    return (x.reshape(batch, length, d),) + conv_out + (delta_out, ret_out)


def kernel(x_prompt, x_sample, state_conv_a, state_conv_qkv, state_delta, state_ret, norm_mix,
           norm_ffn, final_norm, even_w_in, even_conv_a, even_conv_qkv, even_a_log, even_dt_bias,
           even_dn_norm, even_w_out, odd_w_in, odd_gn_g, odd_gn_b, odd_w_out, ffn_w_gate, ffn_w_up,
           ffn_w_down):
    w = dict(norm_mix=norm_mix, norm_ffn=norm_ffn, final_norm=final_norm, even_w_in=even_w_in,
             even_conv_a=even_conv_a, even_conv_qkv=even_conv_qkv, even_a_log=even_a_log,
             even_dt_bias=even_dt_bias, even_dn_norm=even_dn_norm, even_w_out=even_w_out,
             odd_w_in=odd_w_in, odd_gn_g=odd_gn_g, odd_gn_b=odd_gn_b, odd_w_out=odd_w_out,
             ffn_w_gate=ffn_w_gate, ffn_w_up=ffn_w_up, ffn_w_down=ffn_w_down)
    lp = x_prompt.shape[1]
    ls = x_sample.shape[1]
    pos_p = jnp.arange(lp, dtype=jnp.int32)
    y_prompt, ca_p, cq_p, d_p, r_p = _run_trunk(x_prompt, None, pos_p, w)
    pos_s = PAST_LEN + jnp.arange(ls, dtype=jnp.int32)
    y_sample, ca_s, cq_s, d_s, r_s = _run_trunk(
        x_sample, (state_conv_a, state_conv_qkv, state_delta, state_ret), pos_s, w)
    return (y_prompt, y_sample, ca_p, cq_p, d_p, r_p, ca_s, cq_s, d_s, r_s)
```

```python
import functools
import math

import jax
import jax.numpy as jnp
from jax import lax
from jax.experimental import pallas as pl
from jax.experimental.pallas import tpu as pltpu

F32 = jnp.float32
BF16 = jnp.bfloat16

EPS = 1e-6
GN_EPS = 1e-5
ROPE_BASE = 10000.0
PAST_LEN = 16384
CHUNK = 64

CONV_A_W = 3
DN_CONV_W = 4
DN_HEADS = 4
DN_HEAD_DIM = 128
RET_HEADS = 8
RET_QK_DIM = 128
RET_V_DIM = 256

LANES = 128
SUBLANES = 8
VMEM_LIMIT_BYTES = 56 * 1024 * 1024
DN_ROWS = 64
ROW_TILE = 512


def _compiler_params(semantics):
    return pltpu.CompilerParams(dimension_semantics=semantics,
                                vmem_limit_bytes=VMEM_LIMIT_BYTES)


def _resident(shape):
    nd = len(shape)
    return pl.BlockSpec(shape, lambda *_: (0,) * nd, pipeline_mode=pl.Buffered(1))


def _layer_rows(stack, layer, part=0, parts=1):
    _, k, n = stack.shape
    return pl.BlockSpec((None, k // parts, n), lambda *_: (layer, part, 0),
                        pipeline_mode=pl.Buffered(1))


def _mm(a, b):
    return jnp.dot(a.astype(BF16), b.astype(BF16), preferred_element_type=F32)


def _mm_nt(a, b):
    return lax.dot_general(a.astype(BF16), b.astype(BF16), (((1,), (1,)), ((), ())),
                           preferred_element_type=F32)


def _mm_tn(a, b):
    return lax.dot_general(a.astype(BF16), b.astype(BF16), (((0,), (0,)), ((), ())),
                           preferred_element_type=F32)


def _sigmoid(x):
    return 0.5 * jnp.tanh(0.5 * x) + 0.5


def _silu(x):
    return x * _sigmoid(x)


def _rmsnorm(x, g):
    return x * lax.rsqrt(jnp.mean(x * x, axis=-1, keepdims=True) + EPS) * g


def _token_dtype(lt):
    return BF16 if lt % (2 * SUBLANES) == 0 else F32


def _in_even_kernel(*refs, bt, lt, has_state, n_alias):
    n_in = 8 + (2 if has_state else 0)
    (x_ref, g_ref, wm_ref, wab_ref, caw_ref, cqw_ref, alog_ref, dt_ref) = refs[:8]
    if has_state:
        ca_prev_ref, cq_prev_ref = refs[8:10]
    (ya_ref, q_ref, k_ref, v_ref, sz_ref, gb_ref, ca_new_ref, cq_new_ref,
     ua_s, qkv_s) = refs[n_in + n_alias:]
    hd = DN_HEAD_DIM
    dn = DN_HEADS * hd
    ca = ua_s.shape[-1]
    d = x_ref.shape[-1]
    l = pl.program_id(1)
    nl = pl.num_programs(1)
    rows = bt * lt
    pad = SUBLANES

    @pl.when(l == 0)
    def _():
        ua_s[:, 0:pad, :] = jnp.zeros((bt, pad, ca), F32)
        qkv_s[:, 0:pad, :] = jnp.zeros((bt, pad, 3 * dn), F32)
        if has_state:
            ua_s[:, pad - (CONV_A_W - 1):pad, :] = ca_prev_ref[...]
            qkv_s[:, pad - (DN_CONV_W - 1):pad, :] = cq_prev_ref[...]

    @pl.when(l > 0)
    def _():
        ua_s[:, 0:pad, :] = ua_s[:, lt:lt + pad, :]
        qkv_s[:, 0:pad, :] = qkv_s[:, lt:lt + pad, :]

    h = _rmsnorm(x_ref[...].reshape(rows, d), g_ref[...]).astype(BF16)

    def proj(lo, hi):
        return jnp.dot(h, wm_ref[:, lo:hi], preferred_element_type=F32)

    qkv0 = 3 * ca

    def proj_qkv(part):
        cols = slice(part * dn, (part + 1) * dn)
        qkv_s[:, pad:pad + lt, cols] = proj(qkv0 + part * dn, qkv0 + (part + 1) * dn).reshape(
            bt, lt, dn)

    def delayed(scr, back, cols):
        if back == 0 or bt > 1:
            return scr[:, pad - back:pad - back + lt, cols]
        rolled = pltpu.roll(scr[0, :, cols], back, 0)
        return rolled[pad:pad + lt].reshape(1, lt, rolled.shape[-1])

    def conv_silu(part):
        cols = slice(part * dn, (part + 1) * dn)
        conv = cqw_ref[0:1, cols] * delayed(qkv_s, DN_CONV_W - 1, cols)
        for i in range(1, DN_CONV_W):
            conv = conv + cqw_ref[i:i + 1, cols] * delayed(qkv_s, DN_CONV_W - 1 - i, cols)
        return _silu(conv)

    def l2norm_to(t, out_ref, scale):
        for hh in range(DN_HEADS):
            sl = slice(hh * hd, (hh + 1) * hd)
            th = t[:, :, sl]
            out_ref[:, :, sl] = th * (lax.rsqrt(jnp.sum(th * th, axis=-1, keepdims=True) + EPS)
                                      * scale)

    proj_qkv(0)
    proj_qkv(1)
    l2norm_to(conv_silu(0), q_ref, hd ** -0.5)
    proj_qkv(2)
    l2norm_to(conv_silu(1), k_ref, 1.0)
    cu = proj(ca, 3 * ca)
    v_ref[...] = conv_silu(2)
    gate_b = proj(0, ca)

    ua_s[:, pad:pad + lt, :] = (cu[:, 0:ca] * cu[:, ca:2 * ca]).reshape(bt, lt, ca)
    conv = caw_ref[0:1, :] * delayed(ua_s, CONV_A_W - 1, slice(0, ca))
    for i in range(1, CONV_A_W):
        conv = conv + caw_ref[i:i + 1, :] * delayed(ua_s, CONV_A_W - 1 - i, slice(0, ca))

    z = proj(3 * ca + 3 * dn, 3 * ca + 4 * dn)
    pab = jnp.dot(h, wab_ref[...], preferred_element_type=F32)

    ya_ref[...] = (gate_b.reshape(bt, lt, ca) * conv).astype(ya_ref.dtype)
    sz_ref[...] = _silu(z).reshape(bt, lt, dn)
    sp = jnp.maximum(pab + dt_ref[...], 0.0) + jnp.log(1.0 + jnp.exp(-jnp.abs(pab + dt_ref[...])))
    col = lax.broadcasted_iota(jnp.int32, pab.shape, 1)
    gb = jnp.where(col < DN_HEADS, -jnp.exp(alog_ref[...]) * sp, _sigmoid(pab))
    gb_ref[...] = gb.reshape(bt, lt, LANES)

    @pl.when(l == nl - 1)
    def _():
        ca_new_ref[...] = ua_s[:, pad + lt - (CONV_A_W - 1):pad + lt, :]
        cq_new_ref[...] = qkv_s[:, pad + lt - (DN_CONV_W - 1):pad + lt, :]


def _in_even(x, g, w_main, w_ab, conv_a_w, conv_qkv_w, a_log, dt_bias, states, prev_out,
             layer, n_layers, bt, lt):
    batch, length, d = x.shape
    ca = conv_a_w.shape[1]
    dn = DN_HEADS * DN_HEAD_DIM
    nl = length // lt
    grid = (batch // bt, nl)
    has_state = states is not None
    tok_map = lambda b, l: (b, l, 0)
    tok = lambda width: pl.BlockSpec((bt, lt, width), tok_map)
    alog_row = jnp.zeros((1, LANES), F32).at[0, :DN_HEADS].set(a_log)
    dt_row = jnp.zeros((1, LANES), F32).at[0, :DN_HEADS].set(dt_bias)
    in_specs = [tok(d), _resident((1, d)), _layer_rows(w_main, layer), _layer_rows(w_ab, layer),
                _resident(conv_a_w.shape), _resident(conv_qkv_w.shape),
                _resident((1, LANES)), _resident((1, LANES))]
    args = [x, g.reshape(1, d), w_main, w_ab, conv_a_w, conv_qkv_w, alog_row, dt_row]
    ca_block = pl.BlockSpec((None, bt, CONV_A_W - 1, ca), lambda b, l: (layer, b, 0, 0))
    cq_block = pl.BlockSpec((None, bt, DN_CONV_W - 1, 3 * dn), lambda b, l: (layer, b, 0, 0))
    if has_state:
        in_specs += [ca_block, cq_block]
        args += list(states)
    aliases = {}
    if prev_out is not None:
        for j, a in enumerate(prev_out):
            aliases[len(args)] = 6 + j
            in_specs.append(pl.BlockSpec(memory_space=pl.ANY))
            args.append(a)
    out_specs = [tok(ca), tok(dn), tok(dn), tok(dn), tok(dn), tok(LANES), ca_block, cq_block]
    tokshape = lambda width, dt: jax.ShapeDtypeStruct((batch, length, width), dt)
    out_shape = [tokshape(ca, _token_dtype(lt)), tokshape(dn, F32), tokshape(dn, F32),
                 tokshape(dn, F32), tokshape(dn, F32), tokshape(LANES, F32),
                 jax.ShapeDtypeStruct((n_layers, batch, CONV_A_W - 1, ca), F32),
                 jax.ShapeDtypeStruct((n_layers, batch, DN_CONV_W - 1, 3 * dn), F32)]
    scratch = [pltpu.VMEM((bt, lt + SUBLANES, ca), F32),
               pltpu.VMEM((bt, lt + SUBLANES, 3 * dn), F32)]
    outs = pl.pallas_call(
        functools.partial(_in_even_kernel, bt=bt, lt=lt, has_state=has_state,
                          n_alias=len(aliases)),
        grid=grid, in_specs=in_specs, out_specs=out_specs, out_shape=out_shape,
        scratch_shapes=scratch, input_output_aliases=aliases,
        compiler_params=_compiler_params(("arbitrary", "arbitrary")),
        name="in_even")(*args)
    return outs[:6], tuple(outs[6:])


def _even_mix_kernel(*refs, bt, lt, c, has_state, single_step, n_alias):
    n_in = 6 + (1 if has_state else 0)
    (q_ref, k_ref, v_ref, sz_ref, gb_ref, dng_ref) = refs[:6]
    if has_state:
        s_prev_ref = refs[6]
    (y_ref, s_ref, o_s, wq_s, u_s, kd_s, aqk_s, egl_s) = refs[n_in + n_alias:]
    hd = DN_HEAD_DIM
    l = pl.program_id(1)
    rows = bt * lt

    @pl.when(l == 0)
    def _():
        if not has_state:
            s_ref[...] = jnp.zeros(s_ref.shape, F32)
        elif not single_step:
            s_ref[...] = s_prev_ref[...]

    sc = DN_ROWS
    nb = sc // c
    nsc = rows // sc
    groups_per_seq = max(lt // sc, 1)
    shift = int(math.log2(c))
    ri = lax.broadcasted_iota(jnp.int32, (sc, sc), 0)
    ci = lax.broadcasted_iota(jnp.int32, (sc, sc), 1)
    rblk = lax.shift_right_logical(ri, shift)
    same = rblk == lax.shift_right_logical(ci, shift)
    upper = (ri <= ci) & same
    causal = (ri >= ci) & same
    strict = (ri > ci) & same
    eye = ri == ci
    last = ci == (lax.shift_left(rblk, shift) + (c - 1))
    eye_f = eye.astype(F32)
    n_sq = shift - 1

    def group(ref, i, lo, hi):
        if nb == 1:
            b, w = divmod(i, groups_per_seq)
            return ref[b, w * sc:(w + 1) * sc, lo:hi]
        return ref[i * nb:(i + 1) * nb, :, lo:hi].reshape(sc, hi - lo)

    def intra_all():
        units = [(i, h) for i in range(nsc) for h in range(DN_HEADS)]
        xs, ts, rhs, qes = {}, {}, {}, {}
        for (i, h) in units:
            rs = slice(i * sc, (i + 1) * sc)
            sl = slice(h * hd, (h + 1) * hd)
            q = group(q_ref, i, h * hd, (h + 1) * hd)
            k = group(k_ref, i, h * hd, (h + 1) * hd)
            v = group(v_ref, i, h * hd, (h + 1) * hd)
            gb = group(gb_ref, i, 0, LANES)
            beta = gb[:, DN_HEADS + h:DN_HEADS + h + 1]
            g_col = gb[:, h:h + 1]
            g_row = jnp.sum(jnp.where(upper, g_col, 0.0), axis=0, keepdims=True)
            g_rows = jnp.broadcast_to(g_row, (sc, sc))
            g_cum = jnp.sum(jnp.where(eye, g_rows, 0.0), axis=1, keepdims=True)
            g_last = jnp.sum(jnp.where(last, g_rows, 0.0), axis=1, keepdims=True)
            decay = jnp.exp(jnp.where(causal, g_cum - g_rows, -jnp.inf))
            kb = k * beta
            kq = _mm_nt(jnp.concatenate([kb, q], axis=0), k)
            e_g = jnp.exp(g_cum)
            xs[i, h] = -(kq[0:sc] * jnp.where(strict, decay, 0.0))
            ts[i, h] = eye_f + xs[i, h]
            rhs[i, h] = jnp.concatenate([kb * e_g, v * beta], axis=1)
            qes[i, h] = q * e_g
            kd_s[rs, sl] = k * jnp.exp(g_last - g_cum)
            aqk_s[i, h] = kq[sc:2 * sc] * decay
            egl_s[i, h] = jnp.broadcast_to(jnp.exp(g_last), (sc, hd))
        for _ in range(n_sq):
            for u in units:
                xs[u] = _mm(xs[u], xs[u])
            for u in units:
                ts[u] = ts[u] + _mm(ts[u], xs[u])
        for (i, h) in units:
            rs = slice(i * sc, (i + 1) * sc)
            sl = slice(h * hd, (h + 1) * hd)
            wu = _mm(ts[i, h], rhs[i, h])
            w = wu[:, 0:hd]
            qe = qes[i, h]
            for j in range(nb):
                wq_s[i * nb + j, h, 0:c] = w[j * c:(j + 1) * c]
                wq_s[i * nb + j, h, c:2 * c] = qe[j * c:(j + 1) * c]
            u_s[rs, sl] = wu[:, hd:2 * hd]

    s_in = s_prev_ref if (has_state and single_step) else s_ref

    def inter(groups):
        units = [(i, h, j) for i in groups for h in range(DN_HEADS) for j in range(nb)]
        bidx = lambda i, j: (i * nb + j) if nb > 1 else i // groups_per_seq
        rows_of = lambda i, j: slice(i * sc + j * c, i * sc + (j + 1) * c)
        cols_of = lambda h: slice(h * hd, (h + 1) * hd)
        ss = {(i, h, j): s_in[bidx(i, j), h] for (i, h, j) in units}
        tqs = {(i, h, j): _mm(wq_s[i * nb + j, h], ss[i, h, j]) for (i, h, j) in units}
        us = {(i, h, j): u_s[rows_of(i, j), cols_of(h)] - tqs[i, h, j][0:c] for (i, h, j) in units}
        upd = {(i, h, j): _mm_tn(kd_s[rows_of(i, j), cols_of(h)], us[i, h, j])
               for (i, h, j) in units}
        for (i, h, j) in units:
            s_ref[bidx(i, j), h] = (ss[i, h, j] * egl_s[i, h, j * c:j * c + 1, :] + upd[i, h, j])
        for i in groups:
            for h in range(DN_HEADS):
                if nb > 1:
                    u = jnp.concatenate([us[i, h, j] for j in range(nb)], axis=0)
                    qs = jnp.concatenate([tqs[i, h, j][c:2 * c] for j in range(nb)], axis=0)
                else:
                    u = us[i, h, 0]
                    qs = tqs[i, h, 0][c:2 * c]
                o_s[i * sc:(i + 1) * sc, cols_of(h)] = qs + _mm(aqk_s[i, h], u)

    intra_all()
    if nb > 1:
        for i in range(nsc):
            inter([i])
    else:
        for w in range(groups_per_seq):
            inter([b * groups_per_seq + w for b in range(bt)])

    for h in range(DN_HEADS):
        sl = slice(h * hd, (h + 1) * hd)
        o = o_s[:, sl]
        o = (o * lax.rsqrt(jnp.mean(o * o, axis=-1, keepdims=True) + EPS) * dng_ref[...]
             * sz_ref[:, :, sl].reshape(rows, hd))
        y_ref[:, :, sl] = o.reshape(bt, lt, hd).astype(y_ref.dtype)


def _even_mix(q, k, v, sz, gb, dn_norm, s_prev, prev_out, layer, n_layers, bt, lt):
    batch, length, dn = q.shape
    c = CHUNK if length % CHUNK == 0 else length
    hd = DN_HEAD_DIM
    nl = length // lt
    rows = bt * lt
    assert rows % DN_ROWS == 0 and DN_ROWS % c == 0 and (c == DN_ROWS or lt == c)
    nsc = rows // DN_ROWS
    nb = DN_ROWS // c
    grid = (batch // bt, nl)
    has_state = s_prev is not None
    tok_map = lambda b, l: (b, l, 0)
    tok = lambda width: pl.BlockSpec((bt, lt, width), tok_map)
    in_specs = [tok(dn), tok(dn), tok(dn), tok(dn), tok(LANES), _resident((1, hd))]
    args = [q, k, v, sz, gb, dn_norm.reshape(1, hd)]
    state_block = pl.BlockSpec((None, bt, DN_HEADS, hd, hd), lambda b, l: (layer, b, 0, 0, 0))
    if has_state:
        in_specs.append(state_block)
        args.append(s_prev)
    aliases = {}
    if prev_out is not None:
        aliases[len(args)] = 1
        in_specs.append(pl.BlockSpec(memory_space=pl.ANY))
        args.append(prev_out)
    out_specs = [tok(dn), state_block]
    out_shape = [jax.ShapeDtypeStruct((batch, length, dn), _token_dtype(lt)),
                 jax.ShapeDtypeStruct((n_layers, batch, DN_HEADS, hd, hd), F32)]
    scratch = [pltpu.VMEM((rows, dn), F32),
               pltpu.VMEM((nsc * nb, DN_HEADS, 2 * c, hd), F32),
               pltpu.VMEM((rows, dn), F32), pltpu.VMEM((rows, dn), F32),
               pltpu.VMEM((nsc, DN_HEADS, DN_ROWS, DN_ROWS), F32),
               pltpu.VMEM((nsc, DN_HEADS, DN_ROWS, hd), F32)]
    return pl.pallas_call(
        functools.partial(_even_mix_kernel, bt=bt, lt=lt, c=c, has_state=has_state,
                          single_step=nl == 1, n_alias=len(aliases)),
        grid=grid, in_specs=in_specs, out_specs=out_specs, out_shape=out_shape,
        scratch_shapes=scratch, input_output_aliases=aliases,
        compiler_params=_compiler_params(("arbitrary", "arbitrary")),
        name="even_mix")(*args)


def _in_odd_kernel(x_ref, g_ref, w_ref, cos_ref, sin_ref, q_ref, k_ref, v_ref, sg_ref):
    dq = RET_HEADS * RET_QK_DIM
    dv = RET_HEADS * RET_V_DIM
    h = _rmsnorm(x_ref[...], g_ref[...]).astype(BF16)

    def proj(lo, hi):
        return jnp.dot(h, w_ref[:, lo:hi], preferred_element_type=F32)

    cos2 = cos_ref[...]
    sin2 = sin_ref[...]

    def rotary(t, out_ref, scale):
        for hh in range(RET_HEADS):
            sl = slice(hh * RET_QK_DIM, (hh + 1) * RET_QK_DIM)
            th = t[:, sl]
            r = th * cos2 + pltpu.roll(th, RET_QK_DIM // 2, 1) * sin2
            out_ref[:, sl] = (r if scale is None else r * scale).astype(out_ref.dtype)

    q = proj(0, dq)
    k = proj(dq, 2 * dq)
    rotary(q, q_ref, None)
    v = proj(2 * dq, 2 * dq + dv)
    rotary(k, k_ref, RET_QK_DIM ** -0.5)
    gate = proj(2 * dq + dv, 2 * dq + 2 * dv)
    v_ref[...] = v.astype(v_ref.dtype)
    sg_ref[...] = _silu(gate).astype(sg_ref.dtype)


def _in_odd(x, g, w, layer, pos, seq_len, tm):
    m, d = x.shape
    dq = RET_HEADS * RET_QK_DIM
    dv = RET_HEADS * RET_V_DIM
    half = RET_QK_DIM // 2
    inv = ROPE_BASE ** (-jnp.arange(half, dtype=F32) / half)
    ang = pos.astype(F32)[:, None] * inv[None, :]
    cos = jnp.cos(ang)
    sin = jnp.sin(ang)
    cos2 = jnp.concatenate([cos, cos], axis=-1)
    sin2 = jnp.concatenate([-sin, sin], axis=-1)
    if seq_len < tm:
        cos2 = jnp.tile(cos2, (tm // seq_len, 1))
        sin2 = jnp.tile(sin2, (tm // seq_len, 1))
    n_tab = cos2.shape[0] // tm
    row = lambda width: pl.BlockSpec((tm, width), lambda i: (i, 0))
    tab = pl.BlockSpec((tm, RET_QK_DIM), lambda i: (i % n_tab, 0))
    return pl.pallas_call(
        _in_odd_kernel, grid=(m // tm,),
        in_specs=[row(d), _resident((1, d)), _layer_rows(w, layer), tab, tab],
        out_specs=[row(dq), row(dq), row(dv), row(dv)],
        out_shape=[jax.ShapeDtypeStruct((m, dq), BF16), jax.ShapeDtypeStruct((m, dq), BF16),
                   jax.ShapeDtypeStruct((m, dv), BF16), jax.ShapeDtypeStruct((m, dv), BF16)],
        compiler_params=_compiler_params(("arbitrary",)),
        name="in_odd")(x, g.reshape(1, d), w, cos2, sin2)


def _odd_mix_kernel(*refs, bt, lt, has_state, single_step, n_alias):
    n_in = 6 + (1 if has_state else 0)
    (q_ref, k_ref, v_ref, sg_ref, gng_ref, gnb_ref) = refs[:6]
    if has_state:
        r_prev_ref = refs[6]
    (y_ref, r_ref, o_s, dec_s, inn_s, kdec_s) = refs[n_in + n_alias:]
    c = lt
    l = pl.program_id(1)

    lgs = [math.log(1.0 - 2.0 ** (-5.0 - h)) for h in range(RET_HEADS)]

    @pl.when((pl.program_id(0) == 0) & (l == 0))
    def _():
        ri = lax.broadcasted_iota(jnp.int32, (c, c), 0)
        ci = lax.broadcasted_iota(jnp.int32, (c, c), 1)
        diff = (ri - ci).astype(F32)
        idx = lax.broadcasted_iota(jnp.int32, (c, RET_QK_DIM), 0).astype(F32)
        for h in range(RET_HEADS):
            dec_s[h] = jnp.where(diff >= 0, jnp.exp(jnp.maximum(diff, 0.0) * lgs[h]), 0.0)
            inn_s[h] = jnp.exp((idx + 1.0) * lgs[h])
            kdec_s[h] = jnp.exp((c - 1.0 - idx) * lgs[h])

    @pl.when(l == 0)
    def _():
        if not has_state:
            r_ref[...] = jnp.zeros(r_ref.shape, F32)
        elif not single_step:
            r_ref[...] = r_prev_ref[...]

    hg = 2 if c >= LANES else RET_HEADS
    r_in = r_prev_ref if (has_state and single_step) else r_ref
    def seq_rows(ref, b, lo, hi):
        if bt == 1:
            return ref[:, lo:hi]
        return ref[:, lo:hi].astype(F32)[b * c:(b + 1) * c]

    for b in range(bt):
        rs = slice(b * c, (b + 1) * c)
        for h0 in range(0, RET_HEADS, hg):
            heads = range(h0, h0 + hg)
            qs = {h: seq_rows(q_ref, b, h * RET_QK_DIM, (h + 1) * RET_QK_DIM) for h in heads}
            ks = {h: seq_rows(k_ref, b, h * RET_QK_DIM, (h + 1) * RET_QK_DIM) for h in heads}
            vs = {h: seq_rows(v_ref, b, h * RET_V_DIM, (h + 1) * RET_V_DIM) for h in heads}
            rr = {h: r_in[b, h] for h in heads}
            att = {h: _mm_nt(qs[h], ks[h]) * dec_s[h] for h in heads}
            qr = {h: _mm(qs[h].astype(F32) * inn_s[h], rr[h]) for h in heads}
            kv = {h: _mm_tn(ks[h].astype(F32) * kdec_s[h], vs[h]) for h in heads}
            for h in heads:
                r_ref[b, h] = rr[h] * math.exp(c * lgs[h]) + kv[h]
            for h in heads:
                o_s[rs, h * RET_V_DIM:(h + 1) * RET_V_DIM] = _mm(att[h], vs[h]) + qr[h]

    for h in range(RET_HEADS):
        sl = slice(h * RET_V_DIM, (h + 1) * RET_V_DIM)
        o = o_s[:, sl]
        mu = jnp.mean(o, axis=-1, keepdims=True)
        var = jnp.mean(jnp.square(o - mu), axis=-1, keepdims=True)
        o = (o - mu) * lax.rsqrt(var + GN_EPS) * gng_ref[:, sl] + gnb_ref[:, sl]
        y_ref[:, sl] = (sg_ref[:, sl].astype(F32) * o).astype(y_ref.dtype)


def _odd_mix(q, k, v, sg, gn_g, gn_b, r_prev, prev_out, layer, n_layers, batch, length, bt, lt):
    dq = RET_HEADS * RET_QK_DIM
    dv = RET_HEADS * RET_V_DIM
    nl = length // lt
    rows = bt * lt
    assert bt == 1 or nl == 1
    grid = (batch // bt, nl)
    has_state = r_prev is not None
    row = lambda width: pl.BlockSpec((rows, width), lambda b, l: (b * nl + l, 0))
    in_specs = [row(dq), row(dq), row(dv), row(dv), _resident((1, dv)), _resident((1, dv))]
    args = [q, k, v, sg, gn_g.reshape(1, dv), gn_b.reshape(1, dv)]
    state_block = pl.BlockSpec((None, bt, RET_HEADS, RET_QK_DIM, RET_V_DIM),
                               lambda b, l: (layer, b, 0, 0, 0))
    if has_state:
        in_specs.append(state_block)
        args.append(r_prev)
    aliases = {}
    if prev_out is not None:
        aliases[len(args)] = 1
        in_specs.append(pl.BlockSpec(memory_space=pl.ANY))
        args.append(prev_out)
    out_specs = [row(dv), state_block]
    out_shape = [jax.ShapeDtypeStruct((batch * length, dv), BF16),
                 jax.ShapeDtypeStruct((n_layers, batch, RET_HEADS, RET_QK_DIM, RET_V_DIM), F32)]
    scratch = [pltpu.VMEM((rows, dv), F32),
               pltpu.VMEM((RET_HEADS, lt, lt), F32),
               pltpu.VMEM((RET_HEADS, lt, RET_QK_DIM), F32),
               pltpu.VMEM((RET_HEADS, lt, RET_QK_DIM), F32)]
    return pl.pallas_call(
        functools.partial(_odd_mix_kernel, bt=bt, lt=lt, has_state=has_state,
                          single_step=nl == 1, n_alias=len(aliases)),
        grid=grid, in_specs=in_specs, out_specs=out_specs, out_shape=out_shape,
        scratch_shapes=scratch, input_output_aliases=aliases,
        compiler_params=_compiler_params(("arbitrary", "arbitrary")),
        name="odd_mix")(*args)


def _out_ffn_kernel(*refs, n_y, final):
    x_ref = refs[0]
    y_refs = refs[1:1 + n_y]
    wo_refs = refs[1 + n_y:1 + 2 * n_y]
    g_ref, wg_ref, wu_ref, wd_ref, gf_ref, o_ref = refs[1 + 2 * n_y:]
    x1 = x_ref[...]
    for y_ref, wo_ref in zip(y_refs, wo_refs):
        x1 = x1 + jnp.dot(y_ref[...].astype(BF16), wo_ref[...], preferred_element_type=F32)
    h = _rmsnorm(x1, g_ref[...]).astype(BF16)
    gate = jnp.dot(h, wg_ref[...], preferred_element_type=F32)
    up = jnp.dot(h, wu_ref[...], preferred_element_type=F32)
    a = (_silu(gate) * up).astype(BF16)
    x2 = x1 + jnp.dot(a, wd_ref[...], preferred_element_type=F32)
    if final:
        x2 = _rmsnorm(x2, gf_ref[...])
    o_ref[...] = x2


def _out_ffn(x, ys, wo, wo_layer, g, wg, wu, wd, ffn_layer, gf, final, tm):
    m, d = x.shape
    assert len({y.shape[1] for y in ys}) == 1
    row = lambda width: pl.BlockSpec((tm, width), lambda i: (i, 0))
    in_specs = ([row(d)] + [row(y.shape[1]) for y in ys]
                + [_layer_rows(wo, wo_layer, j, len(ys)) for j in range(len(ys))]
                + [_resident((1, d)), _layer_rows(wg, ffn_layer), _layer_rows(wu, ffn_layer),
                   _layer_rows(wd, ffn_layer), _resident((1, d))])
    return pl.pallas_call(
        functools.partial(_out_ffn_kernel, n_y=len(ys), final=final), grid=(m // tm,),
        in_specs=in_specs, out_specs=row(d), out_shape=jax.ShapeDtypeStruct((m, d), F32),
        compiler_params=_compiler_params(("arbitrary",)),
        name="out_ffn")(x, *ys, *([wo] * len(ys)), g.reshape(1, d), wg, wu, wd, gf.reshape(1, d))


def _tiles(batch, length):
    tm = min(batch * length, ROW_TILE)
    if length % CHUNK == 0:
        lt = min(length, 256)
        return dict(tm=tm, in_even=(1, min(length, tm)), even=(2 if batch % 2 == 0 else 1, lt),
                    odd=(1, lt))
    return dict(tm=tm, in_even=(max(tm // length, 1), length), even=(min(batch, 16), length),
                odd=(min(batch, 4), length))


def _run_trunk(x, states, pos, w):
    batch, length, d = x.shape
    t = _tiles(batch, length)
    depth = w["norm_mix"].shape[0]
    n_even = (depth + 1) // 2
    n_odd = depth // 2
    x = x.reshape(batch * length, d)
    conv_out = None
    delta_out = None
    ret_out = None
    for i in range(depth):
        if i % 2 == 0:
            e = i // 2
            (ya, q, k, v, sz, gb), conv_out = _in_even(
                x.reshape(batch, length, d), w["norm_mix"][i], w["even_w_main"], w["even_w_ab"],
                w["even_conv_a"][e], w["even_conv_qkv"][e], w["even_a_log"][e],
                w["even_dt_bias"][e], None if states is None else (states[0], states[1]),
                conv_out, e, n_even, *t["in_even"])
            yb, delta_out = _even_mix(q, k, v, sz, gb, w["even_dn_norm"][e],
                                      None if states is None else states[2], delta_out,
                                      e, n_even, *t["even"])
            ys = [ya.reshape(batch * length, -1), yb.reshape(batch * length, -1)]
            wo, wo_layer = w["even_w_out"], e
        else:
            o = i // 2
            q, k, v, sg = _in_odd(x, w["norm_mix"][i], w["odd_w_in"], o, pos, length, t["tm"])
            y, ret_out = _odd_mix(q, k, v, sg, w["odd_gn_g"][o], w["odd_gn_b"][o],
                                  None if states is None else states[3], ret_out, o, n_odd,
                                  batch, length, *t["odd"])
            ys = [y]
            wo, wo_layer = w["odd_w_out"], o
        x = _out_ffn(x, ys, wo, wo_layer, w["norm_ffn"][i], w["ffn_w_gate"], w["ffn_w_up"],
                     w["ffn_w_down"], i, w["final_norm"], i == depth - 1, t["tm"])
    return (x.reshape(batch, length, d),) + conv_out + (delta_out, ret_out)


def kernel(x_prompt, x_sample, state_conv_a, state_conv_qkv, state_delta, state_ret, norm_mix,
           norm_ffn, final_norm, even_w_in, even_conv_a, even_conv_qkv, even_a_log, even_dt_bias,
           even_dn_norm, even_w_out, odd_w_in, odd_gn_g, odd_gn_b, odd_w_out, ffn_w_gate, ffn_w_up,
           ffn_w_down):
    n_main = even_w_in.shape[-1] - 2 * DN_HEADS
    w = dict(norm_mix=norm_mix, norm_ffn=norm_ffn, final_norm=final_norm,
             even_w_main=even_w_in[:, :, :n_main].astype(BF16),
             even_w_ab=jnp.pad(even_w_in[:, :, n_main:],
                               ((0, 0), (0, 0), (0, LANES - 2 * DN_HEADS))).astype(BF16),
             even_conv_a=even_conv_a, even_conv_qkv=even_conv_qkv, even_a_log=even_a_log,
             even_dt_bias=even_dt_bias, even_dn_norm=even_dn_norm,
             even_w_out=even_w_out.astype(BF16), odd_w_in=odd_w_in.astype(BF16),
             odd_gn_g=odd_gn_g, odd_gn_b=odd_gn_b, odd_w_out=odd_w_out.astype(BF16),
             ffn_w_gate=ffn_w_gate.astype(BF16), ffn_w_up=ffn_w_up.astype(BF16),
             ffn_w_down=ffn_w_down.astype(BF16))
    lp = x_prompt.shape[1]
    ls = x_sample.shape[1]
    pos_p = jnp.arange(lp, dtype=jnp.int32)
    y_prompt, ca_p, cq_p, d_p, r_p = _run_trunk(x_prompt, None, pos_p, w)
    pos_s = PAST_LEN + jnp.arange(ls, dtype=jnp.int32)
    y_sample, ca_s, cq_s, d_s, r_s = _run_trunk(
        x_sample, (state_conv_a, state_conv_qkv, state_delta, state_ret), pos_s, w)
    return (y_prompt, y_sample, ca_p, cq_p, d_p, r_p, ca_s, cq_s, d_s, r_s)
```

```python
import functools
import math

import jax
import jax.numpy as jnp
from jax import lax
from jax.experimental import pallas as pl
from jax.experimental.pallas import tpu as pltpu

F32 = jnp.float32
BF16 = jnp.bfloat16

EPS = 1e-6
GN_EPS = 1e-5
ROPE_BASE = 10000.0
PAST_LEN = 16384
CHUNK = 64

CONV_A_W = 3
DN_CONV_W = 4
DN_HEADS = 4
DN_HEAD_DIM = 128
RET_HEADS = 8
RET_QK_DIM = 128
RET_V_DIM = 256

LANES = 128
SUBLANES = 8
VMEM_LIMIT_BYTES = 56 * 1024 * 1024
DN_ROWS = 64
ROW_TILE = 512


def _compiler_params(semantics):
    return pltpu.CompilerParams(dimension_semantics=semantics,
                                vmem_limit_bytes=VMEM_LIMIT_BYTES)


def _resident(shape):
    nd = len(shape)
    return pl.BlockSpec(shape, lambda *_: (0,) * nd, pipeline_mode=pl.Buffered(1))


def _layer_rows(stack, layer, part=0, parts=1):
    _, k, n = stack.shape
    return pl.BlockSpec((None, k // parts, n), lambda *_: (layer, part, 0),
                        pipeline_mode=pl.Buffered(1))


def _mm(a, b):
    return jnp.dot(a.astype(BF16), b.astype(BF16), preferred_element_type=F32)


def _mm_nt(a, b):
    return lax.dot_general(a.astype(BF16), b.astype(BF16), (((1,), (1,)), ((), ())),
                           preferred_element_type=F32)


def _mm_tn(a, b):
    return lax.dot_general(a.astype(BF16), b.astype(BF16), (((0,), (0,)), ((), ())),
                           preferred_element_type=F32)


def _sigmoid(x):
    return 0.5 * jnp.tanh(0.5 * x) + 0.5


def _silu(x):
    return x * _sigmoid(x)


def _rmsnorm(x, g):
    return x * lax.rsqrt(jnp.mean(x * x, axis=-1, keepdims=True) + EPS) * g


def _token_dtype(lt):
    return BF16 if lt % (2 * SUBLANES) == 0 else F32


def _in_even_kernel(*refs, bt, lt, has_state, n_alias):
    n_in = 8 + (2 if has_state else 0)
    (x_ref, g_ref, wm_ref, wab_ref, caw_ref, cqw_ref, alog_ref, dt_ref) = refs[:8]
    if has_state:
        ca_prev_ref, cq_prev_ref = refs[8:10]
    (ya_ref, q_ref, k_ref, v_ref, sz_ref, gb_ref, ca_new_ref, cq_new_ref,
     ua_s, qkv_s) = refs[n_in + n_alias:]
    hd = DN_HEAD_DIM
    dn = DN_HEADS * hd
    ca = ua_s.shape[-1]
    d = x_ref.shape[-1]
    l = pl.program_id(1)
    nl = pl.num_programs(1)
    rows = bt * lt
    pad = SUBLANES

    @pl.when(l == 0)
    def _():
        ua_s[:, 0:pad, :] = jnp.zeros((bt, pad, ca), F32)
        qkv_s[:, 0:pad, :] = jnp.zeros((bt, pad, 3 * dn), F32)
        if has_state:
            ua_s[:, pad - (CONV_A_W - 1):pad, :] = ca_prev_ref[...]
            qkv_s[:, pad - (DN_CONV_W - 1):pad, :] = cq_prev_ref[...]

    @pl.when(l > 0)
    def _():
        ua_s[:, 0:pad, :] = ua_s[:, lt:lt + pad, :]
        qkv_s[:, 0:pad, :] = qkv_s[:, lt:lt + pad, :]

    h = _rmsnorm(x_ref[...].reshape(rows, d), g_ref[...]).astype(BF16)

    def proj(lo, hi):
        return jnp.dot(h, wm_ref[:, lo:hi], preferred_element_type=F32)

    qkv0 = 3 * ca

    def proj_qkv(part):
        cols = slice(part * dn, (part + 1) * dn)
        qkv_s[:, pad:pad + lt, cols] = proj(qkv0 + part * dn, qkv0 + (part + 1) * dn).reshape(
            bt, lt, dn)

    def delayed(scr, back, cols):
        if back == 0 or bt > 1:
            return scr[:, pad - back:pad - back + lt, cols]
        rolled = pltpu.roll(scr[0, :, cols], back, 0)
        return rolled[pad:pad + lt].reshape(1, lt, rolled.shape[-1])

    def conv_silu(part):
        cols = slice(part * dn, (part + 1) * dn)
        conv = cqw_ref[0:1, cols] * delayed(qkv_s, DN_CONV_W - 1, cols)
        for i in range(1, DN_CONV_W):
            conv = conv + cqw_ref[i:i + 1, cols] * delayed(qkv_s, DN_CONV_W - 1 - i, cols)
        return _silu(conv)

    def l2norm_to(t, out_ref, scale):
        for hh in range(DN_HEADS):
            sl = slice(hh * hd, (hh + 1) * hd)
            th = t[:, :, sl]
            out_ref[:, :, sl] = th * (lax.rsqrt(jnp.sum(th * th, axis=-1, keepdims=True) + EPS)
                                      * scale)

    proj_qkv(0)
    proj_qkv(1)
    gate_c = proj(ca, 2 * ca)
    l2norm_to(conv_silu(0), q_ref, hd ** -0.5)
    proj_qkv(2)
    h_a = proj(2 * ca, 3 * ca)
    l2norm_to(conv_silu(1), k_ref, 1.0)
    gate_b = proj(0, ca)
    z = proj(3 * ca + 3 * dn, 3 * ca + 4 * dn)
    v_ref[...] = conv_silu(2)

    ua_s[:, pad:pad + lt, :] = (gate_c * h_a).reshape(bt, lt, ca)
    conv = caw_ref[0:1, :] * delayed(ua_s, CONV_A_W - 1, slice(0, ca))
    for i in range(1, CONV_A_W):
        conv = conv + caw_ref[i:i + 1, :] * delayed(ua_s, CONV_A_W - 1 - i, slice(0, ca))
    ya_ref[...] = (gate_b.reshape(bt, lt, ca) * conv).astype(ya_ref.dtype)

    pab = jnp.dot(h, wab_ref[...], preferred_element_type=F32)
    sz_ref[...] = _silu(z).reshape(bt, lt, dn)
    sp = jnp.maximum(pab + dt_ref[...], 0.0) + jnp.log(1.0 + jnp.exp(-jnp.abs(pab + dt_ref[...])))
    col = lax.broadcasted_iota(jnp.int32, pab.shape, 1)
    gb = jnp.where(col < DN_HEADS, -jnp.exp(alog_ref[...]) * sp, _sigmoid(pab))
    gb_ref[...] = gb.reshape(bt, lt, LANES)

    @pl.when(l == nl - 1)
    def _():
        ca_new_ref[...] = ua_s[:, pad + lt - (CONV_A_W - 1):pad + lt, :]
        cq_new_ref[...] = qkv_s[:, pad + lt - (DN_CONV_W - 1):pad + lt, :]


def _in_even(x, g, w_main, w_ab, conv_a_w, conv_qkv_w, a_log, dt_bias, states, prev_out,
             layer, n_layers, bt, lt):
    batch, length, d = x.shape
    ca = conv_a_w.shape[1]
    dn = DN_HEADS * DN_HEAD_DIM
    nl = length // lt
    grid = (batch // bt, nl)
    has_state = states is not None
    tok_map = lambda b, l: (b, l, 0)
    tok = lambda width: pl.BlockSpec((bt, lt, width), tok_map)
    alog_row = jnp.zeros((1, LANES), F32).at[0, :DN_HEADS].set(a_log)
    dt_row = jnp.zeros((1, LANES), F32).at[0, :DN_HEADS].set(dt_bias)
    in_specs = [tok(d), _resident((1, d)), _layer_rows(w_main, layer), _layer_rows(w_ab, layer),
                _resident(conv_a_w.shape), _resident(conv_qkv_w.shape),
                _resident((1, LANES)), _resident((1, LANES))]
    args = [x, g.reshape(1, d), w_main, w_ab, conv_a_w, conv_qkv_w, alog_row, dt_row]
    ca_block = pl.BlockSpec((None, bt, CONV_A_W - 1, ca), lambda b, l: (layer, b, 0, 0))
    cq_block = pl.BlockSpec((None, bt, DN_CONV_W - 1, 3 * dn), lambda b, l: (layer, b, 0, 0))
    if has_state:
        in_specs += [ca_block, cq_block]
        args += list(states)
    aliases = {}
    if prev_out is not None:
        for j, a in enumerate(prev_out):
            aliases[len(args)] = 6 + j
            in_specs.append(pl.BlockSpec(memory_space=pl.ANY))
            args.append(a)
    out_specs = [tok(ca), tok(dn), tok(dn), tok(dn), tok(dn), tok(LANES), ca_block, cq_block]
    tokshape = lambda width, dt: jax.ShapeDtypeStruct((batch, length, width), dt)
    out_shape = [tokshape(ca, _token_dtype(lt)), tokshape(dn, F32), tokshape(dn, F32),
                 tokshape(dn, F32), tokshape(dn, F32), tokshape(LANES, F32),
                 jax.ShapeDtypeStruct((n_layers, batch, CONV_A_W - 1, ca), F32),
                 jax.ShapeDtypeStruct((n_layers, batch, DN_CONV_W - 1, 3 * dn), F32)]
    scratch = [pltpu.VMEM((bt, lt + SUBLANES, ca), F32),
               pltpu.VMEM((bt, lt + SUBLANES, 3 * dn), F32)]
    outs = pl.pallas_call(
        functools.partial(_in_even_kernel, bt=bt, lt=lt, has_state=has_state,
                          n_alias=len(aliases)),
        grid=grid, in_specs=in_specs, out_specs=out_specs, out_shape=out_shape,
        scratch_shapes=scratch, input_output_aliases=aliases,
        compiler_params=_compiler_params(("arbitrary", "arbitrary")),
        name="in_even")(*args)
    return outs[:6], tuple(outs[6:])


def _even_mix_kernel(*refs, bt, lt, c, has_state, single_step, n_alias):
    n_in = 6 + (1 if has_state else 0)
    (q_ref, k_ref, v_ref, sz_ref, gb_ref, dng_ref) = refs[:6]
    if has_state:
        s_prev_ref = refs[6]
    (y_ref, s_ref, o_s, wq_s, u_s, kd_s, aqk_s, egl_s) = refs[n_in + n_alias:]
    hd = DN_HEAD_DIM
    l = pl.program_id(1)
    rows = bt * lt

    @pl.when(l == 0)
    def _():
        if not has_state:
            s_ref[...] = jnp.zeros(s_ref.shape, F32)
        elif not single_step:
            s_ref[...] = s_prev_ref[...]

    sc = DN_ROWS
    nb = sc // c
    nsc = rows // sc
    groups_per_seq = max(lt // sc, 1)
    shift = int(math.log2(c))
    ri = lax.broadcasted_iota(jnp.int32, (sc, sc), 0)
    ci = lax.broadcasted_iota(jnp.int32, (sc, sc), 1)
    rblk = lax.shift_right_logical(ri, shift)
    same = rblk == lax.shift_right_logical(ci, shift)
    upper = (ri <= ci) & same
    causal = (ri >= ci) & same
    strict = (ri > ci) & same
    eye = ri == ci
    last = ci == (lax.shift_left(rblk, shift) + (c - 1))
    eye_f = eye.astype(F32)
    n_sq = shift - 1

    def group(ref, i, lo, hi):
        if nb == 1:
            b, w = divmod(i, groups_per_seq)
            return ref[b, w * sc:(w + 1) * sc, lo:hi]
        return ref[i * nb:(i + 1) * nb, :, lo:hi].reshape(sc, hi - lo)

    def intra_all():
        units = [(i, h) for i in range(nsc) for h in range(DN_HEADS)]
        xs, ts, rhs, qes = {}, {}, {}, {}
        for (i, h) in units:
            rs = slice(i * sc, (i + 1) * sc)
            sl = slice(h * hd, (h + 1) * hd)
            q = group(q_ref, i, h * hd, (h + 1) * hd)
            k = group(k_ref, i, h * hd, (h + 1) * hd)
            v = group(v_ref, i, h * hd, (h + 1) * hd)
            gb = group(gb_ref, i, 0, LANES)
            beta = gb[:, DN_HEADS + h:DN_HEADS + h + 1]
            g_col = gb[:, h:h + 1]
            g_row = jnp.sum(jnp.where(upper, g_col, 0.0), axis=0, keepdims=True)
            g_rows = jnp.broadcast_to(g_row, (sc, sc))
            g_cum = jnp.sum(jnp.where(eye, g_rows, 0.0), axis=1, keepdims=True)
            g_last = jnp.sum(jnp.where(last, g_rows, 0.0), axis=1, keepdims=True)
            decay = jnp.exp(jnp.where(causal, g_cum - g_rows, -jnp.inf))
            kb = k * beta
            kq = _mm_nt(jnp.concatenate([kb, q], axis=0), k)
            e_g = jnp.exp(g_cum)
            xs[i, h] = -(kq[0:sc] * jnp.where(strict, decay, 0.0))
            ts[i, h] = eye_f + xs[i, h]
            rhs[i, h] = jnp.concatenate([kb * e_g, v * beta], axis=1)
            qes[i, h] = q * e_g
            kd_s[rs, sl] = k * jnp.exp(g_last - g_cum)
            aqk_s[i, h] = kq[sc:2 * sc] * decay
            egl_s[i, h] = jnp.broadcast_to(jnp.exp(g_last), (sc, hd))
        for _ in range(n_sq):
            for u in units:
                xs[u] = _mm(xs[u], xs[u])
            for u in units:
                ts[u] = ts[u] + _mm(ts[u], xs[u])
        for (i, h) in units:
            rs = slice(i * sc, (i + 1) * sc)
            sl = slice(h * hd, (h + 1) * hd)
            wu = _mm(ts[i, h], rhs[i, h])
            w = wu[:, 0:hd]
            qe = qes[i, h]
            for j in range(nb):
                wq_s[i * nb + j, h, 0:c] = w[j * c:(j + 1) * c]
                wq_s[i * nb + j, h, c:2 * c] = qe[j * c:(j + 1) * c]
            u_s[rs, sl] = wu[:, hd:2 * hd]

    s_in = s_prev_ref if (has_state and single_step) else s_ref

    def inter(groups):
        units = [(i, h, j) for i in groups for h in range(DN_HEADS) for j in range(nb)]
        bidx = lambda i, j: (i * nb + j) if nb > 1 else i // groups_per_seq
        rows_of = lambda i, j: slice(i * sc + j * c, i * sc + (j + 1) * c)
        cols_of = lambda h: slice(h * hd, (h + 1) * hd)
        ss = {(i, h, j): s_in[bidx(i, j), h] for (i, h, j) in units}
        tqs = {(i, h, j): _mm(wq_s[i * nb + j, h], ss[i, h, j]) for (i, h, j) in units}
        us = {(i, h, j): u_s[rows_of(i, j), cols_of(h)] - tqs[i, h, j][0:c] for (i, h, j) in units}
        upd = {(i, h, j): _mm_tn(kd_s[rows_of(i, j), cols_of(h)], us[i, h, j])
               for (i, h, j) in units}
        for (i, h, j) in units:
            s_ref[bidx(i, j), h] = (ss[i, h, j] * egl_s[i, h, j * c:j * c + 1, :] + upd[i, h, j])
        for i in groups:
            for h in range(DN_HEADS):
                if nb > 1:
                    u = jnp.concatenate([us[i, h, j] for j in range(nb)], axis=0)
                    qs = jnp.concatenate([tqs[i, h, j][c:2 * c] for j in range(nb)], axis=0)
                else:
                    u = us[i, h, 0]
                    qs = tqs[i, h, 0][c:2 * c]
                o_s[i * sc:(i + 1) * sc, cols_of(h)] = qs + _mm(aqk_s[i, h], u)

    intra_all()
    if nb > 1:
        for i in range(nsc):
            inter([i])
    else:
        for w in range(groups_per_seq):
            inter([b * groups_per_seq + w for b in range(bt)])

    for h in range(DN_HEADS):
        sl = slice(h * hd, (h + 1) * hd)
        o = o_s[:, sl]
        o = (o * lax.rsqrt(jnp.mean(o * o, axis=-1, keepdims=True) + EPS) * dng_ref[...]
             * sz_ref[:, :, sl].reshape(rows, hd))
        y_ref[:, :, sl] = o.reshape(bt, lt, hd).astype(y_ref.dtype)


def _even_mix(q, k, v, sz, gb, dn_norm, s_prev, prev_out, layer, n_layers, bt, lt):
    batch, length, dn = q.shape
    c = CHUNK if length % CHUNK == 0 else length
    hd = DN_HEAD_DIM
    nl = length // lt
    rows = bt * lt
    assert rows % DN_ROWS == 0 and DN_ROWS % c == 0 and (c == DN_ROWS or lt == c)
    nsc = rows // DN_ROWS
    nb = DN_ROWS // c
    grid = (batch // bt, nl)
    has_state = s_prev is not None
    tok_map = lambda b, l: (b, l, 0)
    tok = lambda width: pl.BlockSpec((bt, lt, width), tok_map)
    in_specs = [tok(dn), tok(dn), tok(dn), tok(dn), tok(LANES), _resident((1, hd))]
    args = [q, k, v, sz, gb, dn_norm.reshape(1, hd)]
    state_block = pl.BlockSpec((None, bt, DN_HEADS, hd, hd), lambda b, l: (layer, b, 0, 0, 0))
    if has_state:
        in_specs.append(state_block)
        args.append(s_prev)
    aliases = {}
    if prev_out is not None:
        aliases[len(args)] = 1
        in_specs.append(pl.BlockSpec(memory_space=pl.ANY))
        args.append(prev_out)
    out_specs = [tok(dn), state_block]
    out_shape = [jax.ShapeDtypeStruct((batch, length, dn), _token_dtype(lt)),
                 jax.ShapeDtypeStruct((n_layers, batch, DN_HEADS, hd, hd), F32)]
    scratch = [pltpu.VMEM((rows, dn), F32),
               pltpu.VMEM((nsc * nb, DN_HEADS, 2 * c, hd), F32),
               pltpu.VMEM((rows, dn), F32), pltpu.VMEM((rows, dn), F32),
               pltpu.VMEM((nsc, DN_HEADS, DN_ROWS, DN_ROWS), F32),
               pltpu.VMEM((nsc, DN_HEADS, DN_ROWS, hd), F32)]
    return pl.pallas_call(
        functools.partial(_even_mix_kernel, bt=bt, lt=lt, c=c, has_state=has_state,
                          single_step=nl == 1, n_alias=len(aliases)),
        grid=grid, in_specs=in_specs, out_specs=out_specs, out_shape=out_shape,
        scratch_shapes=scratch, input_output_aliases=aliases,
        compiler_params=_compiler_params(("arbitrary", "arbitrary")),
        name="even_mix")(*args)


def _in_odd_kernel(x_ref, g_ref, w_ref, cos_ref, sin_ref, q_ref, k_ref, v_ref, sg_ref):
    dq = RET_HEADS * RET_QK_DIM
    dv = RET_HEADS * RET_V_DIM
    h = _rmsnorm(x_ref[...], g_ref[...]).astype(BF16)

    def proj(lo, hi):
        return jnp.dot(h, w_ref[:, lo:hi], preferred_element_type=F32)

    cos2 = cos_ref[...]
    sin2 = sin_ref[...]

    def rotary(t, out_ref, scale):
        for hh in range(RET_HEADS):
            sl = slice(hh * RET_QK_DIM, (hh + 1) * RET_QK_DIM)
            th = t[:, sl]
            r = th * cos2 + pltpu.roll(th, RET_QK_DIM // 2, 1) * sin2
            out_ref[:, sl] = (r if scale is None else r * scale).astype(out_ref.dtype)

    q = proj(0, dq)
    k = proj(dq, 2 * dq)
    rotary(q, q_ref, None)
    v = proj(2 * dq, 2 * dq + dv)
    rotary(k, k_ref, RET_QK_DIM ** -0.5)
    gate = proj(2 * dq + dv, 2 * dq + 2 * dv)
    v_ref[...] = v.astype(v_ref.dtype)
    sg_ref[...] = _silu(gate).astype(sg_ref.dtype)


def _in_odd(x, g, w, layer, pos, seq_len, tm):
    m, d = x.shape
    dq = RET_HEADS * RET_QK_DIM
    dv = RET_HEADS * RET_V_DIM
    half = RET_QK_DIM // 2
    inv = ROPE_BASE ** (-jnp.arange(half, dtype=F32) / half)
    ang = pos.astype(F32)[:, None] * inv[None, :]
    cos = jnp.cos(ang)
    sin = jnp.sin(ang)
    cos2 = jnp.concatenate([cos, cos], axis=-1)
    sin2 = jnp.concatenate([-sin, sin], axis=-1)
    if seq_len < tm:
        cos2 = jnp.tile(cos2, (tm // seq_len, 1))
        sin2 = jnp.tile(sin2, (tm // seq_len, 1))
    n_tab = cos2.shape[0] // tm
    row = lambda width: pl.BlockSpec((tm, width), lambda i: (i, 0))
    tab = pl.BlockSpec((tm, RET_QK_DIM), lambda i: (i % n_tab, 0))
    return pl.pallas_call(
        _in_odd_kernel, grid=(m // tm,),
        in_specs=[row(d), _resident((1, d)), _layer_rows(w, layer), tab, tab],
        out_specs=[row(dq), row(dq), row(dv), row(dv)],
        out_shape=[jax.ShapeDtypeStruct((m, dq), BF16), jax.ShapeDtypeStruct((m, dq), BF16),
                   jax.ShapeDtypeStruct((m, dv), BF16), jax.ShapeDtypeStruct((m, dv), BF16)],
        compiler_params=_compiler_params(("arbitrary",)),
        name="in_odd")(x, g.reshape(1, d), w, cos2, sin2)


def _odd_mix_kernel(*refs, bt, lt, has_state, single_step, n_alias):
    n_in = 6 + (1 if has_state else 0)
    (q_ref, k_ref, v_ref, sg_ref, gng_ref, gnb_ref) = refs[:6]
    if has_state:
        r_prev_ref = refs[6]
    (y_ref, r_ref, o_s, dec_s, inn_s, kdec_s) = refs[n_in + n_alias:]
    c = lt
    l = pl.program_id(1)

    lgs = [math.log(1.0 - 2.0 ** (-5.0 - h)) for h in range(RET_HEADS)]

    @pl.when((pl.program_id(0) == 0) & (l == 0))
    def _():
        ri = lax.broadcasted_iota(jnp.int32, (c, c), 0)
        ci = lax.broadcasted_iota(jnp.int32, (c, c), 1)
        diff = (ri - ci).astype(F32)
        idx = lax.broadcasted_iota(jnp.int32, (c, RET_QK_DIM), 0).astype(F32)
        for h in range(RET_HEADS):
            dec_s[h] = jnp.where(diff >= 0, jnp.exp(jnp.maximum(diff, 0.0) * lgs[h]), 0.0)
            inn_s[h] = jnp.exp((idx + 1.0) * lgs[h])
            kdec_s[h] = jnp.exp((c - 1.0 - idx) * lgs[h])

    @pl.when(l == 0)
    def _():
        if not has_state:
            r_ref[...] = jnp.zeros(r_ref.shape, F32)
        elif not single_step:
            r_ref[...] = r_prev_ref[...]

    hg = 2 if c >= LANES else RET_HEADS
    r_in = r_prev_ref if (has_state and single_step) else r_ref
    def seq_rows(ref, b, lo, hi):
        if bt == 1:
            return ref[:, lo:hi]
        return ref[:, lo:hi].astype(F32)[b * c:(b + 1) * c]

    for b in range(bt):
        rs = slice(b * c, (b + 1) * c)
        for h0 in range(0, RET_HEADS, hg):
            heads = range(h0, h0 + hg)
            qs = {h: seq_rows(q_ref, b, h * RET_QK_DIM, (h + 1) * RET_QK_DIM) for h in heads}
            ks = {h: seq_rows(k_ref, b, h * RET_QK_DIM, (h + 1) * RET_QK_DIM) for h in heads}
            vs = {h: seq_rows(v_ref, b, h * RET_V_DIM, (h + 1) * RET_V_DIM) for h in heads}
            rr = {h: r_in[b, h] for h in heads}
            att = {h: _mm_nt(qs[h], ks[h]) * dec_s[h] for h in heads}
            qr = {h: _mm(qs[h].astype(F32) * inn_s[h], rr[h]) for h in heads}
            kv = {h: _mm_tn(ks[h].astype(F32) * kdec_s[h], vs[h]) for h in heads}
            for h in heads:
                r_ref[b, h] = rr[h] * math.exp(c * lgs[h]) + kv[h]
            for h in heads:
                o_s[rs, h * RET_V_DIM:(h + 1) * RET_V_DIM] = _mm(att[h], vs[h]) + qr[h]

    for h in range(RET_HEADS):
        sl = slice(h * RET_V_DIM, (h + 1) * RET_V_DIM)
        o = o_s[:, sl]
        mu = jnp.mean(o, axis=-1, keepdims=True)
        var = jnp.mean(jnp.square(o - mu), axis=-1, keepdims=True)
        o = (o - mu) * lax.rsqrt(var + GN_EPS) * gng_ref[:, sl] + gnb_ref[:, sl]
        y_ref[:, sl] = (sg_ref[:, sl].astype(F32) * o).astype(y_ref.dtype)


def _odd_mix(q, k, v, sg, gn_g, gn_b, r_prev, prev_out, layer, n_layers, batch, length, bt, lt):
    dq = RET_HEADS * RET_QK_DIM
    dv = RET_HEADS * RET_V_DIM
    nl = length // lt
    rows = bt * lt
    assert bt == 1 or nl == 1
    grid = (batch // bt, nl)
    has_state = r_prev is not None
    row = lambda width: pl.BlockSpec((rows, width), lambda b, l: (b * nl + l, 0))
    in_specs = [row(dq), row(dq), row(dv), row(dv), _resident((1, dv)), _resident((1, dv))]
    args = [q, k, v, sg, gn_g.reshape(1, dv), gn_b.reshape(1, dv)]
    state_block = pl.BlockSpec((None, bt, RET_HEADS, RET_QK_DIM, RET_V_DIM),
                               lambda b, l: (layer, b, 0, 0, 0))
    if has_state:
        in_specs.append(state_block)
        args.append(r_prev)
    aliases = {}
    if prev_out is not None:
        aliases[len(args)] = 1
        in_specs.append(pl.BlockSpec(memory_space=pl.ANY))
        args.append(prev_out)
    out_specs = [row(dv), state_block]
    out_shape = [jax.ShapeDtypeStruct((batch * length, dv), BF16),
                 jax.ShapeDtypeStruct((n_layers, batch, RET_HEADS, RET_QK_DIM, RET_V_DIM), F32)]
    scratch = [pltpu.VMEM((rows, dv), F32),
               pltpu.VMEM((RET_HEADS, lt, lt), F32),
               pltpu.VMEM((RET_HEADS, lt, RET_QK_DIM), F32),
               pltpu.VMEM((RET_HEADS, lt, RET_QK_DIM), F32)]
    return pl.pallas_call(
        functools.partial(_odd_mix_kernel, bt=bt, lt=lt, has_state=has_state,
                          single_step=nl == 1, n_alias=len(aliases)),
        grid=grid, in_specs=in_specs, out_specs=out_specs, out_shape=out_shape,
        scratch_shapes=scratch, input_output_aliases=aliases,
        compiler_params=_compiler_params(("arbitrary", "arbitrary")),
        name="odd_mix")(*args)


def _out_ffn_kernel(*refs, n_y, final):
    x_ref = refs[0]
    y_refs = refs[1:1 + n_y]
    wo_refs = refs[1 + n_y:1 + 2 * n_y]
    g_ref, wg_ref, wu_ref, wd_ref, gf_ref, o_ref = refs[1 + 2 * n_y:]
    x1 = x_ref[...]
    for y_ref, wo_ref in zip(y_refs, wo_refs):
        x1 = x1 + jnp.dot(y_ref[...].astype(BF16), wo_ref[...], preferred_element_type=F32)
    h = _rmsnorm(x1, g_ref[...]).astype(BF16)
    gate = jnp.dot(h, wg_ref[...], preferred_element_type=F32)
    up = jnp.dot(h, wu_ref[...], preferred_element_type=F32)
    a = (_silu(gate) * up).astype(BF16)
    x2 = x1 + jnp.dot(a, wd_ref[...], preferred_element_type=F32)
    if final:
        x2 = _rmsnorm(x2, gf_ref[...])
    o_ref[...] = x2


def _out_ffn(x, ys, wo, wo_layer, g, wg, wu, wd, ffn_layer, gf, final, tm):
    m, d = x.shape
    assert len({y.shape[1] for y in ys}) == 1
    row = lambda width: pl.BlockSpec((tm, width), lambda i: (i, 0))
    in_specs = ([row(d)] + [row(y.shape[1]) for y in ys]
                + [_layer_rows(wo, wo_layer, j, len(ys)) for j in range(len(ys))]
                + [_resident((1, d)), _layer_rows(wg, ffn_layer), _layer_rows(wu, ffn_layer),
                   _layer_rows(wd, ffn_layer), _resident((1, d))])
    return pl.pallas_call(
        functools.partial(_out_ffn_kernel, n_y=len(ys), final=final), grid=(m // tm,),
        in_specs=in_specs, out_specs=row(d), out_shape=jax.ShapeDtypeStruct((m, d), F32),
        compiler_params=_compiler_params(("arbitrary",)),
        name="out_ffn")(x, *ys, *([wo] * len(ys)), g.reshape(1, d), wg, wu, wd, gf.reshape(1, d))


def _tiles(batch, length):
    tm = min(batch * length, ROW_TILE)
    if length % CHUNK == 0:
        lt = min(length, 256)
        return dict(tm=tm, in_even=(1, min(length, tm)), even=(math.gcd(batch, 2), lt),
                    odd=(1, lt))
    return dict(tm=tm, in_even=(max(tm // length, 1), length), even=(min(batch, 16), length),
                odd=(min(batch, 8), length))


def _run_trunk(x, states, pos, w):
    batch, length, d = x.shape
    t = _tiles(batch, length)
    depth = w["norm_mix"].shape[0]
    n_even = (depth + 1) // 2
    n_odd = depth // 2
    x = x.reshape(batch * length, d)
    conv_out = None
    delta_out = None
    ret_out = None
    for i in range(depth):
        if i % 2 == 0:
            e = i // 2
            (ya, q, k, v, sz, gb), conv_out = _in_even(
                x.reshape(batch, length, d), w["norm_mix"][i], w["even_w_main"], w["even_w_ab"],
                w["even_conv_a"][e], w["even_conv_qkv"][e], w["even_a_log"][e],
                w["even_dt_bias"][e], None if states is None else (states[0], states[1]),
                conv_out, e, n_even, *t["in_even"])
            yb, delta_out = _even_mix(q, k, v, sz, gb, w["even_dn_norm"][e],
                                      None if states is None else states[2], delta_out,
                                      e, n_even, *t["even"])
            ys = [ya.reshape(batch * length, -1), yb.reshape(batch * length, -1)]
            wo, wo_layer = w["even_w_out"], e
        else:
            o = i // 2
            q, k, v, sg = _in_odd(x, w["norm_mix"][i], w["odd_w_in"], o, pos, length, t["tm"])
            y, ret_out = _odd_mix(q, k, v, sg, w["odd_gn_g"][o], w["odd_gn_b"][o],
                                  None if states is None else states[3], ret_out, o, n_odd,
                                  batch, length, *t["odd"])
            ys = [y]
            wo, wo_layer = w["odd_w_out"], o
        x = _out_ffn(x, ys, wo, wo_layer, w["norm_ffn"][i], w["ffn_w_gate"], w["ffn_w_up"],
                     w["ffn_w_down"], i, w["final_norm"], i == depth - 1, t["tm"])
    return (x.reshape(batch, length, d),) + conv_out + (delta_out, ret_out)


def kernel(x_prompt, x_sample, state_conv_a, state_conv_qkv, state_delta, state_ret, norm_mix,
           norm_ffn, final_norm, even_w_in, even_conv_a, even_conv_qkv, even_a_log, even_dt_bias,
           even_dn_norm, even_w_out, odd_w_in, odd_gn_g, odd_gn_b, odd_w_out, ffn_w_gate, ffn_w_up,
           ffn_w_down):
    n_main = even_w_in.shape[-1] - 2 * DN_HEADS
    w = dict(norm_mix=norm_mix, norm_ffn=norm_ffn, final_norm=final_norm,
             even_w_main=even_w_in.astype(BF16),
             even_w_ab=jnp.pad(even_w_in[:, :, n_main:],
                               ((0, 0), (0, 0), (0, LANES - 2 * DN_HEADS))).astype(BF16),
             even_conv_a=even_conv_a, even_conv_qkv=even_conv_qkv, even_a_log=even_a_log,
             even_dt_bias=even_dt_bias, even_dn_norm=even_dn_norm,
             even_w_out=even_w_out.astype(BF16), odd_w_in=odd_w_in.astype(BF16),
             odd_gn_g=odd_gn_g, odd_gn_b=odd_gn_b, odd_w_out=odd_w_out.astype(BF16),
             ffn_w_gate=ffn_w_gate.astype(BF16), ffn_w_up=ffn_w_up.astype(BF16),
             ffn_w_down=ffn_w_down.astype(BF16))
    lp = x_prompt.shape[1]
    ls = x_sample.shape[1]
    pos_p = jnp.arange(lp, dtype=jnp.int32)
    y_prompt, ca_p, cq_p, d_p, r_p = _run_trunk(x_prompt, None, pos_p, w)
    pos_s = PAST_LEN + jnp.arange(ls, dtype=jnp.int32)
    y_sample, ca_s, cq_s, d_s, r_s = _run_trunk(
        x_sample, (state_conv_a, state_conv_qkv, state_delta, state_ret), pos_s, w)
    return (y_prompt, y_sample, ca_p, cq_p, d_p, r_p, ca_s, cq_s, d_s, r_s)
```

```python
import functools
import math

import jax
import jax.numpy as jnp
from jax import lax
from jax.experimental import pallas as pl
from jax.experimental.pallas import tpu as pltpu

F32 = jnp.float32
BF16 = jnp.bfloat16

EPS = 1e-6
GN_EPS = 1e-5
ROPE_BASE = 10000.0
PAST_LEN = 16384
CHUNK = 64

CONV_A_W = 3
DN_CONV_W = 4
DN_HEADS = 4
DN_HEAD_DIM = 128
RET_HEADS = 8
RET_QK_DIM = 128
RET_V_DIM = 256

LANES = 128
SUBLANES = 8
VMEM_LIMIT_BYTES = 56 * 1024 * 1024
DN_ROWS = 64
INTRA_WAVE = 8
ROW_TILE = 512


def _compiler_params(semantics):
    return pltpu.CompilerParams(dimension_semantics=semantics,
                                vmem_limit_bytes=VMEM_LIMIT_BYTES)


def _resident(shape):
    nd = len(shape)
    return pl.BlockSpec(shape, lambda *_: (0,) * nd, pipeline_mode=pl.Buffered(1))


def _layer_rows(stack, layer, part=0, parts=1):
    _, k, n = stack.shape
    return pl.BlockSpec((None, k // parts, n), lambda *_: (layer, part, 0),
                        pipeline_mode=pl.Buffered(1))


def _group_rows(start, steps, tm):
    def index(i):
        return jnp.minimum(jnp.maximum(i - start, 0), steps - 1)
    return index, (lambda width: pl.BlockSpec((tm, width), lambda i: (index(i), 0)))


def _for_each_group(starts, steps, bodies):
    i = pl.program_id(0)
    if len(bodies) == 1:
        bodies[0]()
        return
    for start, n, body in zip(starts, steps, bodies):
        pl.when((i >= start) & (i < start + n))(body)


def _mm(a, b):
    return jnp.dot(a.astype(BF16), b.astype(BF16), preferred_element_type=F32)


def _mm_nt(a, b):
    return lax.dot_general(a.astype(BF16), b.astype(BF16), (((1,), (1,)), ((), ())),
                           preferred_element_type=F32)


def _mm_tn(a, b):
    return lax.dot_general(a.astype(BF16), b.astype(BF16), (((0,), (0,)), ((), ())),
                           preferred_element_type=F32)


def _sigmoid(x):
    return 0.5 * jnp.tanh(0.5 * x) + 0.5


def _silu(x):
    return x * _sigmoid(x)


def _rmsnorm(x, g):
    return x * lax.rsqrt(jnp.mean(x * x, axis=-1, keepdims=True) + EPS) * g


def _token_dtype(lt):
    return BF16 if lt % (2 * SUBLANES) == 0 else F32


def _in_even_kernel(*refs, bt, lt, has_state, n_alias):
    n_in = 8 + (2 if has_state else 0)
    (x_ref, g_ref, wm_ref, wab_ref, caw_ref, cqw_ref, alog_ref, dt_ref) = refs[:8]
    if has_state:
        ca_prev_ref, cq_prev_ref = refs[8:10]
    (ya_ref, q_ref, k_ref, v_ref, sz_ref, gb_ref, ca_new_ref, cq_new_ref,
     ua_s, qkv_s) = refs[n_in + n_alias:]
    hd = DN_HEAD_DIM
    dn = DN_HEADS * hd
    ca = ua_s.shape[-1]
    d = x_ref.shape[-1]
    l = pl.program_id(1)
    nl = pl.num_programs(1)
    rows = bt * lt
    pad = SUBLANES

    @pl.when(l == 0)
    def _():
        ua_s[:, 0:pad, :] = jnp.zeros((bt, pad, ca), F32)
        qkv_s[:, 0:pad, :] = jnp.zeros((bt, pad, 3 * dn), F32)
        if has_state:
            ua_s[:, pad - (CONV_A_W - 1):pad, :] = ca_prev_ref[...]
            qkv_s[:, pad - (DN_CONV_W - 1):pad, :] = cq_prev_ref[...]

    @pl.when(l > 0)
    def _():
        ua_s[:, 0:pad, :] = ua_s[:, lt:lt + pad, :]
        qkv_s[:, 0:pad, :] = qkv_s[:, lt:lt + pad, :]

    h = _rmsnorm(x_ref[...].reshape(rows, d), g_ref[...]).astype(BF16)

    def proj(lo, hi):
        return jnp.dot(h, wm_ref[:, lo:hi], preferred_element_type=F32)

    qkv0 = 3 * ca

    def proj_qkv(part):
        cols = slice(part * dn, (part + 1) * dn)
        qkv_s[:, pad:pad + lt, cols] = proj(qkv0 + part * dn, qkv0 + (part + 1) * dn).reshape(
            bt, lt, dn)

    def delayed(scr, back, cols):
        if back == 0 or bt > 1:
            return scr[:, pad - back:pad - back + lt, cols]
        rolled = pltpu.roll(scr[0, :, cols], back, 0)
        return rolled[pad:pad + lt].reshape(1, lt, rolled.shape[-1])

    def conv_silu(part):
        cols = slice(part * dn, (part + 1) * dn)
        conv = cqw_ref[0:1, cols] * delayed(qkv_s, DN_CONV_W - 1, cols)
        for i in range(1, DN_CONV_W):
            conv = conv + cqw_ref[i:i + 1, cols] * delayed(qkv_s, DN_CONV_W - 1 - i, cols)
        return _silu(conv)

    def l2norm_to(t, out_ref, scale):
        for hh in range(DN_HEADS):
            sl = slice(hh * hd, (hh + 1) * hd)
            th = t[:, :, sl]
            out_ref[:, :, sl] = th * (lax.rsqrt(jnp.sum(th * th, axis=-1, keepdims=True) + EPS)
                                      * scale)

    proj_qkv(0)
    proj_qkv(1)
    gate_c = proj(ca, 2 * ca)
    l2norm_to(conv_silu(0), q_ref, hd ** -0.5)
    proj_qkv(2)
    h_a = proj(2 * ca, 3 * ca)
    l2norm_to(conv_silu(1), k_ref, 1.0)
    gate_b = proj(0, ca)
    z = proj(3 * ca + 3 * dn, 3 * ca + 4 * dn)
    v_ref[...] = conv_silu(2)

    ua_s[:, pad:pad + lt, :] = (gate_c * h_a).reshape(bt, lt, ca)
    conv = caw_ref[0:1, :] * delayed(ua_s, CONV_A_W - 1, slice(0, ca))
    for i in range(1, CONV_A_W):
        conv = conv + caw_ref[i:i + 1, :] * delayed(ua_s, CONV_A_W - 1 - i, slice(0, ca))
    ya_ref[...] = (gate_b.reshape(bt, lt, ca) * conv).astype(ya_ref.dtype)

    pab = jnp.dot(h, wab_ref[...], preferred_element_type=F32)
    sz_ref[...] = _silu(z).reshape(bt, lt, dn)
    sp = jnp.maximum(pab + dt_ref[...], 0.0) + jnp.log(1.0 + jnp.exp(-jnp.abs(pab + dt_ref[...])))
    col = lax.broadcasted_iota(jnp.int32, pab.shape, 1)
    gb = jnp.where(col < DN_HEADS, -jnp.exp(alog_ref[...]) * sp, _sigmoid(pab))
    gb_ref[...] = gb.reshape(bt, lt, LANES)

    @pl.when(l == nl - 1)
    def _():
        ca_new_ref[...] = ua_s[:, pad + lt - (CONV_A_W - 1):pad + lt, :]
        cq_new_ref[...] = qkv_s[:, pad + lt - (DN_CONV_W - 1):pad + lt, :]


def _in_even(x, g, w_main, w_ab, conv_a_w, conv_qkv_w, a_log, dt_bias, states, prev_out,
             layer, n_layers, bt, lt):
    batch, length, d = x.shape
    ca = conv_a_w.shape[1]
    dn = DN_HEADS * DN_HEAD_DIM
    nl = length // lt
    grid = (batch // bt, nl)
    has_state = states is not None
    tok_map = lambda b, l: (b, l, 0)
    tok = lambda width: pl.BlockSpec((bt, lt, width), tok_map)
    alog_row = jnp.zeros((1, LANES), F32).at[0, :DN_HEADS].set(a_log)
    dt_row = jnp.zeros((1, LANES), F32).at[0, :DN_HEADS].set(dt_bias)
    in_specs = [tok(d), _resident((1, d)), _layer_rows(w_main, layer), _layer_rows(w_ab, layer),
                _resident(conv_a_w.shape), _resident(conv_qkv_w.shape),
                _resident((1, LANES)), _resident((1, LANES))]
    args = [x, g.reshape(1, d), w_main, w_ab, conv_a_w, conv_qkv_w, alog_row, dt_row]
    ca_block = pl.BlockSpec((None, bt, CONV_A_W - 1, ca), lambda b, l: (layer, b, 0, 0))
    cq_block = pl.BlockSpec((None, bt, DN_CONV_W - 1, 3 * dn), lambda b, l: (layer, b, 0, 0))
    if has_state:
        in_specs += [ca_block, cq_block]
        args += list(states)
    aliases = {}
    if prev_out is not None:
        for j, a in enumerate(prev_out):
            aliases[len(args)] = 6 + j
            in_specs.append(pl.BlockSpec(memory_space=pl.ANY))
            args.append(a)
    out_specs = [tok(ca), tok(dn), tok(dn), tok(dn), tok(dn), tok(LANES), ca_block, cq_block]
    tokshape = lambda width, dt: jax.ShapeDtypeStruct((batch, length, width), dt)
    out_shape = [tokshape(ca, _token_dtype(lt)), tokshape(dn, F32), tokshape(dn, F32),
                 tokshape(dn, F32), tokshape(dn, F32), tokshape(LANES, F32),
                 jax.ShapeDtypeStruct((n_layers, batch, CONV_A_W - 1, ca), F32),
                 jax.ShapeDtypeStruct((n_layers, batch, DN_CONV_W - 1, 3 * dn), F32)]
    scratch = [pltpu.VMEM((bt, lt + SUBLANES, ca), F32),
               pltpu.VMEM((bt, lt + SUBLANES, 3 * dn), F32)]
    outs = pl.pallas_call(
        functools.partial(_in_even_kernel, bt=bt, lt=lt, has_state=has_state,
                          n_alias=len(aliases)),
        grid=grid, in_specs=in_specs, out_specs=out_specs, out_shape=out_shape,
        scratch_shapes=scratch, input_output_aliases=aliases,
        compiler_params=_compiler_params(("arbitrary", "arbitrary")),
        name="in_even")(*args)
    return outs[:6], tuple(outs[6:])


def _even_mix_kernel(*refs, bt, lt, c, has_state, single_step, n_alias):
    n_in = 6 + (1 if has_state else 0)
    (q_ref, k_ref, v_ref, sz_ref, gb_ref, dng_ref) = refs[:6]
    if has_state:
        s_prev_ref = refs[6]
    (y_ref, s_ref, o_s, wq_s, u_s, kd_s, aqk_s, egl_s) = refs[n_in + n_alias:]
    hd = DN_HEAD_DIM
    l = pl.program_id(1)
    rows = bt * lt

    @pl.when(l == 0)
    def _():
        if not has_state:
            s_ref[...] = jnp.zeros(s_ref.shape, F32)
        elif not single_step:
            s_ref[...] = s_prev_ref[...]

    sc = DN_ROWS
    nb = sc // c
    nsc = rows // sc
    groups_per_seq = max(lt // sc, 1)
    shift = int(math.log2(c))
    ri = lax.broadcasted_iota(jnp.int32, (sc, sc), 0)
    ci = lax.broadcasted_iota(jnp.int32, (sc, sc), 1)
    rblk = lax.shift_right_logical(ri, shift)
    same = rblk == lax.shift_right_logical(ci, shift)
    upper = (ri <= ci) & same
    causal = (ri >= ci) & same
    strict = (ri > ci) & same
    eye = ri == ci
    last = ci == (lax.shift_left(rblk, shift) + (c - 1))
    eye_f = eye.astype(F32)
    n_sq = shift - 1

    def group(ref, i, lo, hi):
        if nb == 1:
            b, w = divmod(i, groups_per_seq)
            return ref[b, w * sc:(w + 1) * sc, lo:hi]
        return ref[i * nb:(i + 1) * nb, :, lo:hi].reshape(sc, hi - lo)

    def intra(groups):
        units = [(i, h) for i in groups for h in range(DN_HEADS)]
        xs, ts, rhs, qes = {}, {}, {}, {}
        for (i, h) in units:
            rs = slice(i * sc, (i + 1) * sc)
            sl = slice(h * hd, (h + 1) * hd)
            q = group(q_ref, i, h * hd, (h + 1) * hd)
            k = group(k_ref, i, h * hd, (h + 1) * hd)
            v = group(v_ref, i, h * hd, (h + 1) * hd)
            gb = group(gb_ref, i, 0, LANES)
            beta = gb[:, DN_HEADS + h:DN_HEADS + h + 1]
            g_col = gb[:, h:h + 1]
            g_row = jnp.sum(jnp.where(upper, g_col, 0.0), axis=0, keepdims=True)
            g_rows = jnp.broadcast_to(g_row, (sc, sc))
            g_cum = jnp.sum(jnp.where(eye, g_rows, 0.0), axis=1, keepdims=True)
            g_last = jnp.sum(jnp.where(last, g_rows, 0.0), axis=1, keepdims=True)
            decay = jnp.exp(jnp.where(causal, g_cum - g_rows, -jnp.inf))
            kb = k * beta
            kq = _mm_nt(jnp.concatenate([kb, q], axis=0), k)
            e_g = jnp.exp(g_cum)
            xs[i, h] = -(kq[0:sc] * jnp.where(strict, decay, 0.0))
            ts[i, h] = eye_f + xs[i, h]
            rhs[i, h] = jnp.concatenate([kb * e_g, v * beta], axis=1)
            qes[i, h] = q * e_g
            kd_s[rs, sl] = k * jnp.exp(g_last - g_cum)
            aqk_s[i, h] = kq[sc:2 * sc] * decay
            egl_s[i, h] = jnp.broadcast_to(jnp.exp(g_last), (sc, hd))
        for _ in range(n_sq):
            for u in units:
                xs[u] = _mm(xs[u], xs[u])
            for u in units:
                ts[u] = ts[u] + _mm(ts[u], xs[u])
        for (i, h) in units:
            rs = slice(i * sc, (i + 1) * sc)
            sl = slice(h * hd, (h + 1) * hd)
            wu = _mm(ts[i, h], rhs[i, h])
            w = wu[:, 0:hd]
            qe = qes[i, h]
            for j in range(nb):
                wq_s[i * nb + j, h, 0:c] = w[j * c:(j + 1) * c]
                wq_s[i * nb + j, h, c:2 * c] = qe[j * c:(j + 1) * c]
            u_s[rs, sl] = wu[:, hd:2 * hd]

    s_in = s_prev_ref if (has_state and single_step) else s_ref

    def inter(groups):
        units = [(i, h, j) for i in groups for h in range(DN_HEADS) for j in range(nb)]
        bidx = lambda i, j: (i * nb + j) if nb > 1 else i // groups_per_seq
        rows_of = lambda i, j: slice(i * sc + j * c, i * sc + (j + 1) * c)
        cols_of = lambda h: slice(h * hd, (h + 1) * hd)
        ss = {(i, h, j): s_in[bidx(i, j), h] for (i, h, j) in units}
        tqs = {(i, h, j): _mm(wq_s[i * nb + j, h], ss[i, h, j]) for (i, h, j) in units}
        us = {(i, h, j): u_s[rows_of(i, j), cols_of(h)] - tqs[i, h, j][0:c] for (i, h, j) in units}
        upd = {(i, h, j): _mm_tn(kd_s[rows_of(i, j), cols_of(h)], us[i, h, j])
               for (i, h, j) in units}
        for (i, h, j) in units:
            s_ref[bidx(i, j), h] = (ss[i, h, j] * egl_s[i, h, j * c:j * c + 1, :] + upd[i, h, j])
        for i in groups:
            for h in range(DN_HEADS):
                if nb > 1:
                    u = jnp.concatenate([us[i, h, j] for j in range(nb)], axis=0)
                    qs = jnp.concatenate([tqs[i, h, j][c:2 * c] for j in range(nb)], axis=0)
                else:
                    u = us[i, h, 0]
                    qs = tqs[i, h, 0][c:2 * c]
                o_s[i * sc:(i + 1) * sc, cols_of(h)] = qs + _mm(aqk_s[i, h], u)

    wave = INTRA_WAVE
    for i0 in range(0, nsc, wave):
        intra(range(i0, min(i0 + wave, nsc)))
    if nb > 1:
        for i in range(nsc):
            inter([i])
    else:
        for w in range(groups_per_seq):
            inter([b * groups_per_seq + w for b in range(bt)])

    for h in range(DN_HEADS):
        sl = slice(h * hd, (h + 1) * hd)
        o = o_s[:, sl]
        o = (o * lax.rsqrt(jnp.mean(o * o, axis=-1, keepdims=True) + EPS) * dng_ref[...]
             * sz_ref[:, :, sl].reshape(rows, hd))
        y_ref[:, :, sl] = o.reshape(bt, lt, hd).astype(y_ref.dtype)


def _even_mix(q, k, v, sz, gb, dn_norm, s_prev, prev_out, layer, n_layers, bt, lt):
    batch, length, dn = q.shape
    c = CHUNK if length % CHUNK == 0 else length
    hd = DN_HEAD_DIM
    nl = length // lt
    rows = bt * lt
    assert rows % DN_ROWS == 0 and DN_ROWS % c == 0 and (c == DN_ROWS or lt == c)
    nsc = rows // DN_ROWS
    nb = DN_ROWS // c
    grid = (batch // bt, nl)
    has_state = s_prev is not None
    tok_map = lambda b, l: (b, l, 0)
    tok = lambda width: pl.BlockSpec((bt, lt, width), tok_map)
    in_specs = [tok(dn), tok(dn), tok(dn), tok(dn), tok(LANES), _resident((1, hd))]
    args = [q, k, v, sz, gb, dn_norm.reshape(1, hd)]
    state_block = pl.BlockSpec((None, bt, DN_HEADS, hd, hd), lambda b, l: (layer, b, 0, 0, 0))
    if has_state:
        in_specs.append(state_block)
        args.append(s_prev)
    aliases = {}
    if prev_out is not None:
        aliases[len(args)] = 1
        in_specs.append(pl.BlockSpec(memory_space=pl.ANY))
        args.append(prev_out)
    out_specs = [tok(dn), state_block]
    out_shape = [jax.ShapeDtypeStruct((batch, length, dn), _token_dtype(lt)),
                 jax.ShapeDtypeStruct((n_layers, batch, DN_HEADS, hd, hd), F32)]
    scratch = [pltpu.VMEM((rows, dn), F32),
               pltpu.VMEM((nsc * nb, DN_HEADS, 2 * c, hd), F32),
               pltpu.VMEM((rows, dn), F32), pltpu.VMEM((rows, dn), F32),
               pltpu.VMEM((nsc, DN_HEADS, DN_ROWS, DN_ROWS), F32),
               pltpu.VMEM((nsc, DN_HEADS, DN_ROWS, hd), F32)]
    return pl.pallas_call(
        functools.partial(_even_mix_kernel, bt=bt, lt=lt, c=c, has_state=has_state,
                          single_step=nl == 1, n_alias=len(aliases)),
        grid=grid, in_specs=in_specs, out_specs=out_specs, out_shape=out_shape,
        scratch_shapes=scratch, input_output_aliases=aliases,
        compiler_params=_compiler_params(("arbitrary", "arbitrary")),
        name="even_mix")(*args)


def _in_odd_kernel(*refs, starts, steps):
    ng = len(steps)
    g_ref, w_ref = refs[3 * ng:3 * ng + 2]
    outs = refs[3 * ng + 2:]
    dq = RET_HEADS * RET_QK_DIM
    dv = RET_HEADS * RET_V_DIM

    def body(x_ref, cos_ref, sin_ref, q_ref, k_ref, v_ref, sg_ref):
        h = _rmsnorm(x_ref[...], g_ref[...]).astype(BF16)

        def proj(lo, hi):
            return jnp.dot(h, w_ref[:, lo:hi], preferred_element_type=F32)

        cos2 = cos_ref[...]
        sin2 = sin_ref[...]

        def rotary(t, out_ref, scale):
            for hh in range(RET_HEADS):
                sl = slice(hh * RET_QK_DIM, (hh + 1) * RET_QK_DIM)
                th = t[:, sl]
                r = th * cos2 + pltpu.roll(th, RET_QK_DIM // 2, 1) * sin2
                out_ref[:, sl] = (r if scale is None else r * scale).astype(out_ref.dtype)

        q = proj(0, dq)
        k = proj(dq, 2 * dq)
        rotary(q, q_ref, None)
        v = proj(2 * dq, 2 * dq + dv)
        rotary(k, k_ref, RET_QK_DIM ** -0.5)
        gate = proj(2 * dq + dv, 2 * dq + 2 * dv)
        v_ref[...] = v.astype(v_ref.dtype)
        sg_ref[...] = _silu(gate).astype(sg_ref.dtype)

    _for_each_group(starts, steps, [
        functools.partial(body, *refs[3 * g:3 * g + 3], *outs[4 * g:4 * g + 4]) for g in range(ng)])


def _in_odd(xs, g, w, layer, poss, seq_lens, tms):
    d = xs[0].shape[1]
    dq = RET_HEADS * RET_QK_DIM
    dv = RET_HEADS * RET_V_DIM
    half = RET_QK_DIM // 2
    inv = ROPE_BASE ** (-jnp.arange(half, dtype=F32) / half)
    steps = [x.shape[0] // tm for x, tm in zip(xs, tms)]
    starts = [sum(steps[:i]) for i in range(len(steps))]
    args, in_specs, out_specs, out_shape = [], [], [], []
    for x, pos, seq_len, tm, start, n in zip(xs, poss, seq_lens, tms, starts, steps):
        ang = pos.astype(F32)[:, None] * inv[None, :]
        cos = jnp.cos(ang)
        sin = jnp.sin(ang)
        cos2 = jnp.concatenate([cos, cos], axis=-1)
        sin2 = jnp.concatenate([-sin, sin], axis=-1)
        if seq_len < tm:
            cos2 = jnp.tile(cos2, (tm // seq_len, 1))
            sin2 = jnp.tile(sin2, (tm // seq_len, 1))
        n_tab = cos2.shape[0] // tm
        index, row = _group_rows(start, n, tm)
        tab = pl.BlockSpec((tm, RET_QK_DIM),
                           lambda i, index=index, n_tab=n_tab: (index(i) % n_tab, 0))
        args += [x, cos2, sin2]
        in_specs += [row(d), tab, tab]
        out_specs += [row(dq), row(dq), row(dv), row(dv)]
        out_shape += [jax.ShapeDtypeStruct((x.shape[0], width), BF16) for width in (dq, dq, dv, dv)]
    outs = pl.pallas_call(
        functools.partial(_in_odd_kernel, starts=tuple(starts), steps=tuple(steps)),
        grid=(sum(steps),), in_specs=in_specs + [_resident((1, d)), _layer_rows(w, layer)],
        out_specs=out_specs, out_shape=out_shape,
        compiler_params=_compiler_params(("arbitrary",)),
        name="in_odd")(*args, g.reshape(1, d), w)
    return [outs[4 * i:4 * i + 4] for i in range(len(xs))]


def _odd_mix_kernel(*refs, bt, lt, has_state, single_step, n_alias):
    n_in = 6 + (1 if has_state else 0)
    (q_ref, k_ref, v_ref, sg_ref, gng_ref, gnb_ref) = refs[:6]
    if has_state:
        r_prev_ref = refs[6]
    (y_ref, r_ref, o_s, dec_s, inn_s, kdec_s) = refs[n_in + n_alias:]
    c = lt
    l = pl.program_id(1)

    lgs = [math.log(1.0 - 2.0 ** (-5.0 - h)) for h in range(RET_HEADS)]

    @pl.when((pl.program_id(0) == 0) & (l == 0))
    def _():
        ri = lax.broadcasted_iota(jnp.int32, (c, c), 0)
        ci = lax.broadcasted_iota(jnp.int32, (c, c), 1)
        diff = (ri - ci).astype(F32)
        idx = lax.broadcasted_iota(jnp.int32, (c, RET_QK_DIM), 0).astype(F32)
        for h in range(RET_HEADS):
            dec_s[h] = jnp.where(diff >= 0, jnp.exp(jnp.maximum(diff, 0.0) * lgs[h]), 0.0)
            inn_s[h] = jnp.exp((idx + 1.0) * lgs[h])
            kdec_s[h] = jnp.exp((c - 1.0 - idx) * lgs[h])

    @pl.when(l == 0)
    def _():
        if not has_state:
            r_ref[...] = jnp.zeros(r_ref.shape, F32)
        elif not single_step:
            r_ref[...] = r_prev_ref[...]

    hg = 2 if c >= LANES else RET_HEADS
    r_in = r_prev_ref if (has_state and single_step) else r_ref
    def seq_rows(ref, b, lo, hi):
        if bt == 1:
            return ref[:, lo:hi]
        return ref[:, lo:hi].astype(F32)[b * c:(b + 1) * c]

    for b in range(bt):
        rs = slice(b * c, (b + 1) * c)
        for h0 in range(0, RET_HEADS, hg):
            heads = range(h0, h0 + hg)
            qs = {h: seq_rows(q_ref, b, h * RET_QK_DIM, (h + 1) * RET_QK_DIM) for h in heads}
            ks = {h: seq_rows(k_ref, b, h * RET_QK_DIM, (h + 1) * RET_QK_DIM) for h in heads}
            vs = {h: seq_rows(v_ref, b, h * RET_V_DIM, (h + 1) * RET_V_DIM) for h in heads}
            rr = {h: r_in[b, h] for h in heads}
            att = {h: _mm_nt(qs[h], ks[h]) * dec_s[h] for h in heads}
            qr = {h: _mm(qs[h].astype(F32) * inn_s[h], rr[h]) for h in heads}
            kv = {h: _mm_tn(ks[h].astype(F32) * kdec_s[h], vs[h]) for h in heads}
            for h in heads:
                r_ref[b, h] = rr[h] * math.exp(c * lgs[h]) + kv[h]
            for h in heads:
                o_s[rs, h * RET_V_DIM:(h + 1) * RET_V_DIM] = _mm(att[h], vs[h]) + qr[h]

    for h in range(RET_HEADS):
        sl = slice(h * RET_V_DIM, (h + 1) * RET_V_DIM)
        o = o_s[:, sl]
        mu = jnp.mean(o, axis=-1, keepdims=True)
        var = jnp.mean(jnp.square(o - mu), axis=-1, keepdims=True)
        o = (o - mu) * lax.rsqrt(var + GN_EPS) * gng_ref[:, sl] + gnb_ref[:, sl]
        y_ref[:, sl] = (sg_ref[:, sl].astype(F32) * o).astype(y_ref.dtype)


def _odd_mix(q, k, v, sg, gn_g, gn_b, r_prev, prev_out, layer, n_layers, batch, length, bt, lt):
    dq = RET_HEADS * RET_QK_DIM
    dv = RET_HEADS * RET_V_DIM
    nl = length // lt
    rows = bt * lt
    assert bt == 1 or nl == 1
    grid = (batch // bt, nl)
    has_state = r_prev is not None
    row = lambda width: pl.BlockSpec((rows, width), lambda b, l: (b * nl + l, 0))
    in_specs = [row(dq), row(dq), row(dv), row(dv), _resident((1, dv)), _resident((1, dv))]
    args = [q, k, v, sg, gn_g.reshape(1, dv), gn_b.reshape(1, dv)]
    state_block = pl.BlockSpec((None, bt, RET_HEADS, RET_QK_DIM, RET_V_DIM),
                               lambda b, l: (layer, b, 0, 0, 0))
    if has_state:
        in_specs.append(state_block)
        args.append(r_prev)
    aliases = {}
    if prev_out is not None:
        aliases[len(args)] = 1
        in_specs.append(pl.BlockSpec(memory_space=pl.ANY))
        args.append(prev_out)
    out_specs = [row(dv), state_block]
    out_shape = [jax.ShapeDtypeStruct((batch * length, dv), BF16),
                 jax.ShapeDtypeStruct((n_layers, batch, RET_HEADS, RET_QK_DIM, RET_V_DIM), F32)]
    scratch = [pltpu.VMEM((rows, dv), F32),
               pltpu.VMEM((RET_HEADS, lt, lt), F32),
               pltpu.VMEM((RET_HEADS, lt, RET_QK_DIM), F32),
               pltpu.VMEM((RET_HEADS, lt, RET_QK_DIM), F32)]
    return pl.pallas_call(
        functools.partial(_odd_mix_kernel, bt=bt, lt=lt, has_state=has_state,
                          single_step=nl == 1, n_alias=len(aliases)),
        grid=grid, in_specs=in_specs, out_specs=out_specs, out_shape=out_shape,
        scratch_shapes=scratch, input_output_aliases=aliases,
        compiler_params=_compiler_params(("arbitrary", "arbitrary")),
        name="odd_mix")(*args)


def _out_ffn_kernel(*refs, n_y, starts, steps, final):
    ng = len(steps)
    per = 1 + n_y
    rest = refs[ng * per:]
    wo_refs = rest[:n_y]
    g_ref, wg_ref, wu_ref, wd_ref, gf_ref = rest[n_y:n_y + 5]
    o_refs = rest[n_y + 5:]

    def body(x_ref, y_refs, o_ref):
        x1 = x_ref[...]
        for y_ref, wo_ref in zip(y_refs, wo_refs):
            x1 = x1 + jnp.dot(y_ref[...].astype(BF16), wo_ref[...], preferred_element_type=F32)
        h = _rmsnorm(x1, g_ref[...]).astype(BF16)
        gate = jnp.dot(h, wg_ref[...], preferred_element_type=F32)
        up = jnp.dot(h, wu_ref[...], preferred_element_type=F32)
        a = (_silu(gate) * up).astype(BF16)
        x2 = x1 + jnp.dot(a, wd_ref[...], preferred_element_type=F32)
        if final:
            x2 = _rmsnorm(x2, gf_ref[...])
        o_ref[...] = x2

    _for_each_group(starts, steps, [
        functools.partial(body, refs[g * per], refs[g * per + 1:(g + 1) * per], o_refs[g])
        for g in range(ng)])


def _out_ffn(xs, yss, wo, wo_layer, g, wg, wu, wd, ffn_layer, gf, final, tms):
    d = xs[0].shape[1]
    n_y = len(yss[0])
    assert len({y.shape[1] for ys in yss for y in ys}) == 1
    steps = [x.shape[0] // tm for x, tm in zip(xs, tms)]
    starts = [sum(steps[:i]) for i in range(len(steps))]
    args, in_specs, out_specs, out_shape = [], [], [], []
    for x, ys, tm, start, n in zip(xs, yss, tms, starts, steps):
        _, row = _group_rows(start, n, tm)
        args += [x] + list(ys)
        in_specs += [row(d)] + [row(y.shape[1]) for y in ys]
        out_specs.append(row(d))
        out_shape.append(jax.ShapeDtypeStruct(x.shape, F32))
    in_specs += ([_layer_rows(wo, wo_layer, j, n_y) for j in range(n_y)]
                 + [_resident((1, d)), _layer_rows(wg, ffn_layer), _layer_rows(wu, ffn_layer),
                    _layer_rows(wd, ffn_layer), _resident((1, d))])
    return pl.pallas_call(
        functools.partial(_out_ffn_kernel, n_y=n_y, starts=tuple(starts), steps=tuple(steps),
                          final=final),
        grid=(sum(steps),), in_specs=in_specs, out_specs=out_specs, out_shape=out_shape,
        compiler_params=_compiler_params(("arbitrary",)),
        name="out_ffn")(*args, *([wo] * n_y), g.reshape(1, d), wg, wu, wd, gf.reshape(1, d))


def _tiles(batch, length):
    tm = min(batch * length, ROW_TILE)
    if length % CHUNK == 0:
        lt = min(length, 256)
        return dict(tm=tm, in_even=(1, min(length, tm)), even=(math.gcd(batch, 2), lt),
                    odd=(1, lt))
    return dict(tm=tm, in_even=(max(tm // length, 1), length), even=(min(batch, 16), length),
                odd=(min(batch, 8), length))


def _run_trunk(groups, w):
    depth = w["norm_mix"].shape[0]
    n_even = (depth + 1) // 2
    n_odd = depth // 2
    gs = []
    for x, states, pos in groups:
        batch, length, d = x.shape
        gs.append(dict(batch=batch, length=length, d=d, states=states, pos=pos,
                       t=_tiles(batch, length), x=x.reshape(batch * length, d),
                       conv_out=None,
                       delta_out=None, ret_out=None))
    tms = [g["t"]["tm"] for g in gs]
    for i in range(depth):
        if i % 2 == 0:
            e = i // 2
            for g in gs:
                st = g["states"]
                (ya, q, k, v, sz, gb), g["conv_out"] = _in_even(
                    g["x"].reshape(g["batch"], g["length"], g["d"]), w["norm_mix"][i],
                    w["even_w_main"], w["even_w_ab"], w["even_conv_a"][e], w["even_conv_qkv"][e],
                    w["even_a_log"][e], w["even_dt_bias"][e],
                    None if st is None else (st[0], st[1]), g["conv_out"], e, n_even,
                    *g["t"]["in_even"])
                yb, g["delta_out"] = _even_mix(q, k, v, sz, gb, w["even_dn_norm"][e],
                                               None if st is None else st[2], g["delta_out"],
                                               e, n_even, *g["t"]["even"])
                rows = g["batch"] * g["length"]
                g["ys"] = [ya.reshape(rows, -1), yb.reshape(rows, -1)]
            wo, wo_layer = w["even_w_out"], e
        else:
            o = i // 2
            qkvs = _in_odd([g["x"] for g in gs], w["norm_mix"][i], w["odd_w_in"], o,
                           [g["pos"] for g in gs], [g["length"] for g in gs], tms)
            for g, (q, k, v, sg) in zip(gs, qkvs):
                st = g["states"]
                y, g["ret_out"] = _odd_mix(q, k, v, sg, w["odd_gn_g"][o], w["odd_gn_b"][o],
                                           None if st is None else st[3], g["ret_out"], o, n_odd,
                                           g["batch"], g["length"], *g["t"]["odd"])
                g["ys"] = [y]
            wo, wo_layer = w["odd_w_out"], o
        xs = _out_ffn([g["x"] for g in gs], [g["ys"] for g in gs], wo, wo_layer, w["norm_ffn"][i],
                      w["ffn_w_gate"], w["ffn_w_up"], w["ffn_w_down"], i, w["final_norm"],
                      i == depth - 1, tms)
        for g, x in zip(gs, xs):
            g["x"] = x
    return [(g["x"].reshape(g["batch"], g["length"], g["d"]),) + g["conv_out"]
            + (g["delta_out"], g["ret_out"]) for g in gs]


def kernel(x_prompt, x_sample, state_conv_a, state_conv_qkv, state_delta, state_ret, norm_mix,
           norm_ffn, final_norm, even_w_in, even_conv_a, even_conv_qkv, even_a_log, even_dt_bias,
           even_dn_norm, even_w_out, odd_w_in, odd_gn_g, odd_gn_b, odd_w_out, ffn_w_gate, ffn_w_up,
           ffn_w_down):
    n_main = even_w_in.shape[-1] - 2 * DN_HEADS
    w = dict(norm_mix=norm_mix, norm_ffn=norm_ffn, final_norm=final_norm,
             even_w_main=even_w_in.astype(BF16),
             even_w_ab=jnp.pad(even_w_in[:, :, n_main:],
                               ((0, 0), (0, 0), (0, LANES - 2 * DN_HEADS))).astype(BF16),
             even_conv_a=even_conv_a, even_conv_qkv=even_conv_qkv, even_a_log=even_a_log,
             even_dt_bias=even_dt_bias, even_dn_norm=even_dn_norm,
             even_w_out=even_w_out.astype(BF16), odd_w_in=odd_w_in.astype(BF16),
             odd_gn_g=odd_gn_g, odd_gn_b=odd_gn_b, odd_w_out=odd_w_out.astype(BF16),
             ffn_w_gate=ffn_w_gate.astype(BF16), ffn_w_up=ffn_w_up.astype(BF16),
             ffn_w_down=ffn_w_down.astype(BF16))
    lp = x_prompt.shape[1]
    ls = x_sample.shape[1]
    pos_p = jnp.arange(lp, dtype=jnp.int32)
    pos_s = PAST_LEN + jnp.arange(ls, dtype=jnp.int32)
    (y_prompt, ca_p, cq_p, d_p, r_p), (y_sample, ca_s, cq_s, d_s, r_s) = _run_trunk(
        [(x_prompt, None, pos_p),
         (x_sample, (state_conv_a, state_conv_qkv, state_delta, state_ret), pos_s)], w)
    return (y_prompt, y_sample, ca_p, cq_p, d_p, r_p, ca_s, cq_s, d_s, r_s)
```

```python
import functools
import math

import jax
import jax.numpy as jnp
from jax import lax
from jax.experimental import pallas as pl
from jax.experimental.pallas import tpu as pltpu

F32 = jnp.float32
BF16 = jnp.bfloat16

EPS = 1e-6
GN_EPS = 1e-5
ROPE_BASE = 10000.0
PAST_LEN = 16384
CHUNK = 64

CONV_A_W = 3
DN_CONV_W = 4
DN_HEADS = 4
DN_HEAD_DIM = 128
RET_HEADS = 8
RET_QK_DIM = 128
RET_V_DIM = 256

LANES = 128
SUBLANES = 8
VMEM_LIMIT_BYTES = 56 * 1024 * 1024
DN_ROWS = 64
INTRA_WAVE = 8
ROW_TILE = 512


def _compiler_params(semantics):
    return pltpu.CompilerParams(dimension_semantics=semantics,
                                vmem_limit_bytes=VMEM_LIMIT_BYTES)


def _resident(shape):
    nd = len(shape)
    return pl.BlockSpec(shape, lambda *_: (0,) * nd, pipeline_mode=pl.Buffered(1))


def _layer_rows(stack, layer, part=0, parts=1):
    _, k, n = stack.shape
    return pl.BlockSpec((None, k // parts, n), lambda *_: (layer, part, 0),
                        pipeline_mode=pl.Buffered(1))


def _group_rows(start, steps, tm):
    def index(i):
        return jnp.minimum(jnp.maximum(i - start, 0), steps - 1)
    return index, (lambda width: pl.BlockSpec((tm, width), lambda i: (index(i), 0)))


def _for_each_group(starts, steps, bodies):
    i = pl.program_id(0)
    if len(bodies) == 1:
        bodies[0]()
        return
    for start, n, body in zip(starts, steps, bodies):
        pl.when((i >= start) & (i < start + n))(body)


def _mm(a, b):
    return jnp.dot(a.astype(BF16), b.astype(BF16), preferred_element_type=F32)


def _mm_nt(a, b):
    return lax.dot_general(a.astype(BF16), b.astype(BF16), (((1,), (1,)), ((), ())),
                           preferred_element_type=F32)


def _mm_tn(a, b):
    return lax.dot_general(a.astype(BF16), b.astype(BF16), (((0,), (0,)), ((), ())),
                           preferred_element_type=F32)


def _sigmoid(x):
    return 0.5 * jnp.tanh(0.5 * x) + 0.5


def _silu(x):
    hx = 0.5 * x
    return hx * jnp.tanh(hx) + hx


def _rmsnorm(x, g):
    return x * lax.rsqrt(jnp.mean(x * x, axis=-1, keepdims=True) + EPS) * g


def _token_dtype(lt):
    return BF16 if lt % (2 * SUBLANES) == 0 else F32


def _in_even_kernel(*refs, bt, lt, has_state, n_alias):
    n_in = 8 + (2 if has_state else 0)
    (x_ref, g_ref, wm_ref, wab_ref, caw_ref, cqw_ref, alog_ref, dt_ref) = refs[:8]
    if has_state:
        ca_prev_ref, cq_prev_ref = refs[8:10]
    (ya_ref, q_ref, k_ref, v_ref, sz_ref, gb_ref, ca_new_ref, cq_new_ref,
     ua_s, qkv_s) = refs[n_in + n_alias:]
    hd = DN_HEAD_DIM
    dn = DN_HEADS * hd
    ca = ua_s.shape[-1]
    d = x_ref.shape[-1]
    l = pl.program_id(1)
    nl = pl.num_programs(1)
    rows = bt * lt
    pad = SUBLANES

    @pl.when(l == 0)
    def _():
        ua_s[:, 0:pad, :] = jnp.zeros((bt, pad, ca), F32)
        qkv_s[:, 0:pad, :] = jnp.zeros((bt, pad, 3 * dn), F32)
        if has_state:
            ua_s[:, pad - (CONV_A_W - 1):pad, :] = ca_prev_ref[...]
            qkv_s[:, pad - (DN_CONV_W - 1):pad, :] = cq_prev_ref[...]

    @pl.when(l > 0)
    def _():
        ua_s[:, 0:pad, :] = ua_s[:, lt:lt + pad, :]
        qkv_s[:, 0:pad, :] = qkv_s[:, lt:lt + pad, :]

    h = _rmsnorm(x_ref[...].reshape(rows, d), g_ref[...]).astype(BF16)

    def proj(lo, hi):
        return jnp.dot(h, wm_ref[:, lo:hi], preferred_element_type=F32)

    qkv0 = 3 * ca

    def proj_qkv(part):
        cols = slice(part * dn, (part + 1) * dn)
        qkv_s[:, pad:pad + lt, cols] = proj(qkv0 + part * dn, qkv0 + (part + 1) * dn).reshape(
            bt, lt, dn)

    def delayed(scr, back, cols):
        if back == 0:
            return scr[:, pad:pad + lt, cols]
        x = scr[:, :, cols]
        width = x.shape[-1]
        n_grp = lt // SUBLANES
        rot = pltpu.roll(x.reshape(bt * (n_grp + 1), SUBLANES, width), back, 1)
        rot = rot.reshape(bt, n_grp + 1, SUBLANES, width)
        own = lax.broadcasted_iota(jnp.int32, (SUBLANES, width), 0) >= back
        return jnp.where(own, rot[:, 1:], rot[:, :n_grp]).reshape(bt, lt, width)

    def conv_silu(part):
        cols = slice(part * dn, (part + 1) * dn)
        conv = cqw_ref[0:1, cols] * delayed(qkv_s, DN_CONV_W - 1, cols)
        for i in range(1, DN_CONV_W):
            conv = conv + cqw_ref[i:i + 1, cols] * delayed(qkv_s, DN_CONV_W - 1 - i, cols)
        return _silu(conv)

    def l2norm_to(t, out_ref, scale):
        for hh in range(DN_HEADS):
            sl = slice(hh * hd, (hh + 1) * hd)
            th = t[:, :, sl]
            out_ref[:, :, sl] = th * (lax.rsqrt(jnp.sum(th * th, axis=-1, keepdims=True) + EPS)
                                      * scale)

    proj_qkv(0)
    proj_qkv(1)
    gate_c = proj(ca, 2 * ca)
    l2norm_to(conv_silu(0), q_ref, hd ** -0.5)
    proj_qkv(2)
    h_a = proj(2 * ca, 3 * ca)
    l2norm_to(conv_silu(1), k_ref, 1.0)
    gate_b = proj(0, ca)
    z = proj(3 * ca + 3 * dn, 3 * ca + 4 * dn)
    v_ref[...] = conv_silu(2)

    ua_s[:, pad:pad + lt, :] = (gate_c * h_a).reshape(bt, lt, ca)
    conv = caw_ref[0:1, :] * delayed(ua_s, CONV_A_W - 1, slice(0, ca))
    for i in range(1, CONV_A_W):
        conv = conv + caw_ref[i:i + 1, :] * delayed(ua_s, CONV_A_W - 1 - i, slice(0, ca))
    ya_ref[...] = (gate_b.reshape(bt, lt, ca) * conv).astype(ya_ref.dtype)

    pab = jnp.dot(h, wab_ref[...], preferred_element_type=F32)
    sz_ref[...] = _silu(z).reshape(bt, lt, dn)
    sp = jnp.maximum(pab + dt_ref[...], 0.0) + jnp.log(1.0 + jnp.exp(-jnp.abs(pab + dt_ref[...])))
    col = lax.broadcasted_iota(jnp.int32, pab.shape, 1)
    gb = jnp.where(col < DN_HEADS, -jnp.exp(alog_ref[...]) * sp, _sigmoid(pab))
    gb_ref[...] = gb.reshape(bt, lt, LANES)

    @pl.when(l == nl - 1)
    def _():
        ca_new_ref[...] = ua_s[:, pad + lt - (CONV_A_W - 1):pad + lt, :]
        cq_new_ref[...] = qkv_s[:, pad + lt - (DN_CONV_W - 1):pad + lt, :]


def _in_even(x, g, w_main, w_ab, conv_a_w, conv_qkv_w, a_log, dt_bias, states, prev_out,
             layer, n_layers, bt, lt):
    batch, length, d = x.shape
    ca = conv_a_w.shape[1]
    dn = DN_HEADS * DN_HEAD_DIM
    nl = length // lt
    grid = (batch // bt, nl)
    has_state = states is not None
    tok_map = lambda b, l: (b, l, 0)
    tok = lambda width: pl.BlockSpec((bt, lt, width), tok_map)
    alog_row = jnp.zeros((1, LANES), F32).at[0, :DN_HEADS].set(a_log)
    dt_row = jnp.zeros((1, LANES), F32).at[0, :DN_HEADS].set(dt_bias)
    in_specs = [tok(d), _resident((1, d)), _layer_rows(w_main, layer), _layer_rows(w_ab, layer),
                _resident(conv_a_w.shape), _resident(conv_qkv_w.shape),
                _resident((1, LANES)), _resident((1, LANES))]
    args = [x, g.reshape(1, d), w_main, w_ab, conv_a_w, conv_qkv_w, alog_row, dt_row]
    ca_block = pl.BlockSpec((None, bt, CONV_A_W - 1, ca), lambda b, l: (layer, b, 0, 0))
    cq_block = pl.BlockSpec((None, bt, DN_CONV_W - 1, 3 * dn), lambda b, l: (layer, b, 0, 0))
    if has_state:
        in_specs += [ca_block, cq_block]
        args += list(states)
    aliases = {}
    if prev_out is not None:
        for j, a in enumerate(prev_out):
            aliases[len(args)] = 6 + j
            in_specs.append(pl.BlockSpec(memory_space=pl.ANY))
            args.append(a)
    out_specs = [tok(ca), tok(dn), tok(dn), tok(dn), tok(dn), tok(LANES), ca_block, cq_block]
    tokshape = lambda width, dt: jax.ShapeDtypeStruct((batch, length, width), dt)
    out_shape = [tokshape(ca, _token_dtype(lt)), tokshape(dn, F32), tokshape(dn, F32),
                 tokshape(dn, F32), tokshape(dn, F32), tokshape(LANES, F32),
                 jax.ShapeDtypeStruct((n_layers, batch, CONV_A_W - 1, ca), F32),
                 jax.ShapeDtypeStruct((n_layers, batch, DN_CONV_W - 1, 3 * dn), F32)]
    scratch = [pltpu.VMEM((bt, lt + SUBLANES, ca), F32),
               pltpu.VMEM((bt, lt + SUBLANES, 3 * dn), F32)]
    outs = pl.pallas_call(
        functools.partial(_in_even_kernel, bt=bt, lt=lt, has_state=has_state,
                          n_alias=len(aliases)),
        grid=grid, in_specs=in_specs, out_specs=out_specs, out_shape=out_shape,
        scratch_shapes=scratch, input_output_aliases=aliases,
        compiler_params=_compiler_params(("arbitrary", "arbitrary")),
        name="in_even")(*args)
    return outs[:6], tuple(outs[6:])


def _even_mix_kernel(*refs, bt, lt, c, has_state, single_step, n_alias):
    n_in = 6 + (1 if has_state else 0)
    (q_ref, k_ref, v_ref, sz_ref, gb_ref, dng_ref) = refs[:6]
    if has_state:
        s_prev_ref = refs[6]
    (y_ref, s_ref, o_s, wq_s, u_s, kd_s, aqk_s, egl_s) = refs[n_in + n_alias:]
    hd = DN_HEAD_DIM
    l = pl.program_id(1)
    rows = bt * lt

    @pl.when(l == 0)
    def _():
        if not has_state:
            s_ref[...] = jnp.zeros(s_ref.shape, F32)
        elif not single_step:
            s_ref[...] = s_prev_ref[...]

    sc = DN_ROWS
    nb = sc // c
    nsc = rows // sc
    groups_per_seq = max(lt // sc, 1)
    shift = int(math.log2(c))
    ri = lax.broadcasted_iota(jnp.int32, (sc, sc), 0)
    ci = lax.broadcasted_iota(jnp.int32, (sc, sc), 1)
    rblk = lax.shift_right_logical(ri, shift)
    same = rblk == lax.shift_right_logical(ci, shift)
    upper = (ri <= ci) & same
    causal = (ri >= ci) & same
    strict = (ri > ci) & same
    eye = ri == ci
    last = ci == (lax.shift_left(rblk, shift) + (c - 1))
    eye_f = eye.astype(F32)
    n_sq = shift - 1

    def group(ref, i, lo, hi):
        if nb == 1:
            b, w = divmod(i, groups_per_seq)
            return ref[b, w * sc:(w + 1) * sc, lo:hi]
        return ref[i * nb:(i + 1) * nb, :, lo:hi].reshape(sc, hi - lo)

    def intra(groups):
        units = [(i, h) for i in groups for h in range(DN_HEADS)]
        xs, ts, rhs, qes = {}, {}, {}, {}
        for (i, h) in units:
            rs = slice(i * sc, (i + 1) * sc)
            sl = slice(h * hd, (h + 1) * hd)
            q = group(q_ref, i, h * hd, (h + 1) * hd)
            k = group(k_ref, i, h * hd, (h + 1) * hd)
            v = group(v_ref, i, h * hd, (h + 1) * hd)
            gb = group(gb_ref, i, 0, LANES)
            beta = gb[:, DN_HEADS + h:DN_HEADS + h + 1]
            g_col = gb[:, h:h + 1]
            g_row = jnp.sum(jnp.where(upper, g_col, 0.0), axis=0, keepdims=True)
            g_rows = jnp.broadcast_to(g_row, (sc, sc))
            g_cum = jnp.sum(jnp.where(eye, g_rows, 0.0), axis=1, keepdims=True)
            g_last = jnp.sum(jnp.where(last, g_rows, 0.0), axis=1, keepdims=True)
            decay = jnp.exp(jnp.where(causal, g_cum - g_rows, -jnp.inf))
            kb = k * beta
            kq = _mm_nt(jnp.concatenate([kb, q], axis=0), k)
            e_g = jnp.exp(g_cum)
            xs[i, h] = -(kq[0:sc] * jnp.where(strict, decay, 0.0))
            ts[i, h] = eye_f + xs[i, h]
            rhs[i, h] = jnp.concatenate([kb * e_g, v * beta], axis=1)
            qes[i, h] = q * e_g
            kd_s[rs, sl] = k * jnp.exp(g_last - g_cum)
            aqk_s[i, h] = kq[sc:2 * sc] * decay
            egl_s[i, h] = jnp.broadcast_to(jnp.exp(g_last), (sc, hd))
        for _ in range(n_sq):
            for u in units:
                xs[u] = _mm(xs[u], xs[u])
            for u in units:
                ts[u] = ts[u] + _mm(ts[u], xs[u])
        for (i, h) in units:
            rs = slice(i * sc, (i + 1) * sc)
            sl = slice(h * hd, (h + 1) * hd)
            wu = _mm(ts[i, h], rhs[i, h])
            w = wu[:, 0:hd]
            qe = qes[i, h]
            for j in range(nb):
                wq_s[i * nb + j, h, 0:c] = w[j * c:(j + 1) * c]
                wq_s[i * nb + j, h, c:2 * c] = qe[j * c:(j + 1) * c]
            u_s[rs, sl] = wu[:, hd:2 * hd]

    s_in = s_prev_ref if (has_state and single_step) else s_ref

    def inter(groups):
        units = [(i, h, j) for i in groups for h in range(DN_HEADS) for j in range(nb)]
        bidx = lambda i, j: (i * nb + j) if nb > 1 else i // groups_per_seq
        rows_of = lambda i, j: slice(i * sc + j * c, i * sc + (j + 1) * c)
        cols_of = lambda h: slice(h * hd, (h + 1) * hd)
        ss = {(i, h, j): s_in[bidx(i, j), h] for (i, h, j) in units}
        tqs = {(i, h, j): _mm(wq_s[i * nb + j, h], ss[i, h, j]) for (i, h, j) in units}
        us = {(i, h, j): u_s[rows_of(i, j), cols_of(h)] - tqs[i, h, j][0:c] for (i, h, j) in units}
        upd = {(i, h, j): _mm_tn(kd_s[rows_of(i, j), cols_of(h)], us[i, h, j])
               for (i, h, j) in units}
        for (i, h, j) in units:
            s_ref[bidx(i, j), h] = (ss[i, h, j] * egl_s[i, h, j * c:j * c + 1, :] + upd[i, h, j])
        for i in groups:
            for h in range(DN_HEADS):
                if nb > 1:
                    u = jnp.concatenate([us[i, h, j] for j in range(nb)], axis=0)
                    qs = jnp.concatenate([tqs[i, h, j][c:2 * c] for j in range(nb)], axis=0)
                else:
                    u = us[i, h, 0]
                    qs = tqs[i, h, 0][c:2 * c]
                o_s[i * sc:(i + 1) * sc, cols_of(h)] = qs + _mm(aqk_s[i, h], u)

    wave = INTRA_WAVE
    for i0 in range(0, nsc, wave):
        intra(range(i0, min(i0 + wave, nsc)))
    if nb > 1:
        for i in range(nsc):
            inter([i])
    else:
        for w in range(groups_per_seq):
            inter([b * groups_per_seq + w for b in range(bt)])

    for h in range(DN_HEADS):
        sl = slice(h * hd, (h + 1) * hd)
        o = o_s[:, sl]
        o = (o * lax.rsqrt(jnp.mean(o * o, axis=-1, keepdims=True) + EPS) * dng_ref[...]
             * sz_ref[:, :, sl].reshape(rows, hd))
        y_ref[:, :, sl] = o.reshape(bt, lt, hd).astype(y_ref.dtype)


def _even_mix(q, k, v, sz, gb, dn_norm, s_prev, prev_out, layer, n_layers, bt, lt):
    batch, length, dn = q.shape
    c = CHUNK if length % CHUNK == 0 else length
    hd = DN_HEAD_DIM
    nl = length // lt
    rows = bt * lt
    assert rows % DN_ROWS == 0 and DN_ROWS % c == 0 and (c == DN_ROWS or lt == c)
    nsc = rows // DN_ROWS
    nb = DN_ROWS // c
    grid = (batch // bt, nl)
    has_state = s_prev is not None
    tok_map = lambda b, l: (b, l, 0)
    tok = lambda width: pl.BlockSpec((bt, lt, width), tok_map)
    in_specs = [tok(dn), tok(dn), tok(dn), tok(dn), tok(LANES), _resident((1, hd))]
    args = [q, k, v, sz, gb, dn_norm.reshape(1, hd)]
    state_block = pl.BlockSpec((None, bt, DN_HEADS, hd, hd), lambda b, l: (layer, b, 0, 0, 0))
    if has_state:
        in_specs.append(state_block)
        args.append(s_prev)
    aliases = {}
    if prev_out is not None:
        aliases[len(args)] = 1
        in_specs.append(pl.BlockSpec(memory_space=pl.ANY))
        args.append(prev_out)
    out_specs = [tok(dn), state_block]
    out_shape = [jax.ShapeDtypeStruct((batch, length, dn), _token_dtype(lt)),
                 jax.ShapeDtypeStruct((n_layers, batch, DN_HEADS, hd, hd), F32)]
    scratch = [pltpu.VMEM((rows, dn), F32),
               pltpu.VMEM((nsc * nb, DN_HEADS, 2 * c, hd), F32),
               pltpu.VMEM((rows, dn), F32), pltpu.VMEM((rows, dn), F32),
               pltpu.VMEM((nsc, DN_HEADS, DN_ROWS, DN_ROWS), F32),
               pltpu.VMEM((nsc, DN_HEADS, DN_ROWS, hd), F32)]
    return pl.pallas_call(
        functools.partial(_even_mix_kernel, bt=bt, lt=lt, c=c, has_state=has_state,
                          single_step=nl == 1, n_alias=len(aliases)),
        grid=grid, in_specs=in_specs, out_specs=out_specs, out_shape=out_shape,
        scratch_shapes=scratch, input_output_aliases=aliases,
        compiler_params=_compiler_params(("arbitrary", "arbitrary")),
        name="even_mix")(*args)


def _in_odd_kernel(*refs, starts, steps):
    ng = len(steps)
    g_ref, w_ref = refs[3 * ng:3 * ng + 2]
    outs = refs[3 * ng + 2:]
    dq = RET_HEADS * RET_QK_DIM
    dv = RET_HEADS * RET_V_DIM

    def body(x_ref, cos_ref, sin_ref, q_ref, k_ref, v_ref, sg_ref):
        h = _rmsnorm(x_ref[...], g_ref[...]).astype(BF16)

        def proj(lo, hi):
            return jnp.dot(h, w_ref[:, lo:hi], preferred_element_type=F32)

        cos2 = cos_ref[...]
        sin2 = sin_ref[...]

        def rotary(t, out_ref, scale):
            for hh in range(RET_HEADS):
                sl = slice(hh * RET_QK_DIM, (hh + 1) * RET_QK_DIM)
                th = t[:, sl]
                r = th * cos2 + pltpu.roll(th, RET_QK_DIM // 2, 1) * sin2
                out_ref[:, sl] = (r if scale is None else r * scale).astype(out_ref.dtype)

        q = proj(0, dq)
        k = proj(dq, 2 * dq)
        rotary(q, q_ref, None)
        v = proj(2 * dq, 2 * dq + dv)
        rotary(k, k_ref, RET_QK_DIM ** -0.5)
        gate = proj(2 * dq + dv, 2 * dq + 2 * dv)
        v_ref[...] = v.astype(v_ref.dtype)
        sg_ref[...] = _silu(gate).astype(sg_ref.dtype)

    _for_each_group(starts, steps, [
        functools.partial(body, *refs[3 * g:3 * g + 3], *outs[4 * g:4 * g + 4]) for g in range(ng)])


def _in_odd(xs, g, w, layer, poss, seq_lens, tms):
    d = xs[0].shape[1]
    dq = RET_HEADS * RET_QK_DIM
    dv = RET_HEADS * RET_V_DIM
    half = RET_QK_DIM // 2
    inv = ROPE_BASE ** (-jnp.arange(half, dtype=F32) / half)
    steps = [x.shape[0] // tm for x, tm in zip(xs, tms)]
    starts = [sum(steps[:i]) for i in range(len(steps))]
    args, in_specs, out_specs, out_shape = [], [], [], []
    for x, pos, seq_len, tm, start, n in zip(xs, poss, seq_lens, tms, starts, steps):
        ang = pos.astype(F32)[:, None] * inv[None, :]
        cos = jnp.cos(ang)
        sin = jnp.sin(ang)
        cos2 = jnp.concatenate([cos, cos], axis=-1)
        sin2 = jnp.concatenate([-sin, sin], axis=-1)
        if seq_len < tm:
            cos2 = jnp.tile(cos2, (tm // seq_len, 1))
            sin2 = jnp.tile(sin2, (tm // seq_len, 1))
        n_tab = cos2.shape[0] // tm
        index, row = _group_rows(start, n, tm)
        tab = pl.BlockSpec((tm, RET_QK_DIM),
                           lambda i, index=index, n_tab=n_tab: (index(i) % n_tab, 0))
        args += [x, cos2, sin2]
        in_specs += [row(d), tab, tab]
        out_specs += [row(dq), row(dq), row(dv), row(dv)]
        out_shape += [jax.ShapeDtypeStruct((x.shape[0], width), BF16) for width in (dq, dq, dv, dv)]
    outs = pl.pallas_call(
        functools.partial(_in_odd_kernel, starts=tuple(starts), steps=tuple(steps)),
        grid=(sum(steps),), in_specs=in_specs + [_resident((1, d)), _layer_rows(w, layer)],
        out_specs=out_specs, out_shape=out_shape,
        compiler_params=_compiler_params(("arbitrary",)),
        name="in_odd")(*args, g.reshape(1, d), w)
    return [outs[4 * i:4 * i + 4] for i in range(len(xs))]


def _odd_mix_kernel(*refs, bt, lt, has_state, single_step, n_alias):
    n_in = 6 + (1 if has_state else 0)
    (q_ref, k_ref, v_ref, sg_ref, gng_ref, gnb_ref) = refs[:6]
    if has_state:
        r_prev_ref = refs[6]
    (y_ref, r_ref, o_s, dec_s, inn_s, kdec_s) = refs[n_in + n_alias:]
    c = lt
    l = pl.program_id(1)

    lgs = [math.log(1.0 - 2.0 ** (-5.0 - h)) for h in range(RET_HEADS)]

    @pl.when((pl.program_id(0) == 0) & (l == 0))
    def _():
        ri = lax.broadcasted_iota(jnp.int32, (c, c), 0)
        ci = lax.broadcasted_iota(jnp.int32, (c, c), 1)
        diff = (ri - ci).astype(F32)
        idx = lax.broadcasted_iota(jnp.int32, (c, RET_QK_DIM), 0).astype(F32)
        for h in range(RET_HEADS):
            dec_s[h] = jnp.where(diff >= 0, jnp.exp(jnp.maximum(diff, 0.0) * lgs[h]), 0.0)
            inn_s[h] = jnp.exp((idx + 1.0) * lgs[h])
            kdec_s[h] = jnp.exp((c - 1.0 - idx) * lgs[h])

    @pl.when(l == 0)
    def _():
        if not has_state:
            r_ref[...] = jnp.zeros(r_ref.shape, F32)
        elif not single_step:
            r_ref[...] = r_prev_ref[...]

    hg = 1 if c >= LANES else RET_HEADS
    r_in = r_prev_ref if (has_state and single_step) else r_ref
    def norm_gate(o, h):
        sl = slice(h * RET_V_DIM, (h + 1) * RET_V_DIM)
        mu = jnp.mean(o, axis=-1, keepdims=True)
        var = jnp.mean(jnp.square(o - mu), axis=-1, keepdims=True)
        o = (o - mu) * lax.rsqrt(var + GN_EPS) * gng_ref[:, sl] + gnb_ref[:, sl]
        y_ref[:, sl] = (sg_ref[:, sl].astype(F32) * o).astype(y_ref.dtype)

    def seq_rows(ref, b, lo, hi):
        if bt == 1:
            return ref[:, lo:hi]
        return ref[:, lo:hi].astype(F32)[b * c:(b + 1) * c]

    for b in range(bt):
        rs = slice(b * c, (b + 1) * c)
        for h0 in range(0, RET_HEADS, hg):
            heads = range(h0, h0 + hg)
            qs = {h: seq_rows(q_ref, b, h * RET_QK_DIM, (h + 1) * RET_QK_DIM) for h in heads}
            ks = {h: seq_rows(k_ref, b, h * RET_QK_DIM, (h + 1) * RET_QK_DIM) for h in heads}
            vs = {h: seq_rows(v_ref, b, h * RET_V_DIM, (h + 1) * RET_V_DIM) for h in heads}
            rr = {h: r_in[b, h] for h in heads}
            att = {h: _mm_nt(qs[h], ks[h]) * dec_s[h] for h in heads}
            qr = {h: _mm(qs[h].astype(F32) * inn_s[h], rr[h]) for h in heads}
            kv = {h: _mm_tn(ks[h].astype(F32) * kdec_s[h], vs[h]) for h in heads}
            for h in heads:
                r_ref[b, h] = rr[h] * math.exp(c * lgs[h]) + kv[h]
            for h in heads:
                o = _mm(att[h], vs[h]) + qr[h]
                if bt == 1:
                    norm_gate(o, h)
                else:
                    o_s[rs, h * RET_V_DIM:(h + 1) * RET_V_DIM] = o

    if bt > 1:
        for h in range(RET_HEADS):
            norm_gate(o_s[:, h * RET_V_DIM:(h + 1) * RET_V_DIM], h)


def _odd_mix(q, k, v, sg, gn_g, gn_b, r_prev, prev_out, layer, n_layers, batch, length, bt, lt):
    dq = RET_HEADS * RET_QK_DIM
    dv = RET_HEADS * RET_V_DIM
    nl = length // lt
    rows = bt * lt
    assert bt == 1 or nl == 1
    grid = (batch // bt, nl)
    has_state = r_prev is not None
    row = lambda width: pl.BlockSpec((rows, width), lambda b, l: (b * nl + l, 0))
    in_specs = [row(dq), row(dq), row(dv), row(dv), _resident((1, dv)), _resident((1, dv))]
    args = [q, k, v, sg, gn_g.reshape(1, dv), gn_b.reshape(1, dv)]
    state_block = pl.BlockSpec((None, bt, RET_HEADS, RET_QK_DIM, RET_V_DIM),
                               lambda b, l: (layer, b, 0, 0, 0))
    if has_state:
        in_specs.append(state_block)
        args.append(r_prev)
    aliases = {}
    if prev_out is not None:
        aliases[len(args)] = 1
        in_specs.append(pl.BlockSpec(memory_space=pl.ANY))
        args.append(prev_out)
    out_specs = [row(dv), state_block]
    out_shape = [jax.ShapeDtypeStruct((batch * length, dv), BF16),
                 jax.ShapeDtypeStruct((n_layers, batch, RET_HEADS, RET_QK_DIM, RET_V_DIM), F32)]
    scratch = [pltpu.VMEM((rows, dv), F32),
               pltpu.VMEM((RET_HEADS, lt, lt), F32),
               pltpu.VMEM((RET_HEADS, lt, RET_QK_DIM), F32),
               pltpu.VMEM((RET_HEADS, lt, RET_QK_DIM), F32)]
    return pl.pallas_call(
        functools.partial(_odd_mix_kernel, bt=bt, lt=lt, has_state=has_state,
                          single_step=nl == 1, n_alias=len(aliases)),
        grid=grid, in_specs=in_specs, out_specs=out_specs, out_shape=out_shape,
        scratch_shapes=scratch, input_output_aliases=aliases,
        compiler_params=_compiler_params(("arbitrary", "arbitrary")),
        name="odd_mix")(*args)


def _out_ffn_kernel(*refs, n_y, starts, steps, final):
    ng = len(steps)
    per = 1 + n_y
    rest = refs[ng * per:]
    wo_refs = rest[:n_y]
    g_ref, wg_ref, wu_ref, wd_ref, gf_ref = rest[n_y:n_y + 5]
    o_refs = rest[n_y + 5:]

    def body(x_ref, y_refs, o_ref):
        x1 = x_ref[...]
        for y_ref, wo_ref in zip(y_refs, wo_refs):
            x1 = x1 + jnp.dot(y_ref[...].astype(BF16), wo_ref[...], preferred_element_type=F32)
        h = _rmsnorm(x1, g_ref[...]).astype(BF16)
        gate = jnp.dot(h, wg_ref[...], preferred_element_type=F32)
        up = jnp.dot(h, wu_ref[...], preferred_element_type=F32)
        a = (_silu(gate) * up).astype(BF16)
        x2 = x1 + jnp.dot(a, wd_ref[...], preferred_element_type=F32)
        if final:
            x2 = _rmsnorm(x2, gf_ref[...])
        o_ref[...] = x2

    _for_each_group(starts, steps, [
        functools.partial(body, refs[g * per], refs[g * per + 1:(g + 1) * per], o_refs[g])
        for g in range(ng)])


def _out_ffn(xs, yss, wo, wo_layer, g, wg, wu, wd, ffn_layer, gf, final, tms):
    d = xs[0].shape[1]
    n_y = len(yss[0])
    assert len({y.shape[1] for ys in yss for y in ys}) == 1
    steps = [x.shape[0] // tm for x, tm in zip(xs, tms)]
    starts = [sum(steps[:i]) for i in range(len(steps))]
    args, in_specs, out_specs, out_shape = [], [], [], []
    for x, ys, tm, start, n in zip(xs, yss, tms, starts, steps):
        _, row = _group_rows(start, n, tm)
        args += [x] + list(ys)
        in_specs += [row(d)] + [row(y.shape[1]) for y in ys]
        out_specs.append(row(d))
        out_shape.append(jax.ShapeDtypeStruct(x.shape, F32))
    in_specs += ([_layer_rows(wo, wo_layer, j, n_y) for j in range(n_y)]
                 + [_resident((1, d)), _layer_rows(wg, ffn_layer), _layer_rows(wu, ffn_layer),
                    _layer_rows(wd, ffn_layer), _resident((1, d))])
    return pl.pallas_call(
        functools.partial(_out_ffn_kernel, n_y=n_y, starts=tuple(starts), steps=tuple(steps),
                          final=final),
        grid=(sum(steps),), in_specs=in_specs, out_specs=out_specs, out_shape=out_shape,
        compiler_params=_compiler_params(("arbitrary",)),
        name="out_ffn")(*args, *([wo] * n_y), g.reshape(1, d), wg, wu, wd, gf.reshape(1, d))


def _tiles(batch, length):
    tm = min(batch * length, ROW_TILE)
    if length % CHUNK == 0:
        lt = min(length, 256)
        return dict(tm=tm, in_even=(1, min(length, tm)), even=(math.gcd(batch, 2), lt),
                    odd=(1, lt))
    return dict(tm=tm, in_even=(max(tm // length, 1), length), even=(min(batch, 16), length),
                odd=(min(batch, 8), length))


def _run_trunk(groups, w):
    depth = w["norm_mix"].shape[0]
    n_even = (depth + 1) // 2
    n_odd = depth // 2
    gs = []
    for x, states, pos in groups:
        batch, length, d = x.shape
        gs.append(dict(batch=batch, length=length, d=d, states=states, pos=pos,
                       t=_tiles(batch, length), x=x.reshape(batch * length, d),
                       conv_out=None,
                       delta_out=None, ret_out=None))
    tms = [g["t"]["tm"] for g in gs]
    for i in range(depth):
        if i % 2 == 0:
            e = i // 2
            for g in gs:
                st = g["states"]
                (ya, q, k, v, sz, gb), g["conv_out"] = _in_even(
                    g["x"].reshape(g["batch"], g["length"], g["d"]), w["norm_mix"][i],
                    w["even_w_main"], w["even_w_ab"], w["even_conv_a"][e], w["even_conv_qkv"][e],
                    w["even_a_log"][e], w["even_dt_bias"][e],
                    None if st is None else (st[0], st[1]), g["conv_out"], e, n_even,
                    *g["t"]["in_even"])
                yb, g["delta_out"] = _even_mix(q, k, v, sz, gb, w["even_dn_norm"][e],
                                               None if st is None else st[2], g["delta_out"],
                                               e, n_even, *g["t"]["even"])
                rows = g["batch"] * g["length"]
                g["ys"] = [ya.reshape(rows, -1), yb.reshape(rows, -1)]
            wo, wo_layer = w["even_w_out"], e
        else:
            o = i // 2
            qkvs = _in_odd([g["x"] for g in gs], w["norm_mix"][i], w["odd_w_in"], o,
                           [g["pos"] for g in gs], [g["length"] for g in gs], tms)
            for g, (q, k, v, sg) in zip(gs, qkvs):
                st = g["states"]
                y, g["ret_out"] = _odd_mix(q, k, v, sg, w["odd_gn_g"][o], w["odd_gn_b"][o],
                                           None if st is None else st[3], g["ret_out"], o, n_odd,
                                           g["batch"], g["length"], *g["t"]["odd"])
                g["ys"] = [y]
            wo, wo_layer = w["odd_w_out"], o
        xs = _out_ffn([g["x"] for g in gs], [g["ys"] for g in gs], wo, wo_layer, w["norm_ffn"][i],
                      w["ffn_w_gate"], w["ffn_w_up"], w["ffn_w_down"], i, w["final_norm"],
                      i == depth - 1, tms)
        for g, x in zip(gs, xs):
            g["x"] = x
    return [(g["x"].reshape(g["batch"], g["length"], g["d"]),) + g["conv_out"]
            + (g["delta_out"], g["ret_out"]) for g in gs]


def kernel(x_prompt, x_sample, state_conv_a, state_conv_qkv, state_delta, state_ret, norm_mix,
           norm_ffn, final_norm, even_w_in, even_conv_a, even_conv_qkv, even_a_log, even_dt_bias,
           even_dn_norm, even_w_out, odd_w_in, odd_gn_g, odd_gn_b, odd_w_out, ffn_w_gate, ffn_w_up,
           ffn_w_down):
    n_main = even_w_in.shape[-1] - 2 * DN_HEADS
    w = dict(norm_mix=norm_mix, norm_ffn=norm_ffn, final_norm=final_norm,
             even_w_main=even_w_in.astype(BF16),
             even_w_ab=jnp.pad(even_w_in[:, :, n_main:],
                               ((0, 0), (0, 0), (0, LANES - 2 * DN_HEADS))).astype(BF16),
             even_conv_a=even_conv_a, even_conv_qkv=even_conv_qkv, even_a_log=even_a_log,
             even_dt_bias=even_dt_bias, even_dn_norm=even_dn_norm,
             even_w_out=even_w_out.astype(BF16), odd_w_in=odd_w_in.astype(BF16),
             odd_gn_g=odd_gn_g, odd_gn_b=odd_gn_b, odd_w_out=odd_w_out.astype(BF16),
             ffn_w_gate=ffn_w_gate.astype(BF16), ffn_w_up=ffn_w_up.astype(BF16),
             ffn_w_down=ffn_w_down.astype(BF16))
    lp = x_prompt.shape[1]
    ls = x_sample.shape[1]
    pos_p = jnp.arange(lp, dtype=jnp.int32)
    pos_s = PAST_LEN + jnp.arange(ls, dtype=jnp.int32)
    (y_prompt, ca_p, cq_p, d_p, r_p), (y_sample, ca_s, cq_s, d_s, r_s) = _run_trunk(
        [(x_prompt, None, pos_p),
         (x_sample, (state_conv_a, state_conv_qkv, state_delta, state_ret), pos_s)], w)
    return (y_prompt, y_sample, ca_p, cq_p, d_p, r_p, ca_s, cq_s, d_s, r_s)
```

```python
import functools
import math

import jax
import jax.numpy as jnp
from jax import lax
from jax.experimental import pallas as pl
from jax.experimental.pallas import tpu as pltpu

F32 = jnp.float32
BF16 = jnp.bfloat16

EPS = 1e-6
GN_EPS = 1e-5
ROPE_BASE = 10000.0
PAST_LEN = 16384
CHUNK = 64

CONV_A_W = 3
DN_CONV_W = 4
DN_HEADS = 4
DN_HEAD_DIM = 128
RET_HEADS = 8
RET_QK_DIM = 128
RET_V_DIM = 256

LANES = 128
SUBLANES = 8
VMEM_LIMIT_BYTES = 56 * 1024 * 1024
DN_ROWS = 64
INTRA_WAVE = 8
ROW_TILE = 512


def _compiler_params(semantics):
    return pltpu.CompilerParams(dimension_semantics=semantics,
                                vmem_limit_bytes=VMEM_LIMIT_BYTES)


def _resident(shape):
    nd = len(shape)
    return pl.BlockSpec(shape, lambda *_: (0,) * nd, pipeline_mode=pl.Buffered(1))


def _layer_rows(stack, layer, part=0, parts=1):
    _, k, n = stack.shape
    return pl.BlockSpec((None, k // parts, n), lambda *_: (layer, part, 0),
                        pipeline_mode=pl.Buffered(1))


def _group_rows(start, steps, tm):
    def index(i):
        return jnp.minimum(jnp.maximum(i - start, 0), steps - 1)
    return index, (lambda width: pl.BlockSpec((tm, width), lambda i: (index(i), 0)))


def _for_each_group(starts, steps, bodies):
    i = pl.program_id(0)
    if len(bodies) == 1:
        bodies[0]()
        return
    for start, n, body in zip(starts, steps, bodies):
        pl.when((i >= start) & (i < start + n))(body)


def _mm(a, b):
    return jnp.dot(a.astype(BF16), b.astype(BF16), preferred_element_type=F32)


def _mm_nt(a, b):
    return lax.dot_general(a.astype(BF16), b.astype(BF16), (((1,), (1,)), ((), ())),
                           preferred_element_type=F32)


def _mm_tn(a, b):
    return lax.dot_general(a.astype(BF16), b.astype(BF16), (((0,), (0,)), ((), ())),
                           preferred_element_type=F32)


def _sigmoid(x):
    return 0.5 * jnp.tanh(0.5 * x) + 0.5


def _silu_of_twice(hx):
    return hx * jnp.tanh(hx) + hx


def _rmsnorm(x, g):
    return x * lax.rsqrt(jnp.mean(x * x, axis=-1, keepdims=True) + EPS) * g


def _token_dtype(lt):
    return BF16 if lt % (2 * SUBLANES) == 0 else F32


def _in_even_kernel(*refs, bt, lt, has_state, n_alias):
    n_in = 8 + (2 if has_state else 0)
    (x_ref, g_ref, wm_ref, wab_ref, caw_ref, cqw_ref, alog_ref, dt_ref) = refs[:8]
    if has_state:
        ca_prev_ref, cq_prev_ref = refs[8:10]
    (ya_ref, q_ref, k_ref, v_ref, sz_ref, gb_ref, ca_new_ref, cq_new_ref,
     ua_s, qkv_s) = refs[n_in + n_alias:]
    hd = DN_HEAD_DIM
    dn = DN_HEADS * hd
    ca = ua_s.shape[-1]
    d = x_ref.shape[-1]
    l = pl.program_id(1)
    nl = pl.num_programs(1)
    rows = bt * lt
    pad = SUBLANES

    @pl.when(l == 0)
    def _():
        ua_s[:, 0:pad, :] = jnp.zeros((bt, pad, ca), F32)
        qkv_s[:, 0:pad, :] = jnp.zeros((bt, pad, 3 * dn), F32)
        if has_state:
            ua_s[:, pad - (CONV_A_W - 1):pad, :] = ca_prev_ref[...]
            qkv_s[:, pad - (DN_CONV_W - 1):pad, :] = cq_prev_ref[...]

    @pl.when(l > 0)
    def _():
        ua_s[:, 0:pad, :] = ua_s[:, lt:lt + pad, :]
        qkv_s[:, 0:pad, :] = qkv_s[:, lt:lt + pad, :]

    h = _rmsnorm(x_ref[...].reshape(rows, d), g_ref[...]).astype(BF16)

    def proj(lo, hi):
        return jnp.dot(h, wm_ref[:, lo:hi], preferred_element_type=F32)

    qkv0 = 3 * ca

    def proj_qkv(part):
        cols = slice(part * dn, (part + 1) * dn)
        qkv_s[:, pad:pad + lt, cols] = proj(qkv0 + part * dn, qkv0 + (part + 1) * dn).reshape(
            bt, lt, dn)

    def delayed(scr, back, cols):
        if back == 0:
            return scr[:, pad:pad + lt, cols]
        x = scr[:, :, cols]
        width = x.shape[-1]
        n_grp = lt // SUBLANES
        rot = pltpu.roll(x.reshape(bt * (n_grp + 1), SUBLANES, width), back, 1)
        rot = rot.reshape(bt, n_grp + 1, SUBLANES, width)
        own = lax.broadcasted_iota(jnp.int32, (SUBLANES, width), 0) >= back
        return jnp.where(own, rot[:, 1:], rot[:, :n_grp]).reshape(bt, lt, width)

    def conv_silu(part):
        cols = slice(part * dn, (part + 1) * dn)
        conv = cqw_ref[0:1, cols] * delayed(qkv_s, DN_CONV_W - 1, cols)
        for i in range(1, DN_CONV_W):
            conv = conv + cqw_ref[i:i + 1, cols] * delayed(qkv_s, DN_CONV_W - 1 - i, cols)
        return _silu_of_twice(conv)

    def l2norm_to(t, out_ref, scale):
        for hh in range(DN_HEADS):
            sl = slice(hh * hd, (hh + 1) * hd)
            th = t[:, :, sl]
            out_ref[:, :, sl] = th * (lax.rsqrt(jnp.sum(th * th, axis=-1, keepdims=True) + EPS)
                                      * scale)

    proj_qkv(0)
    proj_qkv(1)
    gate_c = proj(ca, 2 * ca)
    l2norm_to(conv_silu(0), q_ref, hd ** -0.5)
    proj_qkv(2)
    h_a = proj(2 * ca, 3 * ca)
    l2norm_to(conv_silu(1), k_ref, 1.0)
    gate_b = proj(0, ca)
    z = proj(3 * ca + 3 * dn, 3 * ca + 4 * dn)
    v_ref[...] = conv_silu(2)

    ua_s[:, pad:pad + lt, :] = (gate_c * h_a).reshape(bt, lt, ca)
    conv = caw_ref[0:1, :] * delayed(ua_s, CONV_A_W - 1, slice(0, ca))
    for i in range(1, CONV_A_W):
        conv = conv + caw_ref[i:i + 1, :] * delayed(ua_s, CONV_A_W - 1 - i, slice(0, ca))
    ya_ref[...] = (gate_b.reshape(bt, lt, ca) * conv).astype(ya_ref.dtype)

    pab = jnp.dot(h, wab_ref[...], preferred_element_type=F32)
    sz_ref[...] = _silu_of_twice(z).reshape(bt, lt, dn)
    sp = jnp.maximum(pab + dt_ref[...], 0.0) + jnp.log(1.0 + jnp.exp(-jnp.abs(pab + dt_ref[...])))
    col = lax.broadcasted_iota(jnp.int32, pab.shape, 1)
    gb = jnp.where(col < DN_HEADS, -jnp.exp(alog_ref[...]) * sp, _sigmoid(pab))
    gb_ref[...] = gb.reshape(bt, lt, LANES)

    @pl.when(l == nl - 1)
    def _():
        ca_new_ref[...] = ua_s[:, pad + lt - (CONV_A_W - 1):pad + lt, :]
        cq_new_ref[...] = qkv_s[:, pad + lt - (DN_CONV_W - 1):pad + lt, :]


def _in_even(x, g, w_main, w_ab, conv_a_w, conv_qkv_w, a_log, dt_bias, states, prev_out,
             layer, n_layers, bt, lt):
    batch, length, d = x.shape
    ca = conv_a_w.shape[1]
    dn = DN_HEADS * DN_HEAD_DIM
    nl = length // lt
    grid = (batch // bt, nl)
    has_state = states is not None
    tok_map = lambda b, l: (b, l, 0)
    tok = lambda width: pl.BlockSpec((bt, lt, width), tok_map)
    alog_row = jnp.zeros((1, LANES), F32).at[0, :DN_HEADS].set(a_log)
    dt_row = jnp.zeros((1, LANES), F32).at[0, :DN_HEADS].set(dt_bias)
    in_specs = [tok(d), _resident((1, d)), _layer_rows(w_main, layer), _layer_rows(w_ab, layer),
                _resident(conv_a_w.shape), _resident(conv_qkv_w.shape),
                _resident((1, LANES)), _resident((1, LANES))]
    args = [x, g.reshape(1, d), w_main, w_ab, conv_a_w, conv_qkv_w, alog_row, dt_row]
    ca_block = pl.BlockSpec((None, bt, CONV_A_W - 1, ca), lambda b, l: (layer, b, 0, 0))
    cq_block = pl.BlockSpec((None, bt, DN_CONV_W - 1, 3 * dn), lambda b, l: (layer, b, 0, 0))
    if has_state:
        in_specs += [ca_block, cq_block]
        args += list(states)
    aliases = {}
    if prev_out is not None:
        for j, a in enumerate(prev_out):
            aliases[len(args)] = 6 + j
            in_specs.append(pl.BlockSpec(memory_space=pl.ANY))
            args.append(a)
    out_specs = [tok(ca), tok(dn), tok(dn), tok(dn), tok(dn), tok(LANES), ca_block, cq_block]
    tokshape = lambda width, dt: jax.ShapeDtypeStruct((batch, length, width), dt)
    out_shape = [tokshape(ca, _token_dtype(lt)), tokshape(dn, F32), tokshape(dn, F32),
                 tokshape(dn, F32), tokshape(dn, F32), tokshape(LANES, F32),
                 jax.ShapeDtypeStruct((n_layers, batch, CONV_A_W - 1, ca), F32),
                 jax.ShapeDtypeStruct((n_layers, batch, DN_CONV_W - 1, 3 * dn), F32)]
    scratch = [pltpu.VMEM((bt, lt + SUBLANES, ca), F32),
               pltpu.VMEM((bt, lt + SUBLANES, 3 * dn), F32)]
    outs = pl.pallas_call(
        functools.partial(_in_even_kernel, bt=bt, lt=lt, has_state=has_state,
                          n_alias=len(aliases)),
        grid=grid, in_specs=in_specs, out_specs=out_specs, out_shape=out_shape,
        scratch_shapes=scratch, input_output_aliases=aliases,
        compiler_params=_compiler_params(("arbitrary", "arbitrary")),
        name="in_even")(*args)
    return outs[:6], tuple(outs[6:])


def _even_mix_kernel(*refs, bt, lt, c, has_state, single_step, n_alias):
    n_in = 6 + (1 if has_state else 0)
    (q_ref, k_ref, v_ref, sz_ref, gb_ref, dng_ref) = refs[:6]
    if has_state:
        s_prev_ref = refs[6]
    (y_ref, s_ref, o_s, wq_s, u_s, kd_s, aqk_s, egl_s) = refs[n_in + n_alias:]
    hd = DN_HEAD_DIM
    l = pl.program_id(1)
    rows = bt * lt

    @pl.when(l == 0)
    def _():
        if not has_state:
            s_ref[...] = jnp.zeros(s_ref.shape, F32)
        elif not single_step:
            s_ref[...] = s_prev_ref[...]

    sc = DN_ROWS
    nb = sc // c
    nsc = rows // sc
    groups_per_seq = max(lt // sc, 1)
    shift = int(math.log2(c))
    ri = lax.broadcasted_iota(jnp.int32, (sc, sc), 0)
    ci = lax.broadcasted_iota(jnp.int32, (sc, sc), 1)
    rblk = lax.shift_right_logical(ri, shift)
    same = rblk == lax.shift_right_logical(ci, shift)
    upper = (ri <= ci) & same
    causal = (ri >= ci) & same
    strict = (ri > ci) & same
    eye = ri == ci
    last = ci == (lax.shift_left(rblk, shift) + (c - 1))
    eye_f = eye.astype(F32)
    n_sq = shift - 1

    def group(ref, i, lo, hi):
        if nb == 1:
            b, w = divmod(i, groups_per_seq)
            return ref[b, w * sc:(w + 1) * sc, lo:hi]
        return ref[i * nb:(i + 1) * nb, :, lo:hi].reshape(sc, hi - lo)

    def intra(groups):
        units = [(i, h) for i in groups for h in range(DN_HEADS)]
        xs, ts, rhs, qes = {}, {}, {}, {}
        for (i, h) in units:
            rs = slice(i * sc, (i + 1) * sc)
            sl = slice(h * hd, (h + 1) * hd)
            q = group(q_ref, i, h * hd, (h + 1) * hd)
            k = group(k_ref, i, h * hd, (h + 1) * hd)
            v = group(v_ref, i, h * hd, (h + 1) * hd)
            gb = group(gb_ref, i, 0, LANES)
            beta = gb[:, DN_HEADS + h:DN_HEADS + h + 1]
            g_col = gb[:, h:h + 1]
            g_row = jnp.sum(jnp.where(upper, g_col, 0.0), axis=0, keepdims=True)
            g_rows = jnp.broadcast_to(g_row, (sc, sc))
            g_cum = jnp.sum(jnp.where(eye, g_rows, 0.0), axis=1, keepdims=True)
            g_last = jnp.sum(jnp.where(last, g_rows, 0.0), axis=1, keepdims=True)
            decay = jnp.exp(jnp.where(causal, g_cum - g_rows, -jnp.inf))
            kb = k * beta
            kq = _mm_nt(jnp.concatenate([kb, q], axis=0), k)
            e_g = jnp.exp(g_cum)
            xs[i, h] = -(kq[0:sc] * jnp.where(strict, decay, 0.0))
            ts[i, h] = eye_f + xs[i, h]
            rhs[i, h] = jnp.concatenate([kb * e_g, v * beta], axis=1)
            qes[i, h] = q * e_g
            kd_s[rs, sl] = k * jnp.exp(g_last - g_cum)
            aqk_s[i, h] = kq[sc:2 * sc] * decay
            egl_s[i, h] = jnp.broadcast_to(jnp.exp(g_last), (sc, hd))
        for _ in range(n_sq):
            for u in units:
                xs[u] = _mm(xs[u], xs[u])
            for u in units:
                ts[u] = ts[u] + _mm(ts[u], xs[u])
        for (i, h) in units:
            rs = slice(i * sc, (i + 1) * sc)
            sl = slice(h * hd, (h + 1) * hd)
            wu = _mm(ts[i, h], rhs[i, h])
            w = wu[:, 0:hd]
            qe = qes[i, h]
            for j in range(nb):
                wq_s[i * nb + j, h, 0:c] = w[j * c:(j + 1) * c]
                wq_s[i * nb + j, h, c:2 * c] = qe[j * c:(j + 1) * c]
            u_s[rs, sl] = wu[:, hd:2 * hd]

    s_in = s_prev_ref if (has_state and single_step) else s_ref

    def inter(groups):
        units = [(i, h, j) for i in groups for h in range(DN_HEADS) for j in range(nb)]
        bidx = lambda i, j: (i * nb + j) if nb > 1 else i // groups_per_seq
        rows_of = lambda i, j: slice(i * sc + j * c, i * sc + (j + 1) * c)
        cols_of = lambda h: slice(h * hd, (h + 1) * hd)
        ss = {(i, h, j): s_in[bidx(i, j), h] for (i, h, j) in units}
        tqs = {(i, h, j): _mm(wq_s[i * nb + j, h], ss[i, h, j]) for (i, h, j) in units}
        us = {(i, h, j): u_s[rows_of(i, j), cols_of(h)] - tqs[i, h, j][0:c] for (i, h, j) in units}
        upd = {(i, h, j): _mm_tn(kd_s[rows_of(i, j), cols_of(h)], us[i, h, j])
               for (i, h, j) in units}
        for (i, h, j) in units:
            s_ref[bidx(i, j), h] = (ss[i, h, j] * egl_s[i, h, j * c:j * c + 1, :] + upd[i, h, j])
        for i in groups:
            for h in range(DN_HEADS):
                if nb > 1:
                    u = jnp.concatenate([us[i, h, j] for j in range(nb)], axis=0)
                    qs = jnp.concatenate([tqs[i, h, j][c:2 * c] for j in range(nb)], axis=0)
                else:
                    u = us[i, h, 0]
                    qs = tqs[i, h, 0][c:2 * c]
                o_s[i * sc:(i + 1) * sc, cols_of(h)] = qs + _mm(aqk_s[i, h], u)

    wave = INTRA_WAVE
    for i0 in range(0, nsc, wave):
        intra(range(i0, min(i0 + wave, nsc)))
    if nb > 1:
        for i in range(nsc):
            inter([i])
    else:
        for w in range(groups_per_seq):
            inter([b * groups_per_seq + w for b in range(bt)])

    for h in range(DN_HEADS):
        sl = slice(h * hd, (h + 1) * hd)
        o = o_s[:, sl]
        o = (o * lax.rsqrt(jnp.mean(o * o, axis=-1, keepdims=True) + EPS) * dng_ref[...]
             * sz_ref[:, :, sl].reshape(rows, hd))
        y_ref[:, :, sl] = o.reshape(bt, lt, hd).astype(y_ref.dtype)


def _even_mix(q, k, v, sz, gb, dn_norm, s_prev, prev_out, layer, n_layers, bt, lt):
    batch, length, dn = q.shape
    c = CHUNK if length % CHUNK == 0 else length
    hd = DN_HEAD_DIM
    nl = length // lt
    rows = bt * lt
    assert rows % DN_ROWS == 0 and DN_ROWS % c == 0 and (c == DN_ROWS or lt == c)
    nsc = rows // DN_ROWS
    nb = DN_ROWS // c
    grid = (batch // bt, nl)
    has_state = s_prev is not None
    tok_map = lambda b, l: (b, l, 0)
    tok = lambda width: pl.BlockSpec((bt, lt, width), tok_map)
    in_specs = [tok(dn), tok(dn), tok(dn), tok(dn), tok(LANES), _resident((1, hd))]
    args = [q, k, v, sz, gb, dn_norm.reshape(1, hd)]
    state_block = pl.BlockSpec((None, bt, DN_HEADS, hd, hd), lambda b, l: (layer, b, 0, 0, 0))
    if has_state:
        in_specs.append(state_block)
        args.append(s_prev)
    aliases = {}
    if prev_out is not None:
        aliases[len(args)] = 1
        in_specs.append(pl.BlockSpec(memory_space=pl.ANY))
        args.append(prev_out)
    out_specs = [tok(dn), state_block]
    out_shape = [jax.ShapeDtypeStruct((batch, length, dn), _token_dtype(lt)),
                 jax.ShapeDtypeStruct((n_layers, batch, DN_HEADS, hd, hd), F32)]
    scratch = [pltpu.VMEM((rows, dn), F32),
               pltpu.VMEM((nsc * nb, DN_HEADS, 2 * c, hd), F32),
               pltpu.VMEM((rows, dn), F32), pltpu.VMEM((rows, dn), F32),
               pltpu.VMEM((nsc, DN_HEADS, DN_ROWS, DN_ROWS), F32),
               pltpu.VMEM((nsc, DN_HEADS, DN_ROWS, hd), F32)]
    return pl.pallas_call(
        functools.partial(_even_mix_kernel, bt=bt, lt=lt, c=c, has_state=has_state,
                          single_step=nl == 1, n_alias=len(aliases)),
        grid=grid, in_specs=in_specs, out_specs=out_specs, out_shape=out_shape,
        scratch_shapes=scratch, input_output_aliases=aliases,
        compiler_params=_compiler_params(("arbitrary", "arbitrary")),
        name="even_mix")(*args)


def _in_odd_kernel(*refs, starts, steps):
    ng = len(steps)
    g_ref, w_ref = refs[3 * ng:3 * ng + 2]
    outs = refs[3 * ng + 2:]
    dq = RET_HEADS * RET_QK_DIM
    dv = RET_HEADS * RET_V_DIM

    def body(x_ref, cos_ref, sin_ref, q_ref, k_ref, v_ref, sg_ref):
        h = _rmsnorm(x_ref[...], g_ref[...]).astype(BF16)

        def proj(lo, hi):
            return jnp.dot(h, w_ref[:, lo:hi], preferred_element_type=F32)

        cos2 = cos_ref[...]
        sin2 = sin_ref[...]

        def rotary(t, out_ref, scale):
            c2, s2 = (cos2, sin2) if scale is None else (cos2 * scale, sin2 * scale)
            for hh in range(RET_HEADS):
                sl = slice(hh * RET_QK_DIM, (hh + 1) * RET_QK_DIM)
                th = t[:, sl]
                r = th * c2 + pltpu.roll(th, RET_QK_DIM // 2, 1) * s2
                out_ref[:, sl] = r.astype(out_ref.dtype)

        q = proj(0, dq)
        k = proj(dq, 2 * dq)
        rotary(q, q_ref, None)
        v = proj(2 * dq, 2 * dq + dv)
        rotary(k, k_ref, RET_QK_DIM ** -0.5)
        gate = proj(2 * dq + dv, 2 * dq + 2 * dv)
        v_ref[...] = v.astype(v_ref.dtype)
        sg_ref[...] = _silu_of_twice(gate).astype(sg_ref.dtype)

    _for_each_group(starts, steps, [
        functools.partial(body, *refs[3 * g:3 * g + 3], *outs[4 * g:4 * g + 4]) for g in range(ng)])


def _in_odd(xs, g, w, layer, poss, seq_lens, tms):
    d = xs[0].shape[1]
    dq = RET_HEADS * RET_QK_DIM
    dv = RET_HEADS * RET_V_DIM
    half = RET_QK_DIM // 2
    inv = ROPE_BASE ** (-jnp.arange(half, dtype=F32) / half)
    steps = [x.shape[0] // tm for x, tm in zip(xs, tms)]
    starts = [sum(steps[:i]) for i in range(len(steps))]
    args, in_specs, out_specs, out_shape = [], [], [], []
    for x, pos, seq_len, tm, start, n in zip(xs, poss, seq_lens, tms, starts, steps):
        ang = pos.astype(F32)[:, None] * inv[None, :]
        cos = jnp.cos(ang)
        sin = jnp.sin(ang)
        cos2 = jnp.concatenate([cos, cos], axis=-1)
        sin2 = jnp.concatenate([-sin, sin], axis=-1)
        if seq_len < tm:
            cos2 = jnp.tile(cos2, (tm // seq_len, 1))
            sin2 = jnp.tile(sin2, (tm // seq_len, 1))
        n_tab = cos2.shape[0] // tm
        index, row = _group_rows(start, n, tm)
        tab = pl.BlockSpec((tm, RET_QK_DIM),
                           lambda i, index=index, n_tab=n_tab: (index(i) % n_tab, 0))
        args += [x, cos2, sin2]
        in_specs += [row(d), tab, tab]
        out_specs += [row(dq), row(dq), row(dv), row(dv)]
        out_shape += [jax.ShapeDtypeStruct((x.shape[0], width), BF16) for width in (dq, dq, dv, dv)]
    outs = pl.pallas_call(
        functools.partial(_in_odd_kernel, starts=tuple(starts), steps=tuple(steps)),
        grid=(sum(steps),), in_specs=in_specs + [_resident((1, d)), _layer_rows(w, layer)],
        out_specs=out_specs, out_shape=out_shape,
        compiler_params=_compiler_params(("arbitrary",)),
        name="in_odd")(*args, g.reshape(1, d), w)
    return [outs[4 * i:4 * i + 4] for i in range(len(xs))]


def _odd_mix_kernel(*refs, bt, lt, has_state, single_step, n_alias):
    n_in = 6 + (1 if has_state else 0)
    (q_ref, k_ref, v_ref, sg_ref, gng_ref, gnb_ref) = refs[:6]
    if has_state:
        r_prev_ref = refs[6]
    (y_ref, r_ref, o_s, dec_s, inn_s, kdec_s) = refs[n_in + n_alias:]
    c = lt
    l = pl.program_id(1)

    lgs = [math.log(1.0 - 2.0 ** (-5.0 - h)) for h in range(RET_HEADS)]

    @pl.when((pl.program_id(0) == 0) & (l == 0))
    def _():
        ri = lax.broadcasted_iota(jnp.int32, (c, c), 0)
        ci = lax.broadcasted_iota(jnp.int32, (c, c), 1)
        diff = (ri - ci).astype(F32)
        idx = lax.broadcasted_iota(jnp.int32, (c, RET_QK_DIM), 0).astype(F32)
        for h in range(RET_HEADS):
            dec_s[h] = jnp.where(diff >= 0, jnp.exp(jnp.maximum(diff, 0.0) * lgs[h]), 0.0)
            inn_s[h] = jnp.exp((idx + 1.0) * lgs[h])
            kdec_s[h] = jnp.exp((c - 1.0 - idx) * lgs[h])

    @pl.when(l == 0)
    def _():
        if not has_state:
            r_ref[...] = jnp.zeros(r_ref.shape, F32)
        elif not single_step:
            r_ref[...] = r_prev_ref[...]

    hg = 1 if c >= LANES else RET_HEADS
    r_in = r_prev_ref if (has_state and single_step) else r_ref
    def norm_gate(o, h):
        sl = slice(h * RET_V_DIM, (h + 1) * RET_V_DIM)
        mu = jnp.mean(o, axis=-1, keepdims=True)
        var = jnp.mean(jnp.square(o - mu), axis=-1, keepdims=True)
        o = (o - mu) * lax.rsqrt(var + GN_EPS) * gng_ref[:, sl] + gnb_ref[:, sl]
        y_ref[:, sl] = (sg_ref[:, sl].astype(F32) * o).astype(y_ref.dtype)

    def seq_rows(ref, b, lo, hi):
        if bt == 1:
            return ref[:, lo:hi]
        return ref[:, lo:hi].astype(F32)[b * c:(b + 1) * c]

    for b in range(bt):
        rs = slice(b * c, (b + 1) * c)
        for h0 in range(0, RET_HEADS, hg):
            heads = range(h0, h0 + hg)
            qs = {h: seq_rows(q_ref, b, h * RET_QK_DIM, (h + 1) * RET_QK_DIM) for h in heads}
            ks = {h: seq_rows(k_ref, b, h * RET_QK_DIM, (h + 1) * RET_QK_DIM) for h in heads}
            vs = {h: seq_rows(v_ref, b, h * RET_V_DIM, (h + 1) * RET_V_DIM) for h in heads}
            rr = {h: r_in[b, h] for h in heads}
            att = {h: _mm_nt(qs[h], ks[h]) * dec_s[h] for h in heads}
            qr = {h: _mm(qs[h].astype(F32) * inn_s[h], rr[h]) for h in heads}
            kv = {h: _mm_tn(ks[h].astype(F32) * kdec_s[h], vs[h]) for h in heads}
            for h in heads:
                r_ref[b, h] = rr[h] * math.exp(c * lgs[h]) + kv[h]
            for h in heads:
                o = _mm(att[h], vs[h]) + qr[h]
                if bt == 1:
                    norm_gate(o, h)
                else:
                    o_s[rs, h * RET_V_DIM:(h + 1) * RET_V_DIM] = o

    if bt > 1:
        for h in range(RET_HEADS):
            norm_gate(o_s[:, h * RET_V_DIM:(h + 1) * RET_V_DIM], h)


def _odd_mix(q, k, v, sg, gn_g, gn_b, r_prev, prev_out, layer, n_layers, batch, length, bt, lt):
    dq = RET_HEADS * RET_QK_DIM
    dv = RET_HEADS * RET_V_DIM
    nl = length // lt
    rows = bt * lt
    assert bt == 1 or nl == 1
    grid = (batch // bt, nl)
    has_state = r_prev is not None
    row = lambda width: pl.BlockSpec((rows, width), lambda b, l: (b * nl + l, 0))
    in_specs = [row(dq), row(dq), row(dv), row(dv), _resident((1, dv)), _resident((1, dv))]
    args = [q, k, v, sg, gn_g.reshape(1, dv), gn_b.reshape(1, dv)]
    state_block = pl.BlockSpec((None, bt, RET_HEADS, RET_QK_DIM, RET_V_DIM),
                               lambda b, l: (layer, b, 0, 0, 0))
    if has_state:
        in_specs.append(state_block)
        args.append(r_prev)
    aliases = {}
    if prev_out is not None:
        aliases[len(args)] = 1
        in_specs.append(pl.BlockSpec(memory_space=pl.ANY))
        args.append(prev_out)
    out_specs = [row(dv), state_block]
    out_shape = [jax.ShapeDtypeStruct((batch * length, dv), BF16),
                 jax.ShapeDtypeStruct((n_layers, batch, RET_HEADS, RET_QK_DIM, RET_V_DIM), F32)]
    scratch = [pltpu.VMEM((rows, dv), F32),
               pltpu.VMEM((RET_HEADS, lt, lt), F32),
               pltpu.VMEM((RET_HEADS, lt, RET_QK_DIM), F32),
               pltpu.VMEM((RET_HEADS, lt, RET_QK_DIM), F32)]
    return pl.pallas_call(
        functools.partial(_odd_mix_kernel, bt=bt, lt=lt, has_state=has_state,
                          single_step=nl == 1, n_alias=len(aliases)),
        grid=grid, in_specs=in_specs, out_specs=out_specs, out_shape=out_shape,
        scratch_shapes=scratch, input_output_aliases=aliases,
        compiler_params=_compiler_params(("arbitrary", "arbitrary")),
        name="odd_mix")(*args)


def _out_ffn_kernel(*refs, n_y, starts, steps, final):
    ng = len(steps)
    per = 1 + n_y
    rest = refs[ng * per:]
    wo_refs = rest[:n_y]
    g_ref, wg_ref, wu_ref, wd_ref, gf_ref = rest[n_y:n_y + 5]
    o_refs = rest[n_y + 5:]

    def body(x_ref, y_refs, o_ref):
        x1 = x_ref[...]
        for y_ref, wo_ref in zip(y_refs, wo_refs):
            x1 = x1 + jnp.dot(y_ref[...].astype(BF16), wo_ref[...], preferred_element_type=F32)
        h = _rmsnorm(x1, g_ref[...]).astype(BF16)
        gate = jnp.dot(h, wg_ref[...], preferred_element_type=F32)
        up = jnp.dot(h, wu_ref[...], preferred_element_type=F32)
        a = (_silu_of_twice(gate) * up).astype(BF16)
        x2 = x1 + jnp.dot(a, wd_ref[...], preferred_element_type=F32)
        if final:
            x2 = _rmsnorm(x2, gf_ref[...])
        o_ref[...] = x2

    _for_each_group(starts, steps, [
        functools.partial(body, refs[g * per], refs[g * per + 1:(g + 1) * per], o_refs[g])
        for g in range(ng)])


def _out_ffn(xs, yss, wo, wo_layer, g, wg, wu, wd, ffn_layer, gf, final, tms):
    d = xs[0].shape[1]
    n_y = len(yss[0])
    assert len({y.shape[1] for ys in yss for y in ys}) == 1
    steps = [x.shape[0] // tm for x, tm in zip(xs, tms)]
    starts = [sum(steps[:i]) for i in range(len(steps))]
    args, in_specs, out_specs, out_shape = [], [], [], []
    for x, ys, tm, start, n in zip(xs, yss, tms, starts, steps):
        _, row = _group_rows(start, n, tm)
        args += [x] + list(ys)
        in_specs += [row(d)] + [row(y.shape[1]) for y in ys]
        out_specs.append(row(d))
        out_shape.append(jax.ShapeDtypeStruct(x.shape, F32))
    in_specs += ([_layer_rows(wo, wo_layer, j, n_y) for j in range(n_y)]
                 + [_resident((1, d)), _layer_rows(wg, ffn_layer), _layer_rows(wu, ffn_layer),
                    _layer_rows(wd, ffn_layer), _resident((1, d))])
    return pl.pallas_call(
        functools.partial(_out_ffn_kernel, n_y=n_y, starts=tuple(starts), steps=tuple(steps),
                          final=final),
        grid=(sum(steps),), in_specs=in_specs, out_specs=out_specs, out_shape=out_shape,
        compiler_params=_compiler_params(("arbitrary",)),
        name="out_ffn")(*args, *([wo] * n_y), g.reshape(1, d), wg, wu, wd, gf.reshape(1, d))


def _tiles(batch, length):
    tm = min(batch * length, ROW_TILE)
    if length % CHUNK == 0:
        lt = min(length, 256)
        return dict(tm=tm, in_even=(1, min(length, tm)), even=(math.gcd(batch, 2), lt),
                    odd=(1, lt))
    return dict(tm=tm, in_even=(max(tm // length, 1), length), even=(min(batch, 16), length),
                odd=(min(batch, 8), length))


def _run_trunk(groups, w):
    depth = w["norm_mix"].shape[0]
    n_even = (depth + 1) // 2
    n_odd = depth // 2
    gs = []
    for x, states, pos in groups:
        batch, length, d = x.shape
        gs.append(dict(batch=batch, length=length, d=d, states=states, pos=pos,
                       t=_tiles(batch, length), x=x.reshape(batch * length, d),
                       conv_out=None,
                       delta_out=None, ret_out=None))
    tms = [g["t"]["tm"] for g in gs]
    for i in range(depth):
        if i % 2 == 0:
            e = i // 2
            for g in gs:
                st = g["states"]
                (ya, q, k, v, sz, gb), g["conv_out"] = _in_even(
                    g["x"].reshape(g["batch"], g["length"], g["d"]), w["norm_mix"][i],
                    w["even_w_main"], w["even_w_ab"], w["even_conv_a"][e], w["even_conv_qkv"][e],
                    w["even_a_log"][e], w["even_dt_bias"][e],
                    None if st is None else (st[0], st[1]), g["conv_out"], e, n_even,
                    *g["t"]["in_even"])
                yb, g["delta_out"] = _even_mix(q, k, v, sz, gb, w["even_dn_norm"][e],
                                               None if st is None else st[2], g["delta_out"],
                                               e, n_even, *g["t"]["even"])
                rows = g["batch"] * g["length"]
                g["ys"] = [ya.reshape(rows, -1), yb.reshape(rows, -1)]
            wo, wo_layer = w["even_w_out"], e
        else:
            o = i // 2
            qkvs = _in_odd([g["x"] for g in gs], w["norm_mix"][i], w["odd_w_in"], o,
                           [g["pos"] for g in gs], [g["length"] for g in gs], tms)
            for g, (q, k, v, sg) in zip(gs, qkvs):
                st = g["states"]
                y, g["ret_out"] = _odd_mix(q, k, v, sg, w["odd_gn_g"][o], w["odd_gn_b"][o],
                                           None if st is None else st[3], g["ret_out"], o, n_odd,
                                           g["batch"], g["length"], *g["t"]["odd"])
                g["ys"] = [y]
            wo, wo_layer = w["odd_w_out"], o
        xs = _out_ffn([g["x"] for g in gs], [g["ys"] for g in gs], wo, wo_layer, w["norm_ffn"][i],
                      w["ffn_w_gate"], w["ffn_w_up"], w["ffn_w_down"], i, w["final_norm"],
                      i == depth - 1, tms)
        for g, x in zip(gs, xs):
            g["x"] = x
    return [(g["x"].reshape(g["batch"], g["length"], g["d"]),) + g["conv_out"]
            + (g["delta_out"], g["ret_out"]) for g in gs]


def kernel(x_prompt, x_sample, state_conv_a, state_conv_qkv, state_delta, state_ret, norm_mix,
           norm_ffn, final_norm, even_w_in, even_conv_a, even_conv_qkv, even_a_log, even_dt_bias,
           even_dn_norm, even_w_out, odd_w_in, odd_gn_g, odd_gn_b, odd_w_out, ffn_w_gate, ffn_w_up,
           ffn_w_down):
    n_main = even_w_in.shape[-1] - 2 * DN_HEADS
    dn = DN_HEADS * DN_HEAD_DIM
    dv = RET_HEADS * RET_V_DIM
    even_cols = jnp.arange(even_w_in.shape[-1])
    even_half = jnp.where((even_cols >= n_main - dn) & (even_cols < n_main), 0.5, 1.0)
    odd_cols = jnp.arange(odd_w_in.shape[-1])
    odd_half = jnp.where(odd_cols >= odd_w_in.shape[-1] - dv, 0.5, 1.0)
    w = dict(norm_mix=norm_mix, norm_ffn=norm_ffn, final_norm=final_norm,
             even_w_main=(even_w_in * even_half).astype(BF16),
             even_w_ab=jnp.pad(even_w_in[:, :, n_main:],
                               ((0, 0), (0, 0), (0, LANES - 2 * DN_HEADS))).astype(BF16),
             even_conv_a=even_conv_a, even_conv_qkv=0.5 * even_conv_qkv, even_a_log=even_a_log,
             even_dt_bias=even_dt_bias, even_dn_norm=even_dn_norm,
             even_w_out=even_w_out.astype(BF16), odd_w_in=(odd_w_in * odd_half).astype(BF16),
             odd_gn_g=odd_gn_g, odd_gn_b=odd_gn_b, odd_w_out=odd_w_out.astype(BF16),
             ffn_w_gate=(0.5 * ffn_w_gate).astype(BF16), ffn_w_up=ffn_w_up.astype(BF16),
             ffn_w_down=ffn_w_down.astype(BF16))
    lp = x_prompt.shape[1]
    ls = x_sample.shape[1]
    pos_p = jnp.arange(lp, dtype=jnp.int32)
    pos_s = PAST_LEN + jnp.arange(ls, dtype=jnp.int32)
    (y_prompt, ca_p, cq_p, d_p, r_p), (y_sample, ca_s, cq_s, d_s, r_s) = _run_trunk(
        [(x_prompt, None, pos_p),
         (x_sample, (state_conv_a, state_conv_qkv, state_delta, state_ret), pos_s)], w)
    return (y_prompt, y_sample, ca_p, cq_p, d_p, r_p, ca_s, cq_s, d_s, r_s)
```

```python
import functools
import math

import jax
import jax.numpy as jnp
from jax import lax
from jax.experimental import pallas as pl
from jax.experimental.pallas import tpu as pltpu

F32 = jnp.float32
BF16 = jnp.bfloat16

EPS = 1e-6
GN_EPS = 1e-5
ROPE_BASE = 10000.0
PAST_LEN = 16384
CHUNK = 64

CONV_A_W = 3
DN_CONV_W = 4
DN_HEADS = 4
DN_HEAD_DIM = 128
RET_HEADS = 8
RET_QK_DIM = 128
RET_V_DIM = 256

LANES = 128
SUBLANES = 8
VMEM_LIMIT_BYTES = 56 * 1024 * 1024
DN_ROWS = 64
INTRA_WAVE = 8
ROW_TILE = 512


def _compiler_params(semantics):
    return pltpu.CompilerParams(dimension_semantics=semantics,
                                vmem_limit_bytes=VMEM_LIMIT_BYTES)


def _resident(shape):
    nd = len(shape)
    return pl.BlockSpec(shape, lambda *_: (0,) * nd, pipeline_mode=pl.Buffered(1))


def _layer_rows(stack, layer, part=0, parts=1):
    _, k, n = stack.shape
    return pl.BlockSpec((None, k // parts, n), lambda *_: (layer, part, 0),
                        pipeline_mode=pl.Buffered(1))


def _group_rows(start, steps, tm):
    def index(i):
        return jnp.minimum(jnp.maximum(i - start, 0), steps - 1)
    return index, (lambda width: pl.BlockSpec((tm, width), lambda i: (index(i), 0)))


def _for_each_group(starts, steps, bodies):
    i = pl.program_id(0)
    if len(bodies) == 1:
        bodies[0]()
        return
    for start, n, body in zip(starts, steps, bodies):
        pl.when((i >= start) & (i < start + n))(body)


def _mm(a, b):
    return jnp.dot(a.astype(BF16), b.astype(BF16), preferred_element_type=F32)


def _mm_nt(a, b):
    return lax.dot_general(a.astype(BF16), b.astype(BF16), (((1,), (1,)), ((), ())),
                           preferred_element_type=F32)


def _mm_tn(a, b):
    return lax.dot_general(a.astype(BF16), b.astype(BF16), (((0,), (0,)), ((), ())),
                           preferred_element_type=F32)


def _sigmoid(x):
    return 0.5 * jnp.tanh(0.5 * x) + 0.5


def _silu_of_twice(hx):
    return hx * jnp.tanh(hx) + hx


def _rmsnorm(x, g):
    return x * lax.rsqrt(jnp.mean(x * x, axis=-1, keepdims=True) + EPS) * g


def _token_dtype(lt):
    return BF16 if lt % (2 * SUBLANES) == 0 else F32


def _in_even_kernel(*refs, bt, lt, has_state, n_alias):
    n_in = 7 + (2 if has_state else 0)
    (x_ref, g_ref, wm_ref, caw_ref, cqw_ref, alog_ref, dt_ref) = refs[:7]
    if has_state:
        ca_prev_ref, cq_prev_ref = refs[7:9]
    (ya_ref, q_ref, k_ref, v_ref, sz_ref, gb_ref, ca_new_ref, cq_new_ref,
     ua_s, qkv_s) = refs[n_in + n_alias:]
    hd = DN_HEAD_DIM
    dn = DN_HEADS * hd
    ca = ua_s.shape[-1]
    d = x_ref.shape[-1]
    l = pl.program_id(1)
    nl = pl.num_programs(1)
    rows = bt * lt
    pad = SUBLANES

    @pl.when(l == 0)
    def _():
        ua_s[:, 0:pad, :] = jnp.zeros((bt, pad, ca), F32)
        qkv_s[:, 0:pad, :] = jnp.zeros((bt, pad, 3 * dn), F32)
        if has_state:
            ua_s[:, pad - (CONV_A_W - 1):pad, :] = ca_prev_ref[...]
            qkv_s[:, pad - (DN_CONV_W - 1):pad, :] = cq_prev_ref[...]

    @pl.when(l > 0)
    def _():
        ua_s[:, 0:pad, :] = ua_s[:, lt:lt + pad, :]
        qkv_s[:, 0:pad, :] = qkv_s[:, lt:lt + pad, :]

    h = _rmsnorm(x_ref[...].reshape(rows, d), g_ref[...]).astype(BF16)

    def proj(lo, hi):
        return jnp.dot(h, wm_ref[:, lo:hi], preferred_element_type=F32)

    qkv0 = 3 * ca

    def proj_qkv(part):
        cols = slice(part * dn, (part + 1) * dn)
        qkv_s[:, pad:pad + lt, cols] = proj(qkv0 + part * dn, qkv0 + (part + 1) * dn).reshape(
            bt, lt, dn)

    def delayed(scr, back, cols):
        if back == 0:
            return scr[:, pad:pad + lt, cols]
        x = scr[:, :, cols]
        width = x.shape[-1]
        n_grp = lt // SUBLANES
        rot = pltpu.roll(x.reshape(bt * (n_grp + 1), SUBLANES, width), back, 1)
        rot = rot.reshape(bt, n_grp + 1, SUBLANES, width)
        own = lax.broadcasted_iota(jnp.int32, (SUBLANES, width), 0) >= back
        return jnp.where(own, rot[:, 1:], rot[:, :n_grp]).reshape(bt, lt, width)

    def conv_silu(part):
        cols = slice(part * dn, (part + 1) * dn)
        conv = cqw_ref[0:1, cols] * delayed(qkv_s, DN_CONV_W - 1, cols)
        for i in range(1, DN_CONV_W):
            conv = conv + cqw_ref[i:i + 1, cols] * delayed(qkv_s, DN_CONV_W - 1 - i, cols)
        return _silu_of_twice(conv)

    def l2norm_to(t, out_ref, scale):
        for hh in range(DN_HEADS):
            sl = slice(hh * hd, (hh + 1) * hd)
            th = t[:, :, sl]
            out_ref[:, :, sl] = th * (lax.rsqrt(jnp.sum(th * th, axis=-1, keepdims=True) + EPS)
                                      * scale)

    proj_qkv(0)
    proj_qkv(1)
    gate_c = proj(ca, 2 * ca)
    l2norm_to(conv_silu(0), q_ref, hd ** -0.5)
    proj_qkv(2)
    h_a = proj(2 * ca, 3 * ca)
    l2norm_to(conv_silu(1), k_ref, 1.0)
    gate_b = proj(0, ca)
    z = proj(3 * ca + 3 * dn, 3 * ca + 4 * dn)
    v_ref[...] = conv_silu(2)

    ua_s[:, pad:pad + lt, :] = (gate_c * h_a).reshape(bt, lt, ca)
    conv = caw_ref[0:1, :] * delayed(ua_s, CONV_A_W - 1, slice(0, ca))
    for i in range(1, CONV_A_W):
        conv = conv + caw_ref[i:i + 1, :] * delayed(ua_s, CONV_A_W - 1 - i, slice(0, ca))
    ya_ref[...] = (gate_b.reshape(bt, lt, ca) * conv).astype(ya_ref.dtype)

    pab = proj(3 * ca + 4 * dn, 3 * ca + 4 * dn + LANES)
    sz_ref[...] = _silu_of_twice(0.5 * z).reshape(bt, lt, dn)
    sp = jnp.maximum(pab + dt_ref[...], 0.0) + jnp.log(1.0 + jnp.exp(-jnp.abs(pab + dt_ref[...])))
    col = lax.broadcasted_iota(jnp.int32, pab.shape, 1)
    gb = jnp.where(col < DN_HEADS, -jnp.exp(alog_ref[...]) * sp, _sigmoid(pab))
    gb_ref[...] = gb.reshape(bt, lt, LANES)

    @pl.when(l == nl - 1)
    def _():
        ca_new_ref[...] = ua_s[:, pad + lt - (CONV_A_W - 1):pad + lt, :]
        cq_new_ref[...] = qkv_s[:, pad + lt - (DN_CONV_W - 1):pad + lt, :]


def _in_even(x, g, w_main, conv_a_w, conv_qkv_w, a_log, dt_bias, states, prev_out,
             layer, n_layers, bt, lt):
    batch, length, d = x.shape
    ca = conv_a_w.shape[1]
    dn = DN_HEADS * DN_HEAD_DIM
    nl = length // lt
    grid = (batch // bt, nl)
    has_state = states is not None
    tok_map = lambda b, l: (b, l, 0)
    tok = lambda width: pl.BlockSpec((bt, lt, width), tok_map)
    alog_row = jnp.zeros((1, LANES), F32).at[0, :DN_HEADS].set(a_log)
    dt_row = jnp.zeros((1, LANES), F32).at[0, :DN_HEADS].set(dt_bias)
    in_specs = [tok(d), _resident((1, d)), _layer_rows(w_main, layer),
                _resident(conv_a_w.shape), _resident(conv_qkv_w.shape),
                _resident((1, LANES)), _resident((1, LANES))]
    args = [x, g.reshape(1, d), w_main, conv_a_w, conv_qkv_w, alog_row, dt_row]
    ca_block = pl.BlockSpec((None, bt, CONV_A_W - 1, ca), lambda b, l: (layer, b, 0, 0))
    cq_block = pl.BlockSpec((None, bt, DN_CONV_W - 1, 3 * dn), lambda b, l: (layer, b, 0, 0))
    if has_state:
        in_specs += [ca_block, cq_block]
        args += list(states)
    aliases = {}
    if prev_out is not None:
        for j, a in enumerate(prev_out):
            aliases[len(args)] = 6 + j
            in_specs.append(pl.BlockSpec(memory_space=pl.ANY))
            args.append(a)
    out_specs = [tok(ca), tok(dn), tok(dn), tok(dn), tok(dn), tok(LANES), ca_block, cq_block]
    tokshape = lambda width, dt: jax.ShapeDtypeStruct((batch, length, width), dt)
    out_shape = [tokshape(ca, _token_dtype(lt)), tokshape(dn, F32), tokshape(dn, F32),
                 tokshape(dn, F32), tokshape(dn, F32), tokshape(LANES, F32),
                 jax.ShapeDtypeStruct((n_layers, batch, CONV_A_W - 1, ca), F32),
                 jax.ShapeDtypeStruct((n_layers, batch, DN_CONV_W - 1, 3 * dn), F32)]
    scratch = [pltpu.VMEM((bt, lt + SUBLANES, ca), F32),
               pltpu.VMEM((bt, lt + SUBLANES, 3 * dn), F32)]
    outs = pl.pallas_call(
        functools.partial(_in_even_kernel, bt=bt, lt=lt, has_state=has_state,
                          n_alias=len(aliases)),
        grid=grid, in_specs=in_specs, out_specs=out_specs, out_shape=out_shape,
        scratch_shapes=scratch, input_output_aliases=aliases,
        compiler_params=_compiler_params(("arbitrary", "arbitrary")),
        name="in_even")(*args)
    return outs[:6], tuple(outs[6:])


def _even_mix_kernel(*refs, bt, lt, c, has_state, single_step, n_alias):
    n_in = 6 + (1 if has_state else 0)
    (q_ref, k_ref, v_ref, sz_ref, gb_ref, dng_ref) = refs[:6]
    if has_state:
        s_prev_ref = refs[6]
    (y_ref, s_ref, o_s, wq_s, u_s, kd_s, aqk_s, egl_s) = refs[n_in + n_alias:]
    hd = DN_HEAD_DIM
    l = pl.program_id(1)
    rows = bt * lt

    @pl.when(l == 0)
    def _():
        if not has_state:
            s_ref[...] = jnp.zeros(s_ref.shape, F32)
        elif not single_step:
            s_ref[...] = s_prev_ref[...]

    sc = DN_ROWS
    nb = sc // c
    nsc = rows // sc
    groups_per_seq = max(lt // sc, 1)
    shift = int(math.log2(c))
    ri = lax.broadcasted_iota(jnp.int32, (sc, sc), 0)
    ci = lax.broadcasted_iota(jnp.int32, (sc, sc), 1)
    rblk = lax.shift_right_logical(ri, shift)
    same = rblk == lax.shift_right_logical(ci, shift)
    upper = (ri <= ci) & same
    causal = (ri >= ci) & same
    strict = (ri > ci) & same
    eye = ri == ci
    last = ci == (lax.shift_left(rblk, shift) + (c - 1))
    eye_f = eye.astype(F32)
    n_sq = shift - 1

    def group(ref, i, lo, hi):
        if nb == 1:
            b, w = divmod(i, groups_per_seq)
            return ref[b, w * sc:(w + 1) * sc, lo:hi]
        return ref[i * nb:(i + 1) * nb, :, lo:hi].reshape(sc, hi - lo)

    def intra(groups):
        units = [(i, h) for i in groups for h in range(DN_HEADS)]
        xs, ts, rhs, qes = {}, {}, {}, {}
        for (i, h) in units:
            rs = slice(i * sc, (i + 1) * sc)
            sl = slice(h * hd, (h + 1) * hd)
            q = group(q_ref, i, h * hd, (h + 1) * hd)
            k = group(k_ref, i, h * hd, (h + 1) * hd)
            v = group(v_ref, i, h * hd, (h + 1) * hd)
            gb = group(gb_ref, i, 0, LANES)
            beta = gb[:, DN_HEADS + h:DN_HEADS + h + 1]
            g_col = gb[:, h:h + 1]
            g_row = jnp.sum(jnp.where(upper, g_col, 0.0), axis=0, keepdims=True)
            g_rows = jnp.broadcast_to(g_row, (sc, sc))
            g_cum = jnp.sum(jnp.where(eye, g_rows, 0.0), axis=1, keepdims=True)
            g_last = jnp.sum(jnp.where(last, g_rows, 0.0), axis=1, keepdims=True)
            decay = jnp.exp(jnp.where(causal, g_cum - g_rows, -jnp.inf))
            kb = k * beta
            kq = _mm_nt(jnp.concatenate([kb, q], axis=0), k)
            e_g = jnp.exp(g_cum)
            xs[i, h] = -(kq[0:sc] * jnp.where(strict, decay, 0.0))
            ts[i, h] = eye_f + xs[i, h]
            rhs[i, h] = jnp.concatenate([kb * e_g, v * beta], axis=1)
            qes[i, h] = q * e_g
            kd_s[rs, sl] = k * jnp.exp(g_last - g_cum)
            aqk_s[i, h] = kq[sc:2 * sc] * decay
            egl_s[i, h] = jnp.broadcast_to(jnp.exp(g_last), (sc, hd))
        for _ in range(n_sq):
            for u in units:
                xs[u] = _mm(xs[u], xs[u])
            for u in units:
                ts[u] = ts[u] + _mm(ts[u], xs[u])
        for (i, h) in units:
            rs = slice(i * sc, (i + 1) * sc)
            sl = slice(h * hd, (h + 1) * hd)
            wu = _mm(ts[i, h], rhs[i, h])
            w = wu[:, 0:hd]
            qe = qes[i, h]
            for j in range(nb):
                wq_s[i * nb + j, h, 0:c] = w[j * c:(j + 1) * c]
                wq_s[i * nb + j, h, c:2 * c] = qe[j * c:(j + 1) * c]
            u_s[rs, sl] = wu[:, hd:2 * hd]

    s_in = s_prev_ref if (has_state and single_step) else s_ref

    def inter(groups):
        units = [(i, h, j) for i in groups for h in range(DN_HEADS) for j in range(nb)]
        bidx = lambda i, j: (i * nb + j) if nb > 1 else i // groups_per_seq
        rows_of = lambda i, j: slice(i * sc + j * c, i * sc + (j + 1) * c)
        cols_of = lambda h: slice(h * hd, (h + 1) * hd)
        ss = {(i, h, j): s_in[bidx(i, j), h] for (i, h, j) in units}
        tqs = {(i, h, j): _mm(wq_s[i * nb + j, h], ss[i, h, j]) for (i, h, j) in units}
        us = {(i, h, j): u_s[rows_of(i, j), cols_of(h)] - tqs[i, h, j][0:c] for (i, h, j) in units}
        upd = {(i, h, j): _mm_tn(kd_s[rows_of(i, j), cols_of(h)], us[i, h, j])
               for (i, h, j) in units}
        for (i, h, j) in units:
            s_ref[bidx(i, j), h] = (ss[i, h, j] * egl_s[i, h, j * c:j * c + 1, :] + upd[i, h, j])
        for i in groups:
            for h in range(DN_HEADS):
                if nb > 1:
                    u = jnp.concatenate([us[i, h, j] for j in range(nb)], axis=0)
                    qs = jnp.concatenate([tqs[i, h, j][c:2 * c] for j in range(nb)], axis=0)
                else:
                    u = us[i, h, 0]
                    qs = tqs[i, h, 0][c:2 * c]
                o_s[i * sc:(i + 1) * sc, cols_of(h)] = qs + _mm(aqk_s[i, h], u)

    wave = INTRA_WAVE
    for i0 in range(0, nsc, wave):
        intra(range(i0, min(i0 + wave, nsc)))
    if nb > 1:
        for i in range(nsc):
            inter([i])
    else:
        for w in range(groups_per_seq):
            inter([b * groups_per_seq + w for b in range(bt)])

    for h in range(DN_HEADS):
        sl = slice(h * hd, (h + 1) * hd)
        o = o_s[:, sl]
        o = (o * lax.rsqrt(jnp.mean(o * o, axis=-1, keepdims=True) + EPS) * dng_ref[...]
             * sz_ref[:, :, sl].reshape(rows, hd))
        y_ref[:, :, sl] = o.reshape(bt, lt, hd).astype(y_ref.dtype)


def _even_mix(q, k, v, sz, gb, dn_norm, s_prev, prev_out, layer, n_layers, bt, lt):
    batch, length, dn = q.shape
    c = CHUNK if length % CHUNK == 0 else length
    hd = DN_HEAD_DIM
    nl = length // lt
    rows = bt * lt
    assert rows % DN_ROWS == 0 and DN_ROWS % c == 0 and (c == DN_ROWS or lt == c)
    nsc = rows // DN_ROWS
    nb = DN_ROWS // c
    grid = (batch // bt, nl)
    has_state = s_prev is not None
    tok_map = lambda b, l: (b, l, 0)
    tok = lambda width: pl.BlockSpec((bt, lt, width), tok_map)
    in_specs = [tok(dn), tok(dn), tok(dn), tok(dn), tok(LANES), _resident((1, hd))]
    args = [q, k, v, sz, gb, dn_norm.reshape(1, hd)]
    state_block = pl.BlockSpec((None, bt, DN_HEADS, hd, hd), lambda b, l: (layer, b, 0, 0, 0))
    if has_state:
        in_specs.append(state_block)
        args.append(s_prev)
    aliases = {}
    if prev_out is not None:
        aliases[len(args)] = 1
        in_specs.append(pl.BlockSpec(memory_space=pl.ANY))
        args.append(prev_out)
    out_specs = [tok(dn), state_block]
    out_shape = [jax.ShapeDtypeStruct((batch, length, dn), _token_dtype(lt)),
                 jax.ShapeDtypeStruct((n_layers, batch, DN_HEADS, hd, hd), F32)]
    scratch = [pltpu.VMEM((rows, dn), F32),
               pltpu.VMEM((nsc * nb, DN_HEADS, 2 * c, hd), F32),
               pltpu.VMEM((rows, dn), F32), pltpu.VMEM((rows, dn), F32),
               pltpu.VMEM((nsc, DN_HEADS, DN_ROWS, DN_ROWS), F32),
               pltpu.VMEM((nsc, DN_HEADS, DN_ROWS, hd), F32)]
    return pl.pallas_call(
        functools.partial(_even_mix_kernel, bt=bt, lt=lt, c=c, has_state=has_state,
                          single_step=nl == 1, n_alias=len(aliases)),
        grid=grid, in_specs=in_specs, out_specs=out_specs, out_shape=out_shape,
        scratch_shapes=scratch, input_output_aliases=aliases,
        compiler_params=_compiler_params(("arbitrary", "arbitrary")),
        name="even_mix")(*args)


def _in_odd_kernel(*refs, starts, steps):
    ng = len(steps)
    g_ref, w_ref = refs[3 * ng:3 * ng + 2]
    outs = refs[3 * ng + 2:]
    dq = RET_HEADS * RET_QK_DIM
    dv = RET_HEADS * RET_V_DIM

    def body(x_ref, cos_ref, sin_ref, q_ref, k_ref, v_ref, sg_ref):
        h = _rmsnorm(x_ref[...], g_ref[...]).astype(BF16)

        def proj(lo, hi):
            return jnp.dot(h, w_ref[:, lo:hi], preferred_element_type=F32)

        cos2 = cos_ref[...]
        sin2 = sin_ref[...]

        def rotary(t, out_ref, scale):
            c2, s2 = (cos2, sin2) if scale is None else (cos2 * scale, sin2 * scale)
            for hh in range(RET_HEADS):
                sl = slice(hh * RET_QK_DIM, (hh + 1) * RET_QK_DIM)
                th = t[:, sl]
                r = th * c2 + pltpu.roll(th, RET_QK_DIM // 2, 1) * s2
                out_ref[:, sl] = r.astype(out_ref.dtype)

        q = proj(0, dq)
        k = proj(dq, 2 * dq)
        rotary(q, q_ref, None)
        v = proj(2 * dq, 2 * dq + dv)
        rotary(k, k_ref, RET_QK_DIM ** -0.5)
        gate = proj(2 * dq + dv, 2 * dq + 2 * dv)
        v_ref[...] = v.astype(v_ref.dtype)
        sg_ref[...] = _silu_of_twice(gate).astype(sg_ref.dtype)

    _for_each_group(starts, steps, [
        functools.partial(body, *refs[3 * g:3 * g + 3], *outs[4 * g:4 * g + 4]) for g in range(ng)])


def _in_odd(xs, g, w, layer, poss, seq_lens, tms):
    d = xs[0].shape[1]
    dq = RET_HEADS * RET_QK_DIM
    dv = RET_HEADS * RET_V_DIM
    half = RET_QK_DIM // 2
    inv = ROPE_BASE ** (-jnp.arange(half, dtype=F32) / half)
    steps = [x.shape[0] // tm for x, tm in zip(xs, tms)]
    starts = [sum(steps[:i]) for i in range(len(steps))]
    args, in_specs, out_specs, out_shape = [], [], [], []
    for x, pos, seq_len, tm, start, n in zip(xs, poss, seq_lens, tms, starts, steps):
        ang = pos.astype(F32)[:, None] * inv[None, :]
        cos = jnp.cos(ang)
        sin = jnp.sin(ang)
        cos2 = jnp.concatenate([cos, cos], axis=-1)
        sin2 = jnp.concatenate([-sin, sin], axis=-1)
        if seq_len < tm:
            cos2 = jnp.tile(cos2, (tm // seq_len, 1))
            sin2 = jnp.tile(sin2, (tm // seq_len, 1))
        n_tab = cos2.shape[0] // tm
        index, row = _group_rows(start, n, tm)
        tab = pl.BlockSpec((tm, RET_QK_DIM),
                           lambda i, index=index, n_tab=n_tab: (index(i) % n_tab, 0))
        args += [x, cos2, sin2]
        in_specs += [row(d), tab, tab]
        out_specs += [row(dq), row(dq), row(dv), row(dv)]
        out_shape += [jax.ShapeDtypeStruct((x.shape[0], width), BF16) for width in (dq, dq, dv, dv)]
    outs = pl.pallas_call(
        functools.partial(_in_odd_kernel, starts=tuple(starts), steps=tuple(steps)),
        grid=(sum(steps),), in_specs=in_specs + [_resident((1, d)), _layer_rows(w, layer)],
        out_specs=out_specs, out_shape=out_shape,
        compiler_params=_compiler_params(("arbitrary",)),
        name="in_odd")(*args, g.reshape(1, d), w)
    return [outs[4 * i:4 * i + 4] for i in range(len(xs))]


def _odd_mix_kernel(*refs, bt, lt, has_state, single_step, n_alias):
    n_in = 6 + (1 if has_state else 0)
    (q_ref, k_ref, v_ref, sg_ref, gng_ref, gnb_ref) = refs[:6]
    if has_state:
        r_prev_ref = refs[6]
    (y_ref, r_ref, o_s, dec_s, inn_s, kdec_s) = refs[n_in + n_alias:]
    c = lt
    l = pl.program_id(1)

    lgs = [math.log(1.0 - 2.0 ** (-5.0 - h)) for h in range(RET_HEADS)]

    @pl.when((pl.program_id(0) == 0) & (l == 0))
    def _():
        ri = lax.broadcasted_iota(jnp.int32, (c, c), 0)
        ci = lax.broadcasted_iota(jnp.int32, (c, c), 1)
        diff = (ri - ci).astype(F32)
        idx = lax.broadcasted_iota(jnp.int32, (c, RET_QK_DIM), 0).astype(F32)
        for h in range(RET_HEADS):
            dec_s[h] = jnp.where(diff >= 0, jnp.exp(jnp.maximum(diff, 0.0) * lgs[h]), 0.0)
            inn_s[h] = jnp.exp((idx + 1.0) * lgs[h])
            kdec_s[h] = jnp.exp((c - 1.0 - idx) * lgs[h])

    @pl.when(l == 0)
    def _():
        if not has_state:
            r_ref[...] = jnp.zeros(r_ref.shape, F32)
        elif not single_step:
            r_ref[...] = r_prev_ref[...]

    hg = 1 if c >= LANES else RET_HEADS
    r_in = r_prev_ref if (has_state and single_step) else r_ref
    def norm_gate(o, h):
        sl = slice(h * RET_V_DIM, (h + 1) * RET_V_DIM)
        mu = jnp.mean(o, axis=-1, keepdims=True)
        var = jnp.mean(jnp.square(o - mu), axis=-1, keepdims=True)
        o = (o - mu) * lax.rsqrt(var + GN_EPS) * gng_ref[:, sl] + gnb_ref[:, sl]
        y_ref[:, sl] = (sg_ref[:, sl].astype(F32) * o).astype(y_ref.dtype)

    def seq_rows(ref, b, lo, hi):
        if bt == 1:
            return ref[:, lo:hi]
        return ref[:, lo:hi].astype(F32)[b * c:(b + 1) * c]

    for b in range(bt):
        rs = slice(b * c, (b + 1) * c)
        for h0 in range(0, RET_HEADS, hg):
            heads = range(h0, h0 + hg)
            qs = {h: seq_rows(q_ref, b, h * RET_QK_DIM, (h + 1) * RET_QK_DIM) for h in heads}
            ks = {h: seq_rows(k_ref, b, h * RET_QK_DIM, (h + 1) * RET_QK_DIM) for h in heads}
            vs = {h: seq_rows(v_ref, b, h * RET_V_DIM, (h + 1) * RET_V_DIM) for h in heads}
            rr = {h: r_in[b, h] for h in heads}
            att = {h: _mm_nt(qs[h], ks[h]) * dec_s[h] for h in heads}
            qr = {h: _mm(qs[h].astype(F32) * inn_s[h], rr[h]) for h in heads}
            kv = {h: _mm_tn(ks[h].astype(F32) * kdec_s[h], vs[h]) for h in heads}
            for h in heads:
                r_ref[b, h] = rr[h] * math.exp(c * lgs[h]) + kv[h]
            for h in heads:
                o = _mm(att[h], vs[h]) + qr[h]
                if bt == 1:
                    norm_gate(o, h)
                else:
                    o_s[rs, h * RET_V_DIM:(h + 1) * RET_V_DIM] = o

    if bt > 1:
        for h in range(RET_HEADS):
            norm_gate(o_s[:, h * RET_V_DIM:(h + 1) * RET_V_DIM], h)


def _odd_mix(q, k, v, sg, gn_g, gn_b, r_prev, prev_out, layer, n_layers, batch, length, bt, lt):
    dq = RET_HEADS * RET_QK_DIM
    dv = RET_HEADS * RET_V_DIM
    nl = length // lt
    rows = bt * lt
    assert bt == 1 or nl == 1
    grid = (batch // bt, nl)
    has_state = r_prev is not None
    row = lambda width: pl.BlockSpec((rows, width), lambda b, l: (b * nl + l, 0))
    in_specs = [row(dq), row(dq), row(dv), row(dv), _resident((1, dv)), _resident((1, dv))]
    args = [q, k, v, sg, gn_g.reshape(1, dv), gn_b.reshape(1, dv)]
    state_block = pl.BlockSpec((None, bt, RET_HEADS, RET_QK_DIM, RET_V_DIM),
                               lambda b, l: (layer, b, 0, 0, 0))
    if has_state:
        in_specs.append(state_block)
        args.append(r_prev)
    aliases = {}
    if prev_out is not None:
        aliases[len(args)] = 1
        in_specs.append(pl.BlockSpec(memory_space=pl.ANY))
        args.append(prev_out)
    out_specs = [row(dv), state_block]
    out_shape = [jax.ShapeDtypeStruct((batch * length, dv), BF16),
                 jax.ShapeDtypeStruct((n_layers, batch, RET_HEADS, RET_QK_DIM, RET_V_DIM), F32)]
    scratch = [pltpu.VMEM((rows, dv), F32),
               pltpu.VMEM((RET_HEADS, lt, lt), F32),
               pltpu.VMEM((RET_HEADS, lt, RET_QK_DIM), F32),
               pltpu.VMEM((RET_HEADS, lt, RET_QK_DIM), F32)]
    return pl.pallas_call(
        functools.partial(_odd_mix_kernel, bt=bt, lt=lt, has_state=has_state,
                          single_step=nl == 1, n_alias=len(aliases)),
        grid=grid, in_specs=in_specs, out_specs=out_specs, out_shape=out_shape,
        scratch_shapes=scratch, input_output_aliases=aliases,
        compiler_params=_compiler_params(("arbitrary", "arbitrary")),
        name="odd_mix")(*args)


def _out_ffn_kernel(*refs, n_y, starts, steps, final):
    ng = len(steps)
    per = 1 + n_y
    rest = refs[ng * per:]
    wo_refs = rest[:n_y]
    g_ref, wg_ref, wu_ref, wd_ref, gf_ref = rest[n_y:n_y + 5]
    o_refs = rest[n_y + 5:]

    def body(x_ref, y_refs, o_ref):
        x1 = x_ref[...]
        for y_ref, wo_ref in zip(y_refs, wo_refs):
            x1 = x1 + jnp.dot(y_ref[...].astype(BF16), wo_ref[...], preferred_element_type=F32)
        h = _rmsnorm(x1, g_ref[...]).astype(BF16)
        gate = jnp.dot(h, wg_ref[...], preferred_element_type=F32)
        up = jnp.dot(h, wu_ref[...], preferred_element_type=F32)
        a = (_silu_of_twice(gate) * up).astype(BF16)
        x2 = x1 + jnp.dot(a, wd_ref[...], preferred_element_type=F32)
        if final:
            x2 = _rmsnorm(x2, gf_ref[...])
        o_ref[...] = x2

    _for_each_group(starts, steps, [
        functools.partial(body, refs[g * per], refs[g * per + 1:(g + 1) * per], o_refs[g])
        for g in range(ng)])


def _out_ffn(xs, yss, wo, wo_layer, g, wg, wu, wd, ffn_layer, gf, final, tms):
    d = xs[0].shape[1]
    n_y = len(yss[0])
    assert len({y.shape[1] for ys in yss for y in ys}) == 1
    steps = [x.shape[0] // tm for x, tm in zip(xs, tms)]
    starts = [sum(steps[:i]) for i in range(len(steps))]
    args, in_specs, out_specs, out_shape = [], [], [], []
    for x, ys, tm, start, n in zip(xs, yss, tms, starts, steps):
        _, row = _group_rows(start, n, tm)
        args += [x] + list(ys)
        in_specs += [row(d)] + [row(y.shape[1]) for y in ys]
        out_specs.append(row(d))
        out_shape.append(jax.ShapeDtypeStruct(x.shape, F32))
    in_specs += ([_layer_rows(wo, wo_layer, j, n_y) for j in range(n_y)]
                 + [_resident((1, d)), _layer_rows(wg, ffn_layer), _layer_rows(wu, ffn_layer),
                    _layer_rows(wd, ffn_layer), _resident((1, d))])
    return pl.pallas_call(
        functools.partial(_out_ffn_kernel, n_y=n_y, starts=tuple(starts), steps=tuple(steps),
                          final=final),
        grid=(sum(steps),), in_specs=in_specs, out_specs=out_specs, out_shape=out_shape,
        compiler_params=_compiler_params(("arbitrary",)),
        name="out_ffn")(*args, *([wo] * n_y), g.reshape(1, d), wg, wu, wd, gf.reshape(1, d))


def _tiles(batch, length):
    tm = min(batch * length, ROW_TILE)
    if length % CHUNK == 0:
        lt = min(length, 256)
        return dict(tm=tm, in_even=(1, min(length, tm)), even=(math.gcd(batch, 2), lt),
                    odd=(1, lt))
    return dict(tm=tm, in_even=(max(tm // length, 1), length), even=(min(batch, 16), length),
                odd=(min(batch, 8), length))


def _run_trunk(groups, w):
    depth = w["norm_mix"].shape[0]
    n_even = (depth + 1) // 2
    n_odd = depth // 2
    gs = []
    for x, states, pos in groups:
        batch, length, d = x.shape
        gs.append(dict(batch=batch, length=length, d=d, states=states, pos=pos,
                       t=_tiles(batch, length), x=x.reshape(batch * length, d),
                       conv_out=None,
                       delta_out=None, ret_out=None))
    tms = [g["t"]["tm"] for g in gs]
    for i in range(depth):
        if i % 2 == 0:
            e = i // 2
            for g in gs:
                st = g["states"]
                (ya, q, k, v, sz, gb), g["conv_out"] = _in_even(
                    g["x"].reshape(g["batch"], g["length"], g["d"]), w["norm_mix"][i],
                    w["even_w_main"], w["even_conv_a"][e], w["even_conv_qkv"][e],
                    w["even_a_log"][e], w["even_dt_bias"][e],
                    None if st is None else (st[0], st[1]), g["conv_out"], e, n_even,
                    *g["t"]["in_even"])
                yb, g["delta_out"] = _even_mix(q, k, v, sz, gb, w["even_dn_norm"][e],
                                               None if st is None else st[2], g["delta_out"],
                                               e, n_even, *g["t"]["even"])
                rows = g["batch"] * g["length"]
                g["ys"] = [ya.reshape(rows, -1), yb.reshape(rows, -1)]
            wo, wo_layer = w["even_w_out"], e
        else:
            o = i // 2
            qkvs = _in_odd([g["x"] for g in gs], w["norm_mix"][i], w["odd_w_in"], o,
                           [g["pos"] for g in gs], [g["length"] for g in gs], tms)
            for g, (q, k, v, sg) in zip(gs, qkvs):
                st = g["states"]
                y, g["ret_out"] = _odd_mix(q, k, v, sg, w["odd_gn_g"][o], w["odd_gn_b"][o],
                                           None if st is None else st[3], g["ret_out"], o, n_odd,
                                           g["batch"], g["length"], *g["t"]["odd"])
                g["ys"] = [y]
            wo, wo_layer = w["odd_w_out"], o
        xs = _out_ffn([g["x"] for g in gs], [g["ys"] for g in gs], wo, wo_layer, w["norm_ffn"][i],
                      w["ffn_w_gate"], w["ffn_w_up"], w["ffn_w_down"], i, w["final_norm"],
                      i == depth - 1, tms)
        for g, x in zip(gs, xs):
            g["x"] = x
    return [(g["x"].reshape(g["batch"], g["length"], g["d"]),) + g["conv_out"]
            + (g["delta_out"], g["ret_out"]) for g in gs]


def kernel(x_prompt, x_sample, state_conv_a, state_conv_qkv, state_delta, state_ret, norm_mix,
           norm_ffn, final_norm, even_w_in, even_conv_a, even_conv_qkv, even_a_log, even_dt_bias,
           even_dn_norm, even_w_out, odd_w_in, odd_gn_g, odd_gn_b, odd_w_out, ffn_w_gate, ffn_w_up,
           ffn_w_down):
    dv = RET_HEADS * RET_V_DIM
    odd_cols = jnp.arange(odd_w_in.shape[-1])
    odd_half = jnp.where(odd_cols >= odd_w_in.shape[-1] - dv, 0.5, 1.0)
    w = dict(norm_mix=norm_mix, norm_ffn=norm_ffn, final_norm=final_norm,
             even_w_main=jnp.pad(even_w_in, ((0, 0), (0, 0), (0, -even_w_in.shape[-1] % LANES))
                                 ).astype(BF16),
             even_conv_a=even_conv_a, even_conv_qkv=0.5 * even_conv_qkv, even_a_log=even_a_log,
             even_dt_bias=even_dt_bias, even_dn_norm=even_dn_norm,
             even_w_out=even_w_out.astype(BF16), odd_w_in=(odd_w_in * odd_half).astype(BF16),
             odd_gn_g=odd_gn_g, odd_gn_b=odd_gn_b, odd_w_out=odd_w_out.astype(BF16),
             ffn_w_gate=(0.5 * ffn_w_gate).astype(BF16), ffn_w_up=ffn_w_up.astype(BF16),
             ffn_w_down=ffn_w_down.astype(BF16))
    lp = x_prompt.shape[1]
    ls = x_sample.shape[1]
    pos_p = jnp.arange(lp, dtype=jnp.int32)
    pos_s = PAST_LEN + jnp.arange(ls, dtype=jnp.int32)
    (y_prompt, ca_p, cq_p, d_p, r_p), (y_sample, ca_s, cq_s, d_s, r_s) = _run_trunk(
        [(x_prompt, None, pos_p),
         (x_sample, (state_conv_a, state_conv_qkv, state_delta, state_ret), pos_s)], w)
    return (y_prompt, y_sample, ca_p, cq_p, d_p, r_p, ca_s, cq_s, d_s, r_s)
```

```python
import functools
import math

import jax
import jax.numpy as jnp
from jax import lax
from jax.experimental import pallas as pl
from jax.experimental.pallas import tpu as pltpu

F32 = jnp.float32
BF16 = jnp.bfloat16

EPS = 1e-6
GN_EPS = 1e-5
ROPE_BASE = 10000.0
PAST_LEN = 16384
CHUNK = 64

CONV_A_W = 3
DN_CONV_W = 4
DN_HEADS = 4
DN_HEAD_DIM = 128
RET_HEADS = 8
RET_QK_DIM = 128
RET_V_DIM = 256

LANES = 128
SUBLANES = 8
VMEM_LIMIT_BYTES = 56 * 1024 * 1024
DN_ROWS = 64
INTRA_WAVE = 8
ROW_TILE = 512


def _compiler_params(semantics):
    return pltpu.CompilerParams(dimension_semantics=semantics,
                                vmem_limit_bytes=VMEM_LIMIT_BYTES)


def _resident(shape):
    nd = len(shape)
    return pl.BlockSpec(shape, lambda *_: (0,) * nd, pipeline_mode=pl.Buffered(1))


def _layer_rows(stack, layer, part=0, parts=1):
    _, k, n = stack.shape
    return pl.BlockSpec((None, k // parts, n), lambda *_: (layer, part, 0),
                        pipeline_mode=pl.Buffered(1))


def _group_rows(start, steps, tm):
    def index(i):
        return jnp.minimum(jnp.maximum(i - start, 0), steps - 1)
    return index, (lambda width: pl.BlockSpec((tm, width), lambda i: (index(i), 0)))


def _for_each_group(starts, steps, bodies):
    i = pl.program_id(0)
    if len(bodies) == 1:
        bodies[0]()
        return
    for start, n, body in zip(starts, steps, bodies):
        pl.when((i >= start) & (i < start + n))(body)


def _mm(a, b):
    return jnp.dot(a.astype(BF16), b.astype(BF16), preferred_element_type=F32)


def _mm_nt(a, b):
    return lax.dot_general(a.astype(BF16), b.astype(BF16), (((1,), (1,)), ((), ())),
                           preferred_element_type=F32)


def _mm_tn(a, b):
    return lax.dot_general(a.astype(BF16), b.astype(BF16), (((0,), (0,)), ((), ())),
                           preferred_element_type=F32)


def _sigmoid(x):
    return 0.5 * jnp.tanh(0.5 * x) + 0.5


def _silu_of_twice(hx):
    return hx * jnp.tanh(hx) + hx


def _rmsnorm(x, g):
    return x * lax.rsqrt(jnp.mean(x * x, axis=-1, keepdims=True) + EPS) * g


def _token_dtype(lt):
    return BF16 if lt % (2 * SUBLANES) == 0 else F32


def _in_even_kernel(*refs, bt, lt, has_state, n_alias):
    n_in = 8 + (2 if has_state else 0)
    (x_ref, g_ref, wm_ref, wab_ref, caw_ref, cqw_ref, alog_ref, dt_ref) = refs[:8]
    if has_state:
        ca_prev_ref, cq_prev_ref = refs[8:10]
    (ya_ref, q_ref, k_ref, v_ref, sz_ref, gb_ref, ca_new_ref, cq_new_ref,
     ua_s, qkv_s) = refs[n_in + n_alias:]
    hd = DN_HEAD_DIM
    dn = DN_HEADS * hd
    ca = ua_s.shape[-1]
    d = x_ref.shape[-1]
    l = pl.program_id(1)
    nl = pl.num_programs(1)
    rows = bt * lt
    pad = SUBLANES

    @pl.when(l == 0)
    def _():
        ua_s[:, 0:pad, :] = jnp.zeros((bt, pad, ca), F32)
        qkv_s[:, 0:pad, :] = jnp.zeros((bt, pad, 3 * dn), F32)
        if has_state:
            ua_s[:, pad - (CONV_A_W - 1):pad, :] = ca_prev_ref[...]
            qkv_s[:, pad - (DN_CONV_W - 1):pad, :] = cq_prev_ref[...]

    @pl.when(l > 0)
    def _():
        ua_s[:, 0:pad, :] = ua_s[:, lt:lt + pad, :]
        qkv_s[:, 0:pad, :] = qkv_s[:, lt:lt + pad, :]

    h = _rmsnorm(x_ref[...].reshape(rows, d), g_ref[...]).astype(BF16)

    def proj(lo, hi):
        return jnp.dot(h, wm_ref[:, lo:hi], preferred_element_type=F32)

    qkv0 = 3 * ca

    def proj_qkv(part):
        cols = slice(part * dn, (part + 1) * dn)
        qkv_s[:, pad:pad + lt, cols] = proj(qkv0 + part * dn, qkv0 + (part + 1) * dn).reshape(
            bt, lt, dn)

    def delayed(scr, back, cols):
        if back == 0:
            return scr[:, pad:pad + lt, cols]
        x = scr[:, :, cols]
        width = x.shape[-1]
        n_grp = lt // SUBLANES
        rot = pltpu.roll(x.reshape(bt * (n_grp + 1), SUBLANES, width), back, 1)
        rot = rot.reshape(bt, n_grp + 1, SUBLANES, width)
        own = lax.broadcasted_iota(jnp.int32, (SUBLANES, width), 0) >= back
        return jnp.where(own, rot[:, 1:], rot[:, :n_grp]).reshape(bt, lt, width)

    def conv_silu(part):
        cols = slice(part * dn, (part + 1) * dn)
        conv = cqw_ref[0:1, cols] * delayed(qkv_s, DN_CONV_W - 1, cols)
        for i in range(1, DN_CONV_W):
            conv = conv + cqw_ref[i:i + 1, cols] * delayed(qkv_s, DN_CONV_W - 1 - i, cols)
        return _silu_of_twice(conv)

    def l2norm_to(t, out_ref, scale):
        for hh in range(DN_HEADS):
            sl = slice(hh * hd, (hh + 1) * hd)
            th = t[:, :, sl]
            out_ref[:, :, sl] = th * (lax.rsqrt(jnp.sum(th * th, axis=-1, keepdims=True) + EPS)
                                      * scale)

    proj_qkv(0)
    proj_qkv(1)
    gate_c = proj(ca, 2 * ca)
    l2norm_to(conv_silu(0), q_ref, hd ** -0.5)
    proj_qkv(2)
    h_a = proj(2 * ca, 3 * ca)
    l2norm_to(conv_silu(1), k_ref, 1.0)
    gate_b = proj(0, ca)
    z = proj(3 * ca + 3 * dn, 3 * ca + 4 * dn)
    v_ref[...] = conv_silu(2)

    ua_s[:, pad:pad + lt, :] = (gate_c * h_a).reshape(bt, lt, ca)
    conv = caw_ref[0:1, :] * delayed(ua_s, CONV_A_W - 1, slice(0, ca))
    for i in range(1, CONV_A_W):
        conv = conv + caw_ref[i:i + 1, :] * delayed(ua_s, CONV_A_W - 1 - i, slice(0, ca))
    ya_ref[...] = (gate_b.reshape(bt, lt, ca) * conv).astype(ya_ref.dtype)

    pab = jnp.dot(h, wab_ref[...], preferred_element_type=F32)
    sz_ref[...] = _silu_of_twice(0.5 * z).reshape(bt, lt, dn)
    sp = jnp.maximum(pab + dt_ref[...], 0.0) + jnp.log(1.0 + jnp.exp(-jnp.abs(pab + dt_ref[...])))
    col = lax.broadcasted_iota(jnp.int32, pab.shape, 1)
    gb = jnp.where(col < DN_HEADS, -jnp.exp(alog_ref[...]) * sp, _sigmoid(pab))
    gb_ref[...] = gb.reshape(bt, lt, LANES)

    @pl.when(l == nl - 1)
    def _():
        ca_new_ref[...] = ua_s[:, pad + lt - (CONV_A_W - 1):pad + lt, :]
        cq_new_ref[...] = qkv_s[:, pad + lt - (DN_CONV_W - 1):pad + lt, :]


def _in_even(x, g, w_main, w_ab, conv_a_w, conv_qkv_w, a_log, dt_bias, states, prev_out,
             layer, n_layers, bt, lt):
    batch, length, d = x.shape
    ca = conv_a_w.shape[1]
    dn = DN_HEADS * DN_HEAD_DIM
    nl = length // lt
    grid = (batch // bt, nl)
    has_state = states is not None
    tok_map = lambda b, l: (b, l, 0)
    tok = lambda width: pl.BlockSpec((bt, lt, width), tok_map)
    alog_row = jnp.zeros((1, LANES), F32).at[0, :DN_HEADS].set(a_log)
    dt_row = jnp.zeros((1, LANES), F32).at[0, :DN_HEADS].set(dt_bias)
    in_specs = [tok(d), _resident((1, d)), _layer_rows(w_main, layer), _layer_rows(w_ab, layer),
                _resident(conv_a_w.shape), _resident(conv_qkv_w.shape),
                _resident((1, LANES)), _resident((1, LANES))]
    args = [x, g.reshape(1, d), w_main, w_ab, conv_a_w, conv_qkv_w, alog_row, dt_row]
    ca_block = pl.BlockSpec((None, bt, CONV_A_W - 1, ca), lambda b, l: (layer, b, 0, 0))
    cq_block = pl.BlockSpec((None, bt, DN_CONV_W - 1, 3 * dn), lambda b, l: (layer, b, 0, 0))
    if has_state:
        in_specs += [ca_block, cq_block]
        args += list(states)
    aliases = {}
    if prev_out is not None:
        for j, a in enumerate(prev_out):
            aliases[len(args)] = 6 + j
            in_specs.append(pl.BlockSpec(memory_space=pl.ANY))
            args.append(a)
    out_specs = [tok(ca), tok(dn), tok(dn), tok(dn), tok(dn), tok(LANES), ca_block, cq_block]
    tokshape = lambda width, dt: jax.ShapeDtypeStruct((batch, length, width), dt)
    out_shape = [tokshape(ca, _token_dtype(lt)), tokshape(dn, F32), tokshape(dn, F32),
                 tokshape(dn, F32), tokshape(dn, F32), tokshape(LANES, F32),
                 jax.ShapeDtypeStruct((n_layers, batch, CONV_A_W - 1, ca), F32),
                 jax.ShapeDtypeStruct((n_layers, batch, DN_CONV_W - 1, 3 * dn), F32)]
    scratch = [pltpu.VMEM((bt, lt + SUBLANES, ca), F32),
               pltpu.VMEM((bt, lt + SUBLANES, 3 * dn), F32)]
    outs = pl.pallas_call(
        functools.partial(_in_even_kernel, bt=bt, lt=lt, has_state=has_state,
                          n_alias=len(aliases)),
        grid=grid, in_specs=in_specs, out_specs=out_specs, out_shape=out_shape,
        scratch_shapes=scratch, input_output_aliases=aliases,
        compiler_params=_compiler_params(("arbitrary", "arbitrary")),
        name="in_even")(*args)
    return outs[:6], tuple(outs[6:])


def _even_mix_kernel(*refs, bt, lt, c, has_state, single_step, n_alias):
    n_in = 6 + (1 if has_state else 0)
    (q_ref, k_ref, v_ref, sz_ref, gb_ref, dng_ref) = refs[:6]
    if has_state:
        s_prev_ref = refs[6]
    (y_ref, s_ref, o_s, wq_s, u_s, kd_s, aqk_s, egl_s) = refs[n_in + n_alias:]
    hd = DN_HEAD_DIM
    l = pl.program_id(1)
    rows = bt * lt

    @pl.when(l == 0)
    def _():
        if not has_state:
            s_ref[...] = jnp.zeros(s_ref.shape, F32)
        elif not single_step:
            s_ref[...] = s_prev_ref[...]

    sc = DN_ROWS
    nb = sc // c
    nsc = rows // sc
    groups_per_seq = max(lt // sc, 1)
    shift = int(math.log2(c))
    ri = lax.broadcasted_iota(jnp.int32, (sc, sc), 0)
    ci = lax.broadcasted_iota(jnp.int32, (sc, sc), 1)
    rblk = lax.shift_right_logical(ri, shift)
    same = rblk == lax.shift_right_logical(ci, shift)
    upper = (ri <= ci) & same
    causal = (ri >= ci) & same
    strict = (ri > ci) & same
    eye = ri == ci
    last = ci == (lax.shift_left(rblk, shift) + (c - 1))
    eye_f = eye.astype(F32)
    n_sq = shift - 1

    def group(ref, i, lo, hi):
        if nb == 1:
            b, w = divmod(i, groups_per_seq)
            return ref[b, w * sc:(w + 1) * sc, lo:hi]
        return ref[i * nb:(i + 1) * nb, :, lo:hi].reshape(sc, hi - lo)

    def intra(groups):
        units = [(i, h) for i in groups for h in range(DN_HEADS)]
        xs, ts, rhs, qes = {}, {}, {}, {}
        for (i, h) in units:
            rs = slice(i * sc, (i + 1) * sc)
            sl = slice(h * hd, (h + 1) * hd)
            q = group(q_ref, i, h * hd, (h + 1) * hd)
            k = group(k_ref, i, h * hd, (h + 1) * hd)
            v = group(v_ref, i, h * hd, (h + 1) * hd)
            gb = group(gb_ref, i, 0, LANES)
            beta = gb[:, DN_HEADS + h:DN_HEADS + h + 1]
            g_col = gb[:, h:h + 1]
            g_row = jnp.sum(jnp.where(upper, g_col, 0.0), axis=0, keepdims=True)
            g_rows = jnp.broadcast_to(g_row, (sc, sc))
            g_cum = jnp.sum(jnp.where(eye, g_rows, 0.0), axis=1, keepdims=True)
            g_last = jnp.sum(jnp.where(last, g_rows, 0.0), axis=1, keepdims=True)
            decay = jnp.exp(jnp.where(causal, g_cum - g_rows, -jnp.inf))
            kb = k * beta
            kq = _mm_nt(jnp.concatenate([kb, q], axis=0), k)
            e_g = jnp.exp(g_cum)
            xs[i, h] = -(kq[0:sc] * jnp.where(strict, decay, 0.0))
            ts[i, h] = eye_f + xs[i, h]
            rhs[i, h] = jnp.concatenate([kb * e_g, v * beta], axis=1)
            qes[i, h] = q * e_g
            kd_s[rs, sl] = k * jnp.exp(g_last - g_cum)
            aqk_s[i, h] = kq[sc:2 * sc] * decay
            egl_s[i, h] = jnp.broadcast_to(jnp.exp(g_last), (sc, hd))
        for _ in range(n_sq):
            for u in units:
                xs[u] = _mm(xs[u], xs[u])
            for u in units:
                ts[u] = ts[u] + _mm(ts[u], xs[u])
        for (i, h) in units:
            rs = slice(i * sc, (i + 1) * sc)
            sl = slice(h * hd, (h + 1) * hd)
            wu = _mm(ts[i, h], rhs[i, h])
            w = wu[:, 0:hd]
            qe = qes[i, h]
            for j in range(nb):
                wq_s[i * nb + j, h, 0:c] = w[j * c:(j + 1) * c]
                wq_s[i * nb + j, h, c:2 * c] = qe[j * c:(j + 1) * c]
            u_s[rs, sl] = wu[:, hd:2 * hd]

    s_in = s_prev_ref if (has_state and single_step) else s_ref

    def inter(groups):
        units = [(i, h, j) for i in groups for h in range(DN_HEADS) for j in range(nb)]
        bidx = lambda i, j: (i * nb + j) if nb > 1 else i // groups_per_seq
        rows_of = lambda i, j: slice(i * sc + j * c, i * sc + (j + 1) * c)
        cols_of = lambda h: slice(h * hd, (h + 1) * hd)
        ss = {(i, h, j): s_in[bidx(i, j), h] for (i, h, j) in units}
        tqs = {(i, h, j): _mm(wq_s[i * nb + j, h], ss[i, h, j]) for (i, h, j) in units}
        us = {(i, h, j): u_s[rows_of(i, j), cols_of(h)] - tqs[i, h, j][0:c] for (i, h, j) in units}
        upd = {(i, h, j): _mm_tn(kd_s[rows_of(i, j), cols_of(h)], us[i, h, j])
               for (i, h, j) in units}
        for (i, h, j) in units:
            s_ref[bidx(i, j), h] = (ss[i, h, j] * egl_s[i, h, j * c:j * c + 1, :] + upd[i, h, j])
        for i in groups:
            for h in range(DN_HEADS):
                if nb > 1:
                    u = jnp.concatenate([us[i, h, j] for j in range(nb)], axis=0)
                    qs = jnp.concatenate([tqs[i, h, j][c:2 * c] for j in range(nb)], axis=0)
                else:
                    u = us[i, h, 0]
                    qs = tqs[i, h, 0][c:2 * c]
                o_s[i * sc:(i + 1) * sc, cols_of(h)] = qs + _mm(aqk_s[i, h], u)

    wave = INTRA_WAVE
    for i0 in range(0, nsc, wave):
        intra(range(i0, min(i0 + wave, nsc)))
    if nb > 1:
        for i in range(nsc):
            inter([i])
    else:
        for w in range(groups_per_seq):
            inter([b * groups_per_seq + w for b in range(bt)])

    for h in range(DN_HEADS):
        sl = slice(h * hd, (h + 1) * hd)
        o = o_s[:, sl]
        o = (o * lax.rsqrt(jnp.mean(o * o, axis=-1, keepdims=True) + EPS) * dng_ref[...]
             * sz_ref[:, :, sl].reshape(rows, hd))
        y_ref[:, :, sl] = o.reshape(bt, lt, hd).astype(y_ref.dtype)


def _even_mix(q, k, v, sz, gb, dn_norm, s_prev, prev_out, layer, n_layers, bt, lt):
    batch, length, dn = q.shape
    c = CHUNK if length % CHUNK == 0 else length
    hd = DN_HEAD_DIM
    nl = length // lt
    rows = bt * lt
    assert rows % DN_ROWS == 0 and DN_ROWS % c == 0 and (c == DN_ROWS or lt == c)
    nsc = rows // DN_ROWS
    nb = DN_ROWS // c
    grid = (batch // bt, nl)
    has_state = s_prev is not None
    tok_map = lambda b, l: (b, l, 0)
    tok = lambda width: pl.BlockSpec((bt, lt, width), tok_map)
    in_specs = [tok(dn), tok(dn), tok(dn), tok(dn), tok(LANES), _resident((1, hd))]
    args = [q, k, v, sz, gb, dn_norm.reshape(1, hd)]
    state_block = pl.BlockSpec((None, bt, DN_HEADS, hd, hd), lambda b, l: (layer, b, 0, 0, 0))
    if has_state:
        in_specs.append(state_block)
        args.append(s_prev)
    aliases = {}
    if prev_out is not None:
        aliases[len(args)] = 1
        in_specs.append(pl.BlockSpec(memory_space=pl.ANY))
        args.append(prev_out)
    out_specs = [tok(dn), state_block]
    out_shape = [jax.ShapeDtypeStruct((batch, length, dn), _token_dtype(lt)),
                 jax.ShapeDtypeStruct((n_layers, batch, DN_HEADS, hd, hd), F32)]
    scratch = [pltpu.VMEM((rows, dn), F32),
               pltpu.VMEM((nsc * nb, DN_HEADS, 2 * c, hd), F32),
               pltpu.VMEM((rows, dn), F32), pltpu.VMEM((rows, dn), F32),
               pltpu.VMEM((nsc, DN_HEADS, DN_ROWS, DN_ROWS), F32),
               pltpu.VMEM((nsc, DN_HEADS, DN_ROWS, hd), F32)]
    return pl.pallas_call(
        functools.partial(_even_mix_kernel, bt=bt, lt=lt, c=c, has_state=has_state,
                          single_step=nl == 1, n_alias=len(aliases)),
        grid=grid, in_specs=in_specs, out_specs=out_specs, out_shape=out_shape,
        scratch_shapes=scratch, input_output_aliases=aliases,
        compiler_params=_compiler_params(("arbitrary", "arbitrary")),
        name="even_mix")(*args)


def _in_odd_kernel(*refs, starts, steps):
    ng = len(steps)
    g_ref, w_ref = refs[3 * ng:3 * ng + 2]
    outs = refs[3 * ng + 2:]
    dq = RET_HEADS * RET_QK_DIM
    dv = RET_HEADS * RET_V_DIM

    def body(x_ref, cos_ref, sin_ref, q_ref, k_ref, v_ref, sg_ref):
        h = _rmsnorm(x_ref[...], g_ref[...]).astype(BF16)

        def proj(lo, hi):
            return jnp.dot(h, w_ref[:, lo:hi], preferred_element_type=F32)

        cos2 = cos_ref[...]
        sin2 = sin_ref[...]

        def rotary(t, out_ref, scale):
            c2, s2 = (cos2, sin2) if scale is None else (cos2 * scale, sin2 * scale)
            for hh in range(RET_HEADS):
                sl = slice(hh * RET_QK_DIM, (hh + 1) * RET_QK_DIM)
                th = t[:, sl]
                r = th * c2 + pltpu.roll(th, RET_QK_DIM // 2, 1) * s2
                out_ref[:, sl] = r.astype(out_ref.dtype)

        q = proj(0, dq)
        k = proj(dq, 2 * dq)
        rotary(q, q_ref, None)
        v = proj(2 * dq, 2 * dq + dv)
        rotary(k, k_ref, RET_QK_DIM ** -0.5)
        gate = proj(2 * dq + dv, 2 * dq + 2 * dv)
        v_ref[...] = v.astype(v_ref.dtype)
        sg_ref[...] = _silu_of_twice(gate).astype(sg_ref.dtype)

    _for_each_group(starts, steps, [
        functools.partial(body, *refs[3 * g:3 * g + 3], *outs[4 * g:4 * g + 4]) for g in range(ng)])


def _in_odd(xs, g, w, layer, poss, seq_lens, tms):
    d = xs[0].shape[1]
    dq = RET_HEADS * RET_QK_DIM
    dv = RET_HEADS * RET_V_DIM
    half = RET_QK_DIM // 2
    inv = ROPE_BASE ** (-jnp.arange(half, dtype=F32) / half)
    steps = [x.shape[0] // tm for x, tm in zip(xs, tms)]
    starts = [sum(steps[:i]) for i in range(len(steps))]
    args, in_specs, out_specs, out_shape = [], [], [], []
    for x, pos, seq_len, tm, start, n in zip(xs, poss, seq_lens, tms, starts, steps):
        ang = pos.astype(F32)[:, None] * inv[None, :]
        cos = jnp.cos(ang)
        sin = jnp.sin(ang)
        cos2 = jnp.concatenate([cos, cos], axis=-1)
        sin2 = jnp.concatenate([-sin, sin], axis=-1)
        if seq_len < tm:
            cos2 = jnp.tile(cos2, (tm // seq_len, 1))
            sin2 = jnp.tile(sin2, (tm // seq_len, 1))
        n_tab = cos2.shape[0] // tm
        index, row = _group_rows(start, n, tm)
        tab = pl.BlockSpec((tm, RET_QK_DIM),
                           lambda i, index=index, n_tab=n_tab: (index(i) % n_tab, 0))
        args += [x, cos2, sin2]
        in_specs += [row(d), tab, tab]
        out_specs += [row(dq), row(dq), row(dv), row(dv)]
        out_shape += [jax.ShapeDtypeStruct((x.shape[0], width), BF16) for width in (dq, dq, dv, dv)]
    outs = pl.pallas_call(
        functools.partial(_in_odd_kernel, starts=tuple(starts), steps=tuple(steps)),
        grid=(sum(steps),), in_specs=in_specs + [_resident((1, d)), _layer_rows(w, layer)],
        out_specs=out_specs, out_shape=out_shape,
        compiler_params=_compiler_params(("arbitrary",)),
        name="in_odd")(*args, g.reshape(1, d), w)
    return [outs[4 * i:4 * i + 4] for i in range(len(xs))]


def _odd_mix_kernel(*refs, bt, lt, has_state, single_step, n_alias):
    n_in = 6 + (1 if has_state else 0)
    (q_ref, k_ref, v_ref, sg_ref, gng_ref, gnb_ref) = refs[:6]
    if has_state:
        r_prev_ref = refs[6]
    (y_ref, r_ref, o_s, dec_s, inn_s, kdec_s) = refs[n_in + n_alias:]
    c = lt
    l = pl.program_id(1)

    lgs = [math.log(1.0 - 2.0 ** (-5.0 - h)) for h in range(RET_HEADS)]

    @pl.when((pl.program_id(0) == 0) & (l == 0))
    def _():
        ri = lax.broadcasted_iota(jnp.int32, (c, c), 0)
        ci = lax.broadcasted_iota(jnp.int32, (c, c), 1)
        diff = (ri - ci).astype(F32)
        idx = lax.broadcasted_iota(jnp.int32, (c, RET_QK_DIM), 0).astype(F32)
        for h in range(RET_HEADS):
            dec_s[h] = jnp.where(diff >= 0, jnp.exp(jnp.maximum(diff, 0.0) * lgs[h]), 0.0)
            inn_s[h] = jnp.exp((idx + 1.0) * lgs[h])
            kdec_s[h] = jnp.exp((c - 1.0 - idx) * lgs[h])

    @pl.when(l == 0)
    def _():
        if not has_state:
            r_ref[...] = jnp.zeros(r_ref.shape, F32)
        elif not single_step:
            r_ref[...] = r_prev_ref[...]

    hg = 1 if c >= LANES else RET_HEADS
    r_in = r_prev_ref if (has_state and single_step) else r_ref
    def norm_gate(o, h):
        sl = slice(h * RET_V_DIM, (h + 1) * RET_V_DIM)
        mu = jnp.mean(o, axis=-1, keepdims=True)
        var = jnp.mean(jnp.square(o - mu), axis=-1, keepdims=True)
        o = (o - mu) * lax.rsqrt(var + GN_EPS) * gng_ref[:, sl] + gnb_ref[:, sl]
        y_ref[:, sl] = (sg_ref[:, sl].astype(F32) * o).astype(y_ref.dtype)

    def seq_rows(ref, b, lo, hi):
        if bt == 1:
            return ref[:, lo:hi]
        return ref[:, lo:hi].astype(F32)[b * c:(b + 1) * c]

    for b in range(bt):
        rs = slice(b * c, (b + 1) * c)
        for h0 in range(0, RET_HEADS, hg):
            heads = range(h0, h0 + hg)
            qs = {h: seq_rows(q_ref, b, h * RET_QK_DIM, (h + 1) * RET_QK_DIM) for h in heads}
            ks = {h: seq_rows(k_ref, b, h * RET_QK_DIM, (h + 1) * RET_QK_DIM) for h in heads}
            vs = {h: seq_rows(v_ref, b, h * RET_V_DIM, (h + 1) * RET_V_DIM) for h in heads}
            rr = {h: r_in[b, h] for h in heads}
            att = {h: _mm_nt(qs[h], ks[h]) * dec_s[h] for h in heads}
            qr = {h: _mm(qs[h].astype(F32) * inn_s[h], rr[h]) for h in heads}
            kv = {h: _mm_tn(ks[h].astype(F32) * kdec_s[h], vs[h]) for h in heads}
            for h in heads:
                r_ref[b, h] = rr[h] * math.exp(c * lgs[h]) + kv[h]
            for h in heads:
                o = _mm(att[h], vs[h]) + qr[h]
                if bt == 1:
                    norm_gate(o, h)
                else:
                    o_s[rs, h * RET_V_DIM:(h + 1) * RET_V_DIM] = o

    if bt > 1:
        for h in range(RET_HEADS):
            norm_gate(o_s[:, h * RET_V_DIM:(h + 1) * RET_V_DIM], h)


def _odd_mix(q, k, v, sg, gn_g, gn_b, r_prev, prev_out, layer, n_layers, batch, length, bt, lt):
    dq = RET_HEADS * RET_QK_DIM
    dv = RET_HEADS * RET_V_DIM
    nl = length // lt
    rows = bt * lt
    assert bt == 1 or nl == 1
    grid = (batch // bt, nl)
    has_state = r_prev is not None
    row = lambda width: pl.BlockSpec((rows, width), lambda b, l: (b * nl + l, 0))
    in_specs = [row(dq), row(dq), row(dv), row(dv), _resident((1, dv)), _resident((1, dv))]
    args = [q, k, v, sg, gn_g.reshape(1, dv), gn_b.reshape(1, dv)]
    state_block = pl.BlockSpec((None, bt, RET_HEADS, RET_QK_DIM, RET_V_DIM),
                               lambda b, l: (layer, b, 0, 0, 0))
    if has_state:
        in_specs.append(state_block)
        args.append(r_prev)
    aliases = {}
    if prev_out is not None:
        aliases[len(args)] = 1
        in_specs.append(pl.BlockSpec(memory_space=pl.ANY))
        args.append(prev_out)
    out_specs = [row(dv), state_block]
    out_shape = [jax.ShapeDtypeStruct((batch * length, dv), BF16),
                 jax.ShapeDtypeStruct((n_layers, batch, RET_HEADS, RET_QK_DIM, RET_V_DIM), F32)]
    scratch = [pltpu.VMEM((rows, dv), F32),
               pltpu.VMEM((RET_HEADS, lt, lt), F32),
               pltpu.VMEM((RET_HEADS, lt, RET_QK_DIM), F32),
               pltpu.VMEM((RET_HEADS, lt, RET_QK_DIM), F32)]
    return pl.pallas_call(
        functools.partial(_odd_mix_kernel, bt=bt, lt=lt, has_state=has_state,
                          single_step=nl == 1, n_alias=len(aliases)),
        grid=grid, in_specs=in_specs, out_specs=out_specs, out_shape=out_shape,
        scratch_shapes=scratch, input_output_aliases=aliases,
        compiler_params=_compiler_params(("arbitrary", "arbitrary")),
        name="odd_mix")(*args)


def _out_ffn_kernel(*refs, n_y, starts, steps, final):
    ng = len(steps)
    per = 1 + n_y
    rest = refs[ng * per:]
    wo_refs = rest[:n_y]
    g_ref, wg_ref, wu_ref, wd_ref, gf_ref = rest[n_y:n_y + 5]
    o_refs = rest[n_y + 5:]

    def body(x_ref, y_refs, o_ref):
        x1 = x_ref[...]
        for y_ref, wo_ref in zip(y_refs, wo_refs):
            x1 = x1 + jnp.dot(y_ref[...].astype(BF16), wo_ref[...], preferred_element_type=F32)
        h = _rmsnorm(x1, g_ref[...]).astype(BF16)
        gate = jnp.dot(h, wg_ref[...], preferred_element_type=F32)
        up = jnp.dot(h, wu_ref[...], preferred_element_type=F32)
        a = (_silu_of_twice(gate) * up).astype(BF16)
        x2 = x1 + jnp.dot(a, wd_ref[...], preferred_element_type=F32)
        if final:
            x2 = _rmsnorm(x2, gf_ref[...])
        o_ref[...] = x2

    _for_each_group(starts, steps, [
        functools.partial(body, refs[g * per], refs[g * per + 1:(g + 1) * per], o_refs[g])
        for g in range(ng)])


def _out_ffn(xs, yss, wo, wo_layer, g, wg, wu, wd, ffn_layer, gf, final, tms):
    d = xs[0].shape[1]
    n_y = len(yss[0])
    assert len({y.shape[1] for ys in yss for y in ys}) == 1
    steps = [x.shape[0] // tm for x, tm in zip(xs, tms)]
    starts = [sum(steps[:i]) for i in range(len(steps))]
    args, in_specs, out_specs, out_shape = [], [], [], []
    for x, ys, tm, start, n in zip(xs, yss, tms, starts, steps):
        _, row = _group_rows(start, n, tm)
        args += [x] + list(ys)
        in_specs += [row(d)] + [row(y.shape[1]) for y in ys]
        out_specs.append(row(d))
        out_shape.append(jax.ShapeDtypeStruct(x.shape, F32))
    in_specs += ([_layer_rows(wo, wo_layer, j, n_y) for j in range(n_y)]
                 + [_resident((1, d)), _layer_rows(wg, ffn_layer), _layer_rows(wu, ffn_layer),
                    _layer_rows(wd, ffn_layer), _resident((1, d))])
    return pl.pallas_call(
        functools.partial(_out_ffn_kernel, n_y=n_y, starts=tuple(starts), steps=tuple(steps),
                          final=final),
        grid=(sum(steps),), in_specs=in_specs, out_specs=out_specs, out_shape=out_shape,
        compiler_params=_compiler_params(("arbitrary",)),
        name="out_ffn")(*args, *([wo] * n_y), g.reshape(1, d), wg, wu, wd, gf.reshape(1, d))


def _tiles(batch, length):
    tm = min(batch * length, ROW_TILE)
    if length % CHUNK == 0:
        lt = min(length, 256)
        return dict(tm=tm, in_even=(1, min(length, tm)), even=(math.gcd(batch, 2), lt),
                    odd=(1, lt))
    return dict(tm=tm, in_even=(max(tm // length, 1), length), even=(min(batch, 16), length),
                odd=(min(batch, 8), length))


def _run_trunk(groups, w):
    depth = w["norm_mix"].shape[0]
    n_even = (depth + 1) // 2
    n_odd = depth // 2
    gs = []
    for x, states, pos in groups:
        batch, length, d = x.shape
        gs.append(dict(batch=batch, length=length, d=d, states=states, pos=pos,
                       t=_tiles(batch, length), x=x.reshape(batch * length, d),
                       conv_out=None,
                       delta_out=None, ret_out=None))
    tms = [g["t"]["tm"] for g in gs]
    for i in range(depth):
        if i % 2 == 0:
            e = i // 2
            for g in gs:
                st = g["states"]
                (ya, q, k, v, sz, gb), g["conv_out"] = _in_even(
                    g["x"].reshape(g["batch"], g["length"], g["d"]), w["norm_mix"][i],
                    w["even_w_main"], w["even_w_ab"], w["even_conv_a"][e], w["even_conv_qkv"][e],
                    w["even_a_log"][e], w["even_dt_bias"][e],
                    None if st is None else (st[0], st[1]), g["conv_out"], e, n_even,
                    *g["t"]["in_even"])
                yb, g["delta_out"] = _even_mix(q, k, v, sz, gb, w["even_dn_norm"][e],
                                               None if st is None else st[2], g["delta_out"],
                                               e, n_even, *g["t"]["even"])
                rows = g["batch"] * g["length"]
                g["ys"] = [ya.reshape(rows, -1), yb.reshape(rows, -1)]
            wo, wo_layer = w["even_w_out"], e
        else:
            o = i // 2
            qkvs = _in_odd([g["x"] for g in gs], w["norm_mix"][i], w["odd_w_in"], o,
                           [g["pos"] for g in gs], [g["length"] for g in gs], tms)
            for g, (q, k, v, sg) in zip(gs, qkvs):
                st = g["states"]
                y, g["ret_out"] = _odd_mix(q, k, v, sg, w["odd_gn_g"][o], w["odd_gn_b"][o],
                                           None if st is None else st[3], g["ret_out"], o, n_odd,
                                           g["batch"], g["length"], *g["t"]["odd"])
                g["ys"] = [y]
            wo, wo_layer = w["odd_w_out"], o
        xs = _out_ffn([g["x"] for g in gs], [g["ys"] for g in gs], wo, wo_layer, w["norm_ffn"][i],
                      w["ffn_w_gate"], w["ffn_w_up"], w["ffn_w_down"], i, w["final_norm"],
                      i == depth - 1, tms)
        for g, x in zip(gs, xs):
            g["x"] = x
    return [(g["x"].reshape(g["batch"], g["length"], g["d"]),) + g["conv_out"]
            + (g["delta_out"], g["ret_out"]) for g in gs]


def kernel(x_prompt, x_sample, state_conv_a, state_conv_qkv, state_delta, state_ret, norm_mix,
           norm_ffn, final_norm, even_w_in, even_conv_a, even_conv_qkv, even_a_log, even_dt_bias,
           even_dn_norm, even_w_out, odd_w_in, odd_gn_g, odd_gn_b, odd_w_out, ffn_w_gate, ffn_w_up,
           ffn_w_down):
    n_main = even_w_in.shape[-1] - 2 * DN_HEADS
    dv = RET_HEADS * RET_V_DIM
    odd_cols = jnp.arange(odd_w_in.shape[-1])
    odd_half = jnp.where(odd_cols >= odd_w_in.shape[-1] - dv, 0.5, 1.0)
    w = dict(norm_mix=norm_mix, norm_ffn=norm_ffn, final_norm=final_norm,
             even_w_main=even_w_in.astype(BF16),
             even_w_ab=jnp.pad(even_w_in[:, :, n_main:],
                               ((0, 0), (0, 0), (0, LANES - 2 * DN_HEADS))).astype(BF16),
             even_conv_a=even_conv_a, even_conv_qkv=0.5 * even_conv_qkv, even_a_log=even_a_log,
             even_dt_bias=even_dt_bias, even_dn_norm=even_dn_norm,
             even_w_out=even_w_out.astype(BF16), odd_w_in=(odd_w_in * odd_half).astype(BF16),
             odd_gn_g=odd_gn_g, odd_gn_b=odd_gn_b, odd_w_out=odd_w_out.astype(BF16),
             ffn_w_gate=(0.5 * ffn_w_gate).astype(BF16), ffn_w_up=ffn_w_up.astype(BF16),
             ffn_w_down=ffn_w_down.astype(BF16))
    lp = x_prompt.shape[1]
    ls = x_sample.shape[1]
    pos_p = jnp.arange(lp, dtype=jnp.int32)
    pos_s = PAST_LEN + jnp.arange(ls, dtype=jnp.int32)
    (y_prompt, ca_p, cq_p, d_p, r_p), (y_sample, ca_s, cq_s, d_s, r_s) = _run_trunk(
        [(x_prompt, None, pos_p),
         (x_sample, (state_conv_a, state_conv_qkv, state_delta, state_ret), pos_s)], w)
    return (y_prompt, y_sample, ca_p, cq_p, d_p, r_p, ca_s, cq_s, d_s, r_s)
```

```python
import functools
import math

import jax
import jax.numpy as jnp
from jax import lax
from jax.experimental import pallas as pl
from jax.experimental.pallas import tpu as pltpu

F32 = jnp.float32
BF16 = jnp.bfloat16

EPS = 1e-6
GN_EPS = 1e-5
ROPE_BASE = 10000.0
PAST_LEN = 16384
CHUNK = 64

CONV_A_W = 3
DN_CONV_W = 4
DN_HEADS = 4
DN_HEAD_DIM = 128
RET_HEADS = 8
RET_QK_DIM = 128
RET_V_DIM = 256

LANES = 128
SUBLANES = 8
VMEM_LIMIT_BYTES = 56 * 1024 * 1024
DN_ROWS = 64
INTRA_WAVE = 8
ROW_TILE = 512


def _compiler_params(semantics):
    return pltpu.CompilerParams(dimension_semantics=semantics,
                                vmem_limit_bytes=VMEM_LIMIT_BYTES)


def _resident(shape):
    nd = len(shape)
    return pl.BlockSpec(shape, lambda *_: (0,) * nd, pipeline_mode=pl.Buffered(1))


def _layer_rows(stack, layer, part=0, parts=1):
    _, k, n = stack.shape
    return pl.BlockSpec((None, k // parts, n), lambda *_: (layer, part, 0),
                        pipeline_mode=pl.Buffered(1))


def _group_rows(start, steps, tm):
    def index(i):
        return jnp.minimum(jnp.maximum(i - start, 0), steps - 1)
    return index, (lambda width: pl.BlockSpec((tm, width), lambda i: (index(i), 0)))


def _for_each_group(starts, steps, bodies):
    i = pl.program_id(0)
    if len(bodies) == 1:
        bodies[0]()
        return
    for start, n, body in zip(starts, steps, bodies):
        pl.when((i >= start) & (i < start + n))(body)


def _mm(a, b):
    return jnp.dot(a.astype(BF16), b.astype(BF16), preferred_element_type=F32)


def _mm_nt(a, b):
    return lax.dot_general(a.astype(BF16), b.astype(BF16), (((1,), (1,)), ((), ())),
                           preferred_element_type=F32)


def _mm_tn(a, b):
    return lax.dot_general(a.astype(BF16), b.astype(BF16), (((0,), (0,)), ((), ())),
                           preferred_element_type=F32)


def _sigmoid(x):
    return 0.5 * jnp.tanh(0.5 * x) + 0.5


def _silu_of_twice(hx):
    return hx * jnp.tanh(hx) + hx


def _rmsnorm(x, g):
    return x * lax.rsqrt(jnp.mean(x * x, axis=-1, keepdims=True) + EPS) * g


def _token_dtype(lt):
    return BF16 if lt % (2 * SUBLANES) == 0 else F32


def _in_even_kernel(*refs, bt, lt, has_state, n_alias):
    n_in = 8 + (2 if has_state else 0)
    (x_ref, g_ref, wm_ref, wab_ref, caw_ref, cqw_ref, alog_ref, dt_ref) = refs[:8]
    if has_state:
        ca_prev_ref, cq_prev_ref = refs[8:10]
    (ya_ref, q_ref, k_ref, v_ref, sz_ref, gb_ref, ca_new_ref, cq_new_ref,
     ua_s, qkv_s) = refs[n_in + n_alias:]
    hd = DN_HEAD_DIM
    dn = DN_HEADS * hd
    ca = ua_s.shape[-1]
    d = x_ref.shape[-1]
    l = pl.program_id(1)
    nl = pl.num_programs(1)
    rows = bt * lt
    pad = SUBLANES

    @pl.when(l == 0)
    def _():
        ua_s[:, 0:pad, :] = jnp.zeros((bt, pad, ca), F32)
        qkv_s[:, 0:pad, :] = jnp.zeros((bt, pad, 3 * dn), F32)
        if has_state:
            ua_s[:, pad - (CONV_A_W - 1):pad, :] = ca_prev_ref[...]
            qkv_s[:, pad - (DN_CONV_W - 1):pad, :] = cq_prev_ref[...]

    @pl.when(l > 0)
    def _():
        ua_s[:, 0:pad, :] = ua_s[:, lt:lt + pad, :]
        qkv_s[:, 0:pad, :] = qkv_s[:, lt:lt + pad, :]

    h = _rmsnorm(x_ref[...].reshape(rows, d), g_ref[...]).astype(BF16)

    def proj(lo, hi):
        return jnp.dot(h, wm_ref[:, lo:hi], preferred_element_type=F32)

    qkv0 = 3 * ca

    def proj_qkv(part):
        cols = slice(part * dn, (part + 1) * dn)
        qkv_s[:, pad:pad + lt, cols] = proj(qkv0 + part * dn, qkv0 + (part + 1) * dn).reshape(
            bt, lt, dn)

    def delayed(scr, back, cols):
        if back == 0:
            return scr[:, pad:pad + lt, cols]
        x = scr[:, :, cols]
        width = x.shape[-1]
        n_grp = lt // SUBLANES
        rot = pltpu.roll(x.reshape(bt * (n_grp + 1), SUBLANES, width), back, 1)
        rot = rot.reshape(bt, n_grp + 1, SUBLANES, width)
        own = lax.broadcasted_iota(jnp.int32, (SUBLANES, width), 0) >= back
        return jnp.where(own, rot[:, 1:], rot[:, :n_grp]).reshape(bt, lt, width)

    def conv_silu(part):
        cols = slice(part * dn, (part + 1) * dn)
        conv = cqw_ref[0:1, cols] * delayed(qkv_s, DN_CONV_W - 1, cols)
        for i in range(1, DN_CONV_W):
            conv = conv + cqw_ref[i:i + 1, cols] * delayed(qkv_s, DN_CONV_W - 1 - i, cols)
        return _silu_of_twice(conv)

    def l2norm_to(t, out_ref, scale):
        for hh in range(DN_HEADS):
            sl = slice(hh * hd, (hh + 1) * hd)
            th = t[:, :, sl]
            out_ref[:, :, sl] = th * (lax.rsqrt(jnp.sum(th * th, axis=-1, keepdims=True) + EPS)
                                      * scale)

    proj_qkv(0)
    proj_qkv(1)
    gate_c = proj(ca, 2 * ca)
    l2norm_to(conv_silu(0), q_ref, hd ** -0.5)
    proj_qkv(2)
    h_a = proj(2 * ca, 3 * ca)
    l2norm_to(conv_silu(1), k_ref, 1.0)
    gate_b = proj(0, ca)
    z = proj(3 * ca + 3 * dn, 3 * ca + 4 * dn)
    v_ref[...] = conv_silu(2)

    ua_s[:, pad:pad + lt, :] = (gate_c * h_a).reshape(bt, lt, ca)
    conv = caw_ref[0:1, :] * delayed(ua_s, CONV_A_W - 1, slice(0, ca))
    for i in range(1, CONV_A_W):
        conv = conv + caw_ref[i:i + 1, :] * delayed(ua_s, CONV_A_W - 1 - i, slice(0, ca))
    ya_ref[...] = (gate_b.reshape(bt, lt, ca) * conv).astype(ya_ref.dtype)

    pab = jnp.dot(h, wab_ref[...], preferred_element_type=F32)
    sz_ref[...] = _silu_of_twice(0.5 * z).reshape(bt, lt, dn)
    sp = jnp.maximum(pab + dt_ref[...], 0.0) + jnp.log(1.0 + jnp.exp(-jnp.abs(pab + dt_ref[...])))
    col = lax.broadcasted_iota(jnp.int32, pab.shape, 1)
    gb = jnp.where(col < DN_HEADS, -jnp.exp(alog_ref[...]) * sp, _sigmoid(pab))
    gb_ref[...] = gb.reshape(bt, lt, LANES)

    @pl.when(l == nl - 1)
    def _():
        ca_new_ref[...] = ua_s[:, pad + lt - (CONV_A_W - 1):pad + lt, :]
        cq_new_ref[...] = qkv_s[:, pad + lt - (DN_CONV_W - 1):pad + lt, :]


def _in_even(x, g, w_main, w_ab, conv_a_w, conv_qkv_w, a_log, dt_bias, states, prev_out,
             layer, n_layers, bt, lt):
    batch, length, d = x.shape
    ca = conv_a_w.shape[1]
    dn = DN_HEADS * DN_HEAD_DIM
    nl = length // lt
    grid = (batch // bt, nl)
    has_state = states is not None
    tok_map = lambda b, l: (b, l, 0)
    tok = lambda width: pl.BlockSpec((bt, lt, width), tok_map)
    alog_row = jnp.zeros((1, LANES), F32).at[0, :DN_HEADS].set(a_log)
    dt_row = jnp.zeros((1, LANES), F32).at[0, :DN_HEADS].set(dt_bias)
    in_specs = [tok(d), _resident((1, d)), _layer_rows(w_main, layer), _layer_rows(w_ab, layer),
                _resident(conv_a_w.shape), _resident(conv_qkv_w.shape),
                _resident((1, LANES)), _resident((1, LANES))]
    args = [x, g.reshape(1, d), w_main, w_ab, conv_a_w, conv_qkv_w, alog_row, dt_row]
    ca_block = pl.BlockSpec((None, bt, CONV_A_W - 1, ca), lambda b, l: (layer, b, 0, 0))
    cq_block = pl.BlockSpec((None, bt, DN_CONV_W - 1, 3 * dn), lambda b, l: (layer, b, 0, 0))
    if has_state:
        in_specs += [ca_block, cq_block]
        args += list(states)
    aliases = {}
    if prev_out is not None:
        for j, a in enumerate(prev_out):
            aliases[len(args)] = 6 + j
            in_specs.append(pl.BlockSpec(memory_space=pl.ANY))
            args.append(a)
    out_specs = [tok(ca), tok(dn), tok(dn), tok(dn), tok(dn), tok(LANES), ca_block, cq_block]
    tokshape = lambda width, dt: jax.ShapeDtypeStruct((batch, length, width), dt)
    out_shape = [tokshape(ca, _token_dtype(lt)), tokshape(dn, F32), tokshape(dn, F32),
                 tokshape(dn, F32), tokshape(dn, F32), tokshape(LANES, F32),
                 jax.ShapeDtypeStruct((n_layers, batch, CONV_A_W - 1, ca), F32),
                 jax.ShapeDtypeStruct((n_layers, batch, DN_CONV_W - 1, 3 * dn), F32)]
    scratch = [pltpu.VMEM((bt, lt + SUBLANES, ca), F32),
               pltpu.VMEM((bt, lt + SUBLANES, 3 * dn), F32)]
    outs = pl.pallas_call(
        functools.partial(_in_even_kernel, bt=bt, lt=lt, has_state=has_state,
                          n_alias=len(aliases)),
        grid=grid, in_specs=in_specs, out_specs=out_specs, out_shape=out_shape,
        scratch_shapes=scratch, input_output_aliases=aliases,
        compiler_params=_compiler_params(("arbitrary", "arbitrary")),
        name="in_even")(*args)
    return outs[:6], tuple(outs[6:])


def _even_mix_kernel(*refs, bt, lt, c, has_state, single_step, n_alias):
    n_in = 6 + (1 if has_state else 0)
    (q_ref, k_ref, v_ref, sz_ref, gb_ref, dng_ref) = refs[:6]
    if has_state:
        s_prev_ref = refs[6]
    (y_ref, s_ref, o_s, wq_s, u_s, kd_s, aqk_s, egl_s) = refs[n_in + n_alias:]
    hd = DN_HEAD_DIM
    l = pl.program_id(1)
    rows = bt * lt

    @pl.when(l == 0)
    def _():
        if not has_state:
            s_ref[...] = jnp.zeros(s_ref.shape, F32)
        elif not single_step:
            s_ref[...] = s_prev_ref[...]

    sc = DN_ROWS
    nb = sc // c
    nsc = rows // sc
    groups_per_seq = max(lt // sc, 1)
    shift = int(math.log2(c))
    ri = lax.broadcasted_iota(jnp.int32, (sc, sc), 0)
    ci = lax.broadcasted_iota(jnp.int32, (sc, sc), 1)
    rblk = lax.shift_right_logical(ri, shift)
    same = rblk == lax.shift_right_logical(ci, shift)
    upper = (ri <= ci) & same
    causal = (ri >= ci) & same
    strict = (ri > ci) & same
    eye = ri == ci
    last = ci == (lax.shift_left(rblk, shift) + (c - 1))
    eye_f = eye.astype(F32)
    n_sq = shift - 1

    def group(ref, i, lo, hi):
        if nb == 1:
            b, w = divmod(i, groups_per_seq)
            return ref[b, w * sc:(w + 1) * sc, lo:hi]
        return ref[i * nb:(i + 1) * nb, :, lo:hi].reshape(sc, hi - lo)

    def intra(groups):
        units = [(i, h) for i in groups for h in range(DN_HEADS)]
        xs, ts, rhs, qes = {}, {}, {}, {}
        for (i, h) in units:
            rs = slice(i * sc, (i + 1) * sc)
            sl = slice(h * hd, (h + 1) * hd)
            q = group(q_ref, i, h * hd, (h + 1) * hd)
            k = group(k_ref, i, h * hd, (h + 1) * hd)
            v = group(v_ref, i, h * hd, (h + 1) * hd)
            gb = group(gb_ref, i, 0, LANES)
            beta = gb[:, DN_HEADS + h:DN_HEADS + h + 1]
            g_col = gb[:, h:h + 1]
            g_row = jnp.sum(jnp.where(upper, g_col, 0.0), axis=0, keepdims=True)
            g_rows = jnp.broadcast_to(g_row, (sc, sc))
            g_cum = jnp.sum(jnp.where(eye, g_rows, 0.0), axis=1, keepdims=True)
            g_last = jnp.sum(jnp.where(last, g_rows, 0.0), axis=1, keepdims=True)
            decay = jnp.exp(jnp.where(causal, g_cum - g_rows, -jnp.inf))
            kb = k * beta
            kq = _mm_nt(jnp.concatenate([kb, q], axis=0), k)
            e_g = jnp.exp(g_cum)
            xs[i, h] = -(kq[0:sc] * jnp.where(strict, decay, 0.0))
            ts[i, h] = eye_f + xs[i, h]
            rhs[i, h] = jnp.concatenate([kb * e_g, v * beta], axis=1)
            qes[i, h] = q * e_g
            kd_s[rs, sl] = k * jnp.exp(g_last - g_cum)
            aqk_s[i, h] = kq[sc:2 * sc] * decay
            egl_s[i, h] = jnp.broadcast_to(jnp.exp(g_last), (sc, hd))
        for _ in range(n_sq):
            for u in units:
                xs[u] = _mm(xs[u], xs[u])
            for u in units:
                ts[u] = ts[u] + _mm(ts[u], xs[u])
        for (i, h) in units:
            rs = slice(i * sc, (i + 1) * sc)
            sl = slice(h * hd, (h + 1) * hd)
            wu = _mm(ts[i, h], rhs[i, h])
            w = wu[:, 0:hd]
            qe = qes[i, h]
            for j in range(nb):
                wq_s[i * nb + j, h, 0:c] = w[j * c:(j + 1) * c]
                wq_s[i * nb + j, h, c:2 * c] = qe[j * c:(j + 1) * c]
            u_s[rs, sl] = wu[:, hd:2 * hd]

    s_in = s_prev_ref if (has_state and single_step) else s_ref

    def inter(groups):
        units = [(i, h, j) for i in groups for h in range(DN_HEADS) for j in range(nb)]
        bidx = lambda i, j: (i * nb + j) if nb > 1 else i // groups_per_seq
        rows_of = lambda i, j: slice(i * sc + j * c, i * sc + (j + 1) * c)
        cols_of = lambda h: slice(h * hd, (h + 1) * hd)
        ss = {(i, h, j): s_in[bidx(i, j), h] for (i, h, j) in units}
        tqs = {(i, h, j): _mm(wq_s[i * nb + j, h], ss[i, h, j]) for (i, h, j) in units}
        us = {(i, h, j): u_s[rows_of(i, j), cols_of(h)] - tqs[i, h, j][0:c] for (i, h, j) in units}
        upd = {(i, h, j): _mm_tn(kd_s[rows_of(i, j), cols_of(h)], us[i, h, j])
               for (i, h, j) in units}
        for (i, h, j) in units:
            s_ref[bidx(i, j), h] = (ss[i, h, j] * egl_s[i, h, j * c:j * c + 1, :] + upd[i, h, j])
        for i in groups:
            for h in range(DN_HEADS):
                if nb > 1:
                    u = jnp.concatenate([us[i, h, j] for j in range(nb)], axis=0)
                    qs = jnp.concatenate([tqs[i, h, j][c:2 * c] for j in range(nb)], axis=0)
                else:
                    u = us[i, h, 0]
                    qs = tqs[i, h, 0][c:2 * c]
                o_s[i * sc:(i + 1) * sc, cols_of(h)] = qs + _mm(aqk_s[i, h], u)

    wave = INTRA_WAVE
    for i0 in range(0, nsc, wave):
        intra(range(i0, min(i0 + wave, nsc)))
    if nb > 1:
        for i in range(nsc):
            inter([i])
    else:
        for w in range(groups_per_seq):
            inter([b * groups_per_seq + w for b in range(bt)])

    for h in range(DN_HEADS):
        sl = slice(h * hd, (h + 1) * hd)
        o = o_s[:, sl]
        o = (o * lax.rsqrt(jnp.mean(o * o, axis=-1, keepdims=True) + EPS) * dng_ref[...]
             * sz_ref[:, :, sl].reshape(rows, hd))
        y_ref[:, :, sl] = o.reshape(bt, lt, hd).astype(y_ref.dtype)


def _even_mix(q, k, v, sz, gb, dn_norm, s_prev, prev_out, layer, n_layers, bt, lt):
    batch, length, dn = q.shape
    c = CHUNK if length % CHUNK == 0 else length
    hd = DN_HEAD_DIM
    nl = length // lt
    rows = bt * lt
    assert rows % DN_ROWS == 0 and DN_ROWS % c == 0 and (c == DN_ROWS or lt == c)
    nsc = rows // DN_ROWS
    nb = DN_ROWS // c
    grid = (batch // bt, nl)
    has_state = s_prev is not None
    tok_map = lambda b, l: (b, l, 0)
    tok = lambda width: pl.BlockSpec((bt, lt, width), tok_map)
    in_specs = [tok(dn), tok(dn), tok(dn), tok(dn), tok(LANES), _resident((1, hd))]
    args = [q, k, v, sz, gb, dn_norm.reshape(1, hd)]
    state_block = pl.BlockSpec((None, bt, DN_HEADS, hd, hd), lambda b, l: (layer, b, 0, 0, 0))
    if has_state:
        in_specs.append(state_block)
        args.append(s_prev)
    aliases = {}
    if prev_out is not None:
        aliases[len(args)] = 1
        in_specs.append(pl.BlockSpec(memory_space=pl.ANY))
        args.append(prev_out)
    out_specs = [tok(dn), state_block]
    out_shape = [jax.ShapeDtypeStruct((batch, length, dn), _token_dtype(lt)),
                 jax.ShapeDtypeStruct((n_layers, batch, DN_HEADS, hd, hd), F32)]
    scratch = [pltpu.VMEM((rows, dn), F32),
               pltpu.VMEM((nsc * nb, DN_HEADS, 2 * c, hd), F32),
               pltpu.VMEM((rows, dn), F32), pltpu.VMEM((rows, dn), F32),
               pltpu.VMEM((nsc, DN_HEADS, DN_ROWS, DN_ROWS), F32),
               pltpu.VMEM((nsc, DN_HEADS, DN_ROWS, hd), F32)]
    return pl.pallas_call(
        functools.partial(_even_mix_kernel, bt=bt, lt=lt, c=c, has_state=has_state,
                          single_step=nl == 1, n_alias=len(aliases)),
        grid=grid, in_specs=in_specs, out_specs=out_specs, out_shape=out_shape,
        scratch_shapes=scratch, input_output_aliases=aliases,
        compiler_params=_compiler_params(("arbitrary", "arbitrary")),
        name="even_mix")(*args)


def _in_odd_kernel(*refs, starts, steps):
    ng = len(steps)
    g_ref, w_ref = refs[3 * ng:3 * ng + 2]
    outs = refs[3 * ng + 2:]
    dq = RET_HEADS * RET_QK_DIM
    dv = RET_HEADS * RET_V_DIM

    def body(x_ref, cos_ref, sin_ref, q_ref, k_ref, v_ref, sg_ref):
        h = _rmsnorm(x_ref[...], g_ref[...]).astype(BF16)

        def proj(lo, hi):
            return jnp.dot(h, w_ref[:, lo:hi], preferred_element_type=F32)

        cos2 = cos_ref[...]
        sin2 = sin_ref[...]

        def rotary(t, out_ref, scale):
            c2, s2 = (cos2, sin2) if scale is None else (cos2 * scale, sin2 * scale)
            for hh in range(RET_HEADS):
                sl = slice(hh * RET_QK_DIM, (hh + 1) * RET_QK_DIM)
                th = t[:, sl]
                r = th * c2 + pltpu.roll(th, RET_QK_DIM // 2, 1) * s2
                out_ref[:, sl] = r.astype(out_ref.dtype)

        q = proj(0, dq)
        k = proj(dq, 2 * dq)
        rotary(q, q_ref, None)
        v = proj(2 * dq, 2 * dq + dv)
        rotary(k, k_ref, RET_QK_DIM ** -0.5)
        gate = proj(2 * dq + dv, 2 * dq + 2 * dv)
        v_ref[...] = v.astype(v_ref.dtype)
        sg_ref[...] = _silu_of_twice(gate).astype(sg_ref.dtype)

    _for_each_group(starts, steps, [
        functools.partial(body, *refs[3 * g:3 * g + 3], *outs[4 * g:4 * g + 4]) for g in range(ng)])


def _in_odd(xs, g, w, layer, poss, seq_lens, tms):
    d = xs[0].shape[1]
    dq = RET_HEADS * RET_QK_DIM
    dv = RET_HEADS * RET_V_DIM
    half = RET_QK_DIM // 2
    inv = ROPE_BASE ** (-jnp.arange(half, dtype=F32) / half)
    steps = [x.shape[0] // tm for x, tm in zip(xs, tms)]
    starts = [sum(steps[:i]) for i in range(len(steps))]
    args, in_specs, out_specs, out_shape = [], [], [], []
    for x, pos, seq_len, tm, start, n in zip(xs, poss, seq_lens, tms, starts, steps):
        ang = pos.astype(F32)[:, None] * inv[None, :]
        cos = jnp.cos(ang)
        sin = jnp.sin(ang)
        cos2 = jnp.concatenate([cos, cos], axis=-1)
        sin2 = jnp.concatenate([-sin, sin], axis=-1)
        if seq_len < tm:
            cos2 = jnp.tile(cos2, (tm // seq_len, 1))
            sin2 = jnp.tile(sin2, (tm // seq_len, 1))
        n_tab = cos2.shape[0] // tm
        index, row = _group_rows(start, n, tm)
        tab = pl.BlockSpec((tm, RET_QK_DIM),
                           lambda i, index=index, n_tab=n_tab: (index(i) % n_tab, 0))
        args += [x, cos2, sin2]
        in_specs += [row(d), tab, tab]
        out_specs += [row(dq), row(dq), row(dv), row(dv)]
        out_shape += [jax.ShapeDtypeStruct((x.shape[0], width), BF16) for width in (dq, dq, dv, dv)]
    outs = pl.pallas_call(
        functools.partial(_in_odd_kernel, starts=tuple(starts), steps=tuple(steps)),
        grid=(sum(steps),), in_specs=in_specs + [_resident((1, d)), _layer_rows(w, layer)],
        out_specs=out_specs, out_shape=out_shape,
        compiler_params=_compiler_params(("arbitrary",)),
        name="in_odd")(*args, g.reshape(1, d), w)
    return [outs[4 * i:4 * i + 4] for i in range(len(xs))]


def _odd_mix_kernel(*refs, bt, lt, has_state, single_step, tok3d, n_alias):
    n_in = 6 + (1 if has_state else 0)
    (q_ref, k_ref, v_ref, sg_ref, gng_ref, gnb_ref) = refs[:6]
    if has_state:
        r_prev_ref = refs[6]
    (y_ref, r_ref, o_s, dec_s, inn_s, kdec_s) = refs[n_in + n_alias:]
    c = lt
    l = pl.program_id(1)

    lgs = [math.log(1.0 - 2.0 ** (-5.0 - h)) for h in range(RET_HEADS)]

    @pl.when((pl.program_id(0) == 0) & (l == 0))
    def _():
        ri = lax.broadcasted_iota(jnp.int32, (c, c), 0)
        ci = lax.broadcasted_iota(jnp.int32, (c, c), 1)
        diff = (ri - ci).astype(F32)
        idx = lax.broadcasted_iota(jnp.int32, (c, RET_QK_DIM), 0).astype(F32)
        for h in range(RET_HEADS):
            dec_s[h] = jnp.where(diff >= 0, jnp.exp(jnp.maximum(diff, 0.0) * lgs[h]), 0.0)
            inn_s[h] = jnp.exp((idx + 1.0) * lgs[h])
            kdec_s[h] = jnp.exp((c - 1.0 - idx) * lgs[h])

    @pl.when(l == 0)
    def _():
        if not has_state:
            r_ref[...] = jnp.zeros(r_ref.shape, F32)
        elif not single_step:
            r_ref[...] = r_prev_ref[...]

    hg = 1 if c >= LANES else RET_HEADS
    r_in = r_prev_ref if (has_state and single_step) else r_ref
    def norm_gate(o, h, b=None):
        sl = slice(h * RET_V_DIM, (h + 1) * RET_V_DIM)
        mu = jnp.mean(o, axis=-1, keepdims=True)
        var = jnp.mean(jnp.square(o - mu), axis=-1, keepdims=True)
        o = (o - mu) * lax.rsqrt(var + GN_EPS) * gng_ref[:, sl] + gnb_ref[:, sl]
        if tok3d:
            y_ref[b, :, sl] = (sg_ref[b, :, sl].astype(F32) * o).astype(y_ref.dtype)
        else:
            y_ref[:, sl] = (sg_ref[:, sl].astype(F32) * o).astype(y_ref.dtype)

    def seq_rows(ref, b, lo, hi):
        if tok3d:
            return ref[b, :, lo:hi]
        if bt == 1:
            return ref[:, lo:hi]
        return ref[:, lo:hi].astype(F32)[b * c:(b + 1) * c]

    for b in range(bt):
        rs = slice(b * c, (b + 1) * c)
        for h0 in range(0, RET_HEADS, hg):
            heads = range(h0, h0 + hg)
            qs = {h: seq_rows(q_ref, b, h * RET_QK_DIM, (h + 1) * RET_QK_DIM) for h in heads}
            ks = {h: seq_rows(k_ref, b, h * RET_QK_DIM, (h + 1) * RET_QK_DIM) for h in heads}
            vs = {h: seq_rows(v_ref, b, h * RET_V_DIM, (h + 1) * RET_V_DIM) for h in heads}
            rr = {h: r_in[b, h] for h in heads}
            att = {h: _mm_nt(qs[h], ks[h]) * dec_s[h] for h in heads}
            qr = {h: _mm(qs[h].astype(F32) * inn_s[h], rr[h]) for h in heads}
            kv = {h: _mm_tn(ks[h].astype(F32) * kdec_s[h], vs[h]) for h in heads}
            for h in heads:
                r_ref[b, h] = rr[h] * math.exp(c * lgs[h]) + kv[h]
            for h in heads:
                o = _mm(att[h], vs[h]) + qr[h]
                if tok3d or bt == 1:
                    norm_gate(o, h, b)
                else:
                    o_s[rs, h * RET_V_DIM:(h + 1) * RET_V_DIM] = o

    if bt > 1 and not tok3d:
        for h in range(RET_HEADS):
            norm_gate(o_s[:, h * RET_V_DIM:(h + 1) * RET_V_DIM], h)


def _odd_mix(q, k, v, sg, gn_g, gn_b, r_prev, prev_out, layer, n_layers, batch, length, bt, lt):
    dq = RET_HEADS * RET_QK_DIM
    dv = RET_HEADS * RET_V_DIM
    nl = length // lt
    rows = bt * lt
    grid = (batch // bt, nl)
    has_state = r_prev is not None
    tok3d = nl > 1
    if tok3d:
        row = lambda width: pl.BlockSpec((bt, lt, width), lambda b, l: (b, l, 0))
        q, k, v, sg = [t.reshape(batch, length, t.shape[-1]) for t in (q, k, v, sg)]
        y_shape = (batch, length, dv)
    else:
        row = lambda width: pl.BlockSpec((rows, width), lambda b, l: (b, 0))
        y_shape = (batch * length, dv)
    in_specs = [row(dq), row(dq), row(dv), row(dv), _resident((1, dv)), _resident((1, dv))]
    args = [q, k, v, sg, gn_g.reshape(1, dv), gn_b.reshape(1, dv)]
    state_block = pl.BlockSpec((None, bt, RET_HEADS, RET_QK_DIM, RET_V_DIM),
                               lambda b, l: (layer, b, 0, 0, 0))
    if has_state:
        in_specs.append(state_block)
        args.append(r_prev)
    aliases = {}
    if prev_out is not None:
        aliases[len(args)] = 1
        in_specs.append(pl.BlockSpec(memory_space=pl.ANY))
        args.append(prev_out)
    out_specs = [row(dv), state_block]
    out_shape = [jax.ShapeDtypeStruct(y_shape, BF16),
                 jax.ShapeDtypeStruct((n_layers, batch, RET_HEADS, RET_QK_DIM, RET_V_DIM), F32)]
    scratch = [pltpu.VMEM((rows, dv), F32),
               pltpu.VMEM((RET_HEADS, lt, lt), F32),
               pltpu.VMEM((RET_HEADS, lt, RET_QK_DIM), F32),
               pltpu.VMEM((RET_HEADS, lt, RET_QK_DIM), F32)]
    y, r_new = pl.pallas_call(
        functools.partial(_odd_mix_kernel, bt=bt, lt=lt, has_state=has_state,
                          single_step=nl == 1, tok3d=tok3d, n_alias=len(aliases)),
        grid=grid, in_specs=in_specs, out_specs=out_specs, out_shape=out_shape,
        scratch_shapes=scratch, input_output_aliases=aliases,
        compiler_params=_compiler_params(("arbitrary", "arbitrary")),
        name="odd_mix")(*args)
    return y.reshape(batch * length, dv), r_new


def _out_ffn_kernel(*refs, n_y, starts, steps, final):
    ng = len(steps)
    per = 1 + n_y
    rest = refs[ng * per:]
    wo_refs = rest[:n_y]
    g_ref, wg_ref, wu_ref, wd_ref, gf_ref = rest[n_y:n_y + 5]
    o_refs = rest[n_y + 5:]

    def body(x_ref, y_refs, o_ref):
        x1 = x_ref[...]
        for y_ref, wo_ref in zip(y_refs, wo_refs):
            x1 = x1 + jnp.dot(y_ref[...].astype(BF16), wo_ref[...], preferred_element_type=F32)
        h = _rmsnorm(x1, g_ref[...]).astype(BF16)
        gate = jnp.dot(h, wg_ref[...], preferred_element_type=F32)
        up = jnp.dot(h, wu_ref[...], preferred_element_type=F32)
        a = (_silu_of_twice(gate) * up).astype(BF16)
        x2 = x1 + jnp.dot(a, wd_ref[...], preferred_element_type=F32)
        if final:
            x2 = _rmsnorm(x2, gf_ref[...])
        o_ref[...] = x2

    _for_each_group(starts, steps, [
        functools.partial(body, refs[g * per], refs[g * per + 1:(g + 1) * per], o_refs[g])
        for g in range(ng)])


def _out_ffn(xs, yss, wo, wo_layer, g, wg, wu, wd, ffn_layer, gf, final, tms):
    d = xs[0].shape[1]
    n_y = len(yss[0])
    assert len({y.shape[1] for ys in yss for y in ys}) == 1
    steps = [x.shape[0] // tm for x, tm in zip(xs, tms)]
    starts = [sum(steps[:i]) for i in range(len(steps))]
    args, in_specs, out_specs, out_shape = [], [], [], []
    for x, ys, tm, start, n in zip(xs, yss, tms, starts, steps):
        _, row = _group_rows(start, n, tm)
        args += [x] + list(ys)
        in_specs += [row(d)] + [row(y.shape[1]) for y in ys]
        out_specs.append(row(d))
        out_shape.append(jax.ShapeDtypeStruct(x.shape, F32))
    in_specs += ([_layer_rows(wo, wo_layer, j, n_y) for j in range(n_y)]
                 + [_resident((1, d)), _layer_rows(wg, ffn_layer), _layer_rows(wu, ffn_layer),
                    _layer_rows(wd, ffn_layer), _resident((1, d))])
    return pl.pallas_call(
        functools.partial(_out_ffn_kernel, n_y=n_y, starts=tuple(starts), steps=tuple(steps),
                          final=final),
        grid=(sum(steps),), in_specs=in_specs, out_specs=out_specs, out_shape=out_shape,
        compiler_params=_compiler_params(("arbitrary",)),
        name="out_ffn")(*args, *([wo] * n_y), g.reshape(1, d), wg, wu, wd, gf.reshape(1, d))


def _tiles(batch, length):
    tm = min(batch * length, ROW_TILE)
    if length % CHUNK == 0:
        lt = min(length, 256)
        return dict(tm=tm, in_even=(1, min(length, tm)), even=(math.gcd(batch, 2), lt),
                    odd=(math.gcd(batch, 2), lt))
    return dict(tm=tm, in_even=(max(tm // length, 1), length), even=(min(batch, 16), length),
                odd=(min(batch, 8), length))


def _run_trunk(groups, w):
    depth = w["norm_mix"].shape[0]
    n_even = (depth + 1) // 2
    n_odd = depth // 2
    gs = []
    for x, states, pos in groups:
        batch, length, d = x.shape
        gs.append(dict(batch=batch, length=length, d=d, states=states, pos=pos,
                       t=_tiles(batch, length), x=x.reshape(batch * length, d),
                       conv_out=None,
                       delta_out=None, ret_out=None))
    tms = [g["t"]["tm"] for g in gs]
    for i in range(depth):
        if i % 2 == 0:
            e = i // 2
            for g in gs:
                st = g["states"]
                (ya, q, k, v, sz, gb), g["conv_out"] = _in_even(
                    g["x"].reshape(g["batch"], g["length"], g["d"]), w["norm_mix"][i],
                    w["even_w_main"], w["even_w_ab"], w["even_conv_a"][e], w["even_conv_qkv"][e],
                    w["even_a_log"][e], w["even_dt_bias"][e],
                    None if st is None else (st[0], st[1]), g["conv_out"], e, n_even,
                    *g["t"]["in_even"])
                yb, g["delta_out"] = _even_mix(q, k, v, sz, gb, w["even_dn_norm"][e],
                                               None if st is None else st[2], g["delta_out"],
                                               e, n_even, *g["t"]["even"])
                rows = g["batch"] * g["length"]
                g["ys"] = [ya.reshape(rows, -1), yb.reshape(rows, -1)]
            wo, wo_layer = w["even_w_out"], e
        else:
            o = i // 2
            qkvs = _in_odd([g["x"] for g in gs], w["norm_mix"][i], w["odd_w_in"], o,
                           [g["pos"] for g in gs], [g["length"] for g in gs], tms)
            for g, (q, k, v, sg) in zip(gs, qkvs):
                st = g["states"]
                y, g["ret_out"] = _odd_mix(q, k, v, sg, w["odd_gn_g"][o], w["odd_gn_b"][o],
                                           None if st is None else st[3], g["ret_out"], o, n_odd,
                                           g["batch"], g["length"], *g["t"]["odd"])
                g["ys"] = [y]
            wo, wo_layer = w["odd_w_out"], o
        xs = _out_ffn([g["x"] for g in gs], [g["ys"] for g in gs], wo, wo_layer, w["norm_ffn"][i],
                      w["ffn_w_gate"], w["ffn_w_up"], w["ffn_w_down"], i, w["final_norm"],
                      i == depth - 1, tms)
        for g, x in zip(gs, xs):
            g["x"] = x
    return [(g["x"].reshape(g["batch"], g["length"], g["d"]),) + g["conv_out"]
            + (g["delta_out"], g["ret_out"]) for g in gs]


def kernel(x_prompt, x_sample, state_conv_a, state_conv_qkv, state_delta, state_ret, norm_mix,
           norm_ffn, final_norm, even_w_in, even_conv_a, even_conv_qkv, even_a_log, even_dt_bias,
           even_dn_norm, even_w_out, odd_w_in, odd_gn_g, odd_gn_b, odd_w_out, ffn_w_gate, ffn_w_up,
           ffn_w_down):
    n_main = even_w_in.shape[-1] - 2 * DN_HEADS
    dv = RET_HEADS * RET_V_DIM
    odd_cols = jnp.arange(odd_w_in.shape[-1])
    odd_half = jnp.where(odd_cols >= odd_w_in.shape[-1] - dv, 0.5, 1.0)
    w = dict(norm_mix=norm_mix, norm_ffn=norm_ffn, final_norm=final_norm,
             even_w_main=even_w_in.astype(BF16),
             even_w_ab=jnp.pad(even_w_in[:, :, n_main:],
                               ((0, 0), (0, 0), (0, LANES - 2 * DN_HEADS))).astype(BF16),
             even_conv_a=even_conv_a, even_conv_qkv=0.5 * even_conv_qkv, even_a_log=even_a_log,
             even_dt_bias=even_dt_bias, even_dn_norm=even_dn_norm,
             even_w_out=even_w_out.astype(BF16), odd_w_in=(odd_w_in * odd_half).astype(BF16),
             odd_gn_g=odd_gn_g, odd_gn_b=odd_gn_b, odd_w_out=odd_w_out.astype(BF16),
             ffn_w_gate=(0.5 * ffn_w_gate).astype(BF16), ffn_w_up=ffn_w_up.astype(BF16),
             ffn_w_down=ffn_w_down.astype(BF16))
    lp = x_prompt.shape[1]
    ls = x_sample.shape[1]
    pos_p = jnp.arange(lp, dtype=jnp.int32)
    pos_s = PAST_LEN + jnp.arange(ls, dtype=jnp.int32)
    (y_prompt, ca_p, cq_p, d_p, r_p), (y_sample, ca_s, cq_s, d_s, r_s) = _run_trunk(
        [(x_prompt, None, pos_p),
         (x_sample, (state_conv_a, state_conv_qkv, state_delta, state_ret), pos_s)], w)
    return (y_prompt, y_sample, ca_p, cq_p, d_p, r_p, ca_s, cq_s, d_s, r_s)
```

```python
import functools
import math

import jax
import jax.numpy as jnp
from jax import lax
from jax.experimental import pallas as pl
from jax.experimental.pallas import tpu as pltpu

F32 = jnp.float32
BF16 = jnp.bfloat16

EPS = 1e-6
GN_EPS = 1e-5
ROPE_BASE = 10000.0
PAST_LEN = 16384
CHUNK = 64

CONV_A_W = 3
DN_CONV_W = 4
DN_HEADS = 4
DN_HEAD_DIM = 128
RET_HEADS = 8
RET_QK_DIM = 128
RET_V_DIM = 256

LANES = 128
SUBLANES = 8
VMEM_LIMIT_BYTES = 56 * 1024 * 1024
DN_ROWS = 64
INTRA_WAVE = 8
ROW_TILE = 512


def _compiler_params(semantics):
    return pltpu.CompilerParams(dimension_semantics=semantics,
                                vmem_limit_bytes=VMEM_LIMIT_BYTES)


def _resident(shape):
    nd = len(shape)
    return pl.BlockSpec(shape, lambda *_: (0,) * nd, pipeline_mode=pl.Buffered(1))


def _layer_rows(stack, layer, part=0, parts=1):
    _, k, n = stack.shape
    return pl.BlockSpec((None, k // parts, n), lambda *_: (layer, part, 0),
                        pipeline_mode=pl.Buffered(1))


def _group_rows(start, steps, tm):
    def index(i):
        return jnp.minimum(jnp.maximum(i - start, 0), steps - 1)
    return index, (lambda width: pl.BlockSpec((tm, width), lambda i: (index(i), 0)))


def _for_each_group(starts, steps, bodies):
    i = pl.program_id(0)
    if len(bodies) == 1:
        bodies[0]()
        return
    for start, n, body in zip(starts, steps, bodies):
        pl.when((i >= start) & (i < start + n))(body)


def _mm(a, b):
    return jnp.dot(a.astype(BF16), b.astype(BF16), preferred_element_type=F32)


def _mm_nt(a, b):
    return lax.dot_general(a.astype(BF16), b.astype(BF16), (((1,), (1,)), ((), ())),
                           preferred_element_type=F32)


def _mm_tn(a, b):
    return lax.dot_general(a.astype(BF16), b.astype(BF16), (((0,), (0,)), ((), ())),
                           preferred_element_type=F32)


def _sigmoid(x):
    return 0.5 * jnp.tanh(0.5 * x) + 0.5


def _silu_of_twice(hx):
    return hx * jnp.tanh(hx) + hx


def _rmsnorm(x, g):
    return x * lax.rsqrt(jnp.mean(x * x, axis=-1, keepdims=True) + EPS) * g


def _token_dtype(lt):
    return BF16 if lt % (2 * SUBLANES) == 0 else F32


def _in_even_kernel(*refs, bt, lt, has_state, n_alias):
    n_in = 8 + (2 if has_state else 0)
    (x_ref, g_ref, wm_ref, wab_ref, caw_ref, cqw_ref, alog_ref, dt_ref) = refs[:8]
    if has_state:
        ca_prev_ref, cq_prev_ref = refs[8:10]
    (ya_ref, q_ref, k_ref, v_ref, sz_ref, gb_ref, ca_new_ref, cq_new_ref,
     ua_s, qkv_s) = refs[n_in + n_alias:]
    hd = DN_HEAD_DIM
    dn = DN_HEADS * hd
    ca = ua_s.shape[-1]
    d = x_ref.shape[-1]
    l = pl.program_id(1)
    nl = pl.num_programs(1)
    rows = bt * lt
    pad = SUBLANES

    @pl.when(l == 0)
    def _():
        ua_s[:, 0:pad, :] = jnp.zeros((bt, pad, ca), F32)
        qkv_s[:, 0:pad, :] = jnp.zeros((bt, pad, 3 * dn), F32)
        if has_state:
            ua_s[:, pad - (CONV_A_W - 1):pad, :] = ca_prev_ref[...]
            qkv_s[:, pad - (DN_CONV_W - 1):pad, :] = cq_prev_ref[...]

    @pl.when(l > 0)
    def _():
        ua_s[:, 0:pad, :] = ua_s[:, lt:lt + pad, :]
        qkv_s[:, 0:pad, :] = qkv_s[:, lt:lt + pad, :]

    h = _rmsnorm(x_ref[...].reshape(rows, d), g_ref[...]).astype(BF16)

    def proj(lo, hi):
        return jnp.dot(h, wm_ref[:, lo:hi], preferred_element_type=F32)

    qkv0 = 3 * ca

    def proj_qkv(part):
        cols = slice(part * dn, (part + 1) * dn)
        qkv_s[:, pad:pad + lt, cols] = proj(qkv0 + part * dn, qkv0 + (part + 1) * dn).reshape(
            bt, lt, dn)

    def delayed(scr, back, cols):
        if back == 0:
            return scr[:, pad:pad + lt, cols]
        x = scr[:, :, cols]
        width = x.shape[-1]
        n_grp = lt // SUBLANES
        rot = pltpu.roll(x.reshape(bt * (n_grp + 1), SUBLANES, width), back, 1)
        rot = rot.reshape(bt, n_grp + 1, SUBLANES, width)
        own = lax.broadcasted_iota(jnp.int32, (SUBLANES, width), 0) >= back
        return jnp.where(own, rot[:, 1:], rot[:, :n_grp]).reshape(bt, lt, width)

    def conv_silu(part):
        cols = slice(part * dn, (part + 1) * dn)
        conv = cqw_ref[0:1, cols] * delayed(qkv_s, DN_CONV_W - 1, cols)
        for i in range(1, DN_CONV_W):
            conv = conv + cqw_ref[i:i + 1, cols] * delayed(qkv_s, DN_CONV_W - 1 - i, cols)
        return _silu_of_twice(conv)

    def l2norm_to(t, out_ref, scale):
        for hh in range(DN_HEADS):
            sl = slice(hh * hd, (hh + 1) * hd)
            th = t[:, :, sl]
            out_ref[:, :, sl] = th * (lax.rsqrt(jnp.sum(th * th, axis=-1, keepdims=True) + EPS)
                                      * scale)

    proj_qkv(0)
    proj_qkv(1)
    gate_c = proj(ca, 2 * ca)
    l2norm_to(conv_silu(0), q_ref, hd ** -0.5)
    proj_qkv(2)
    h_a = proj(2 * ca, 3 * ca)
    l2norm_to(conv_silu(1), k_ref, 1.0)
    gate_b = proj(0, ca)
    z = proj(3 * ca + 3 * dn, 3 * ca + 4 * dn)
    v_ref[...] = conv_silu(2)

    ua_s[:, pad:pad + lt, :] = (gate_c * h_a).reshape(bt, lt, ca)
    conv = caw_ref[0:1, :] * delayed(ua_s, CONV_A_W - 1, slice(0, ca))
    for i in range(1, CONV_A_W):
        conv = conv + caw_ref[i:i + 1, :] * delayed(ua_s, CONV_A_W - 1 - i, slice(0, ca))
    ya_ref[...] = (gate_b.reshape(bt, lt, ca) * conv).astype(ya_ref.dtype)

    pab = jnp.dot(h, wab_ref[...], preferred_element_type=F32)
    sz_ref[...] = _silu_of_twice(0.5 * z).reshape(bt, lt, dn)
    sp = jnp.maximum(pab + dt_ref[...], 0.0) + jnp.log(1.0 + jnp.exp(-jnp.abs(pab + dt_ref[...])))
    col = lax.broadcasted_iota(jnp.int32, pab.shape, 1)
    gb = jnp.where(col < DN_HEADS, -jnp.exp(alog_ref[...]) * sp, _sigmoid(pab))
    gb_ref[...] = gb.reshape(bt, lt, LANES)

    @pl.when(l == nl - 1)
    def _():
        ca_new_ref[...] = ua_s[:, pad + lt - (CONV_A_W - 1):pad + lt, :]
        cq_new_ref[...] = qkv_s[:, pad + lt - (DN_CONV_W - 1):pad + lt, :]


def _in_even(x, g, w_main, w_ab, conv_a_w, conv_qkv_w, a_log, dt_bias, states, prev_out,
             layer, n_layers, bt, lt):
    batch, length, d = x.shape
    ca = conv_a_w.shape[1]
    dn = DN_HEADS * DN_HEAD_DIM
    nl = length // lt
    grid = (batch // bt, nl)
    has_state = states is not None
    tok_map = lambda b, l: (b, l, 0)
    tok = lambda width: pl.BlockSpec((bt, lt, width), tok_map)
    alog_row = jnp.zeros((1, LANES), F32).at[0, :DN_HEADS].set(a_log)
    dt_row = jnp.zeros((1, LANES), F32).at[0, :DN_HEADS].set(dt_bias)
    in_specs = [tok(d), _resident((1, d)), _layer_rows(w_main, layer), _layer_rows(w_ab, layer),
                _resident(conv_a_w.shape), _resident(conv_qkv_w.shape),
                _resident((1, LANES)), _resident((1, LANES))]
    args = [x, g.reshape(1, d), w_main, w_ab, conv_a_w, conv_qkv_w, alog_row, dt_row]
    ca_block = pl.BlockSpec((None, bt, CONV_A_W - 1, ca), lambda b, l: (layer, b, 0, 0))
    cq_block = pl.BlockSpec((None, bt, DN_CONV_W - 1, 3 * dn), lambda b, l: (layer, b, 0, 0))
    if has_state:
        in_specs += [ca_block, cq_block]
        args += list(states)
    aliases = {}
    if prev_out is not None:
        for j, a in enumerate(prev_out):
            aliases[len(args)] = 6 + j
            in_specs.append(pl.BlockSpec(memory_space=pl.ANY))
            args.append(a)
    out_specs = [tok(ca), tok(dn), tok(dn), tok(dn), tok(dn), tok(LANES), ca_block, cq_block]
    tokshape = lambda width, dt: jax.ShapeDtypeStruct((batch, length, width), dt)
    out_shape = [tokshape(ca, _token_dtype(lt)), tokshape(dn, F32), tokshape(dn, F32),
                 tokshape(dn, F32), tokshape(dn, F32), tokshape(LANES, F32),
                 jax.ShapeDtypeStruct((n_layers, batch, CONV_A_W - 1, ca), F32),
                 jax.ShapeDtypeStruct((n_layers, batch, DN_CONV_W - 1, 3 * dn), F32)]
    scratch = [pltpu.VMEM((bt, lt + SUBLANES, ca), F32),
               pltpu.VMEM((bt, lt + SUBLANES, 3 * dn), F32)]
    outs = pl.pallas_call(
        functools.partial(_in_even_kernel, bt=bt, lt=lt, has_state=has_state,
                          n_alias=len(aliases)),
        grid=grid, in_specs=in_specs, out_specs=out_specs, out_shape=out_shape,
        scratch_shapes=scratch, input_output_aliases=aliases,
        compiler_params=_compiler_params(("arbitrary", "arbitrary")),
        name="in_even")(*args)
    return outs[:6], tuple(outs[6:])


def _even_mix_kernel(*refs, bt, lt, c, has_state, single_step, n_alias):
    n_in = 6 + (1 if has_state else 0)
    (q_ref, k_ref, v_ref, sz_ref, gb_ref, dng_ref) = refs[:6]
    if has_state:
        s_prev_ref = refs[6]
    (y_ref, s_ref, o_s, wq_s, u_s, kd_s, aqk_s, egl_s) = refs[n_in + n_alias:]
    hd = DN_HEAD_DIM
    l = pl.program_id(1)
    rows = bt * lt

    @pl.when(l == 0)
    def _():
        if not has_state:
            s_ref[...] = jnp.zeros(s_ref.shape, F32)
        elif not single_step:
            s_ref[...] = s_prev_ref[...]

    sc = DN_ROWS
    nb = sc // c
    nsc = rows // sc
    groups_per_seq = max(lt // sc, 1)
    shift = int(math.log2(c))
    ri = lax.broadcasted_iota(jnp.int32, (sc, sc), 0)
    ci = lax.broadcasted_iota(jnp.int32, (sc, sc), 1)
    rblk = lax.shift_right_logical(ri, shift)
    same = rblk == lax.shift_right_logical(ci, shift)
    upper = (ri <= ci) & same
    causal = (ri >= ci) & same
    strict = (ri > ci) & same
    eye = ri == ci
    last = ci == (lax.shift_left(rblk, shift) + (c - 1))
    eye_f = eye.astype(F32)
    n_sq = shift - 1

    def group(ref, i, lo, hi):
        if nb == 1:
            b, w = divmod(i, groups_per_seq)
            return ref[b, w * sc:(w + 1) * sc, lo:hi]
        return ref[i * nb:(i + 1) * nb, :, lo:hi].reshape(sc, hi - lo)

    def intra(groups):
        units = [(i, h) for i in groups for h in range(DN_HEADS)]
        xs, ts, rhs, qes = {}, {}, {}, {}
        for (i, h) in units:
            rs = slice(i * sc, (i + 1) * sc)
            sl = slice(h * hd, (h + 1) * hd)
            q = group(q_ref, i, h * hd, (h + 1) * hd)
            k = group(k_ref, i, h * hd, (h + 1) * hd)
            v = group(v_ref, i, h * hd, (h + 1) * hd)
            gb = group(gb_ref, i, 0, LANES)
            beta = gb[:, DN_HEADS + h:DN_HEADS + h + 1]
            g_col = gb[:, h:h + 1]
            g_row = jnp.sum(jnp.where(upper, g_col, 0.0), axis=0, keepdims=True)
            g_rows = jnp.broadcast_to(g_row, (sc, sc))
            g_cum = jnp.sum(jnp.where(eye, g_rows, 0.0), axis=1, keepdims=True)
            g_last = jnp.sum(jnp.where(last, g_rows, 0.0), axis=1, keepdims=True)
            decay = jnp.exp(jnp.where(causal, g_cum - g_rows, -jnp.inf))
            kb = k * beta
            kq = _mm_nt(jnp.concatenate([kb, q], axis=0), k)
            e_g = jnp.exp(g_cum)
            xs[i, h] = -(kq[0:sc] * jnp.where(strict, decay, 0.0))
            ts[i, h] = eye_f + xs[i, h]
            rhs[i, h] = jnp.concatenate([kb * e_g, v * beta], axis=1)
            qes[i, h] = q * e_g
            kd_s[rs, sl] = k * jnp.exp(g_last - g_cum)
            aqk_s[i, h] = kq[sc:2 * sc] * decay
            egl_s[i, h] = jnp.broadcast_to(jnp.exp(g_last), (sc, hd))
        for _ in range(n_sq):
            for u in units:
                xs[u] = _mm(xs[u], xs[u])
            for u in units:
                ts[u] = ts[u] + _mm(ts[u], xs[u])
        for (i, h) in units:
            rs = slice(i * sc, (i + 1) * sc)
            sl = slice(h * hd, (h + 1) * hd)
            wu = _mm(ts[i, h], rhs[i, h])
            w = wu[:, 0:hd]
            qe = qes[i, h]
            for j in range(nb):
                wq_s[i * nb + j, h, 0:c] = w[j * c:(j + 1) * c]
                wq_s[i * nb + j, h, c:2 * c] = qe[j * c:(j + 1) * c]
            u_s[rs, sl] = wu[:, hd:2 * hd]

    s_in = s_prev_ref if (has_state and single_step) else s_ref

    def inter(groups):
        units = [(i, h, j) for i in groups for h in range(DN_HEADS) for j in range(nb)]
        bidx = lambda i, j: (i * nb + j) if nb > 1 else i // groups_per_seq
        rows_of = lambda i, j: slice(i * sc + j * c, i * sc + (j + 1) * c)
        cols_of = lambda h: slice(h * hd, (h + 1) * hd)
        ss = {(i, h, j): s_in[bidx(i, j), h] for (i, h, j) in units}
        tqs = {(i, h, j): _mm(wq_s[i * nb + j, h], ss[i, h, j]) for (i, h, j) in units}
        us = {(i, h, j): u_s[rows_of(i, j), cols_of(h)] - tqs[i, h, j][0:c] for (i, h, j) in units}
        upd = {(i, h, j): _mm_tn(kd_s[rows_of(i, j), cols_of(h)], us[i, h, j])
               for (i, h, j) in units}
        for (i, h, j) in units:
            s_ref[bidx(i, j), h] = (ss[i, h, j] * egl_s[i, h, j * c:j * c + 1, :] + upd[i, h, j])
        for i in groups:
            for h in range(DN_HEADS):
                if nb > 1:
                    u = jnp.concatenate([us[i, h, j] for j in range(nb)], axis=0)
                    qs = jnp.concatenate([tqs[i, h, j][c:2 * c] for j in range(nb)], axis=0)
                else:
                    u = us[i, h, 0]
                    qs = tqs[i, h, 0][c:2 * c]
                o_s[i * sc:(i + 1) * sc, cols_of(h)] = qs + _mm(aqk_s[i, h], u)

    wave = INTRA_WAVE
    for i0 in range(0, nsc, wave):
        intra(range(i0, min(i0 + wave, nsc)))
    if nb > 1:
        for i in range(nsc):
            inter([i])
    else:
        for w in range(groups_per_seq):
            inter([b * groups_per_seq + w for b in range(bt)])

    for h in range(DN_HEADS):
        sl = slice(h * hd, (h + 1) * hd)
        o = o_s[:, sl]
        o = (o * lax.rsqrt(jnp.mean(o * o, axis=-1, keepdims=True) + EPS) * dng_ref[...]
             * sz_ref[:, :, sl].reshape(rows, hd))
        y_ref[:, :, sl] = o.reshape(bt, lt, hd).astype(y_ref.dtype)


def _even_mix(q, k, v, sz, gb, dn_norm, s_prev, prev_out, layer, n_layers, bt, lt):
    batch, length, dn = q.shape
    c = CHUNK if length % CHUNK == 0 else length
    hd = DN_HEAD_DIM
    nl = length // lt
    rows = bt * lt
    assert rows % DN_ROWS == 0 and DN_ROWS % c == 0 and (c == DN_ROWS or lt == c)
    nsc = rows // DN_ROWS
    nb = DN_ROWS // c
    grid = (batch // bt, nl)
    has_state = s_prev is not None
    tok_map = lambda b, l: (b, l, 0)
    tok = lambda width: pl.BlockSpec((bt, lt, width), tok_map)
    in_specs = [tok(dn), tok(dn), tok(dn), tok(dn), tok(LANES), _resident((1, hd))]
    args = [q, k, v, sz, gb, dn_norm.reshape(1, hd)]
    state_block = pl.BlockSpec((None, bt, DN_HEADS, hd, hd), lambda b, l: (layer, b, 0, 0, 0))
    if has_state:
        in_specs.append(state_block)
        args.append(s_prev)
    aliases = {}
    if prev_out is not None:
        aliases[len(args)] = 1
        in_specs.append(pl.BlockSpec(memory_space=pl.ANY))
        args.append(prev_out)
    out_specs = [tok(dn), state_block]
    out_shape = [jax.ShapeDtypeStruct((batch, length, dn), _token_dtype(lt)),
                 jax.ShapeDtypeStruct((n_layers, batch, DN_HEADS, hd, hd), F32)]
    scratch = [pltpu.VMEM((rows, dn), F32),
               pltpu.VMEM((nsc * nb, DN_HEADS, 2 * c, hd), F32),
               pltpu.VMEM((rows, dn), F32), pltpu.VMEM((rows, dn), F32),
               pltpu.VMEM((nsc, DN_HEADS, DN_ROWS, DN_ROWS), F32),
               pltpu.VMEM((nsc, DN_HEADS, DN_ROWS, hd), F32)]
    return pl.pallas_call(
        functools.partial(_even_mix_kernel, bt=bt, lt=lt, c=c, has_state=has_state,
                          single_step=nl == 1, n_alias=len(aliases)),
        grid=grid, in_specs=in_specs, out_specs=out_specs, out_shape=out_shape,
        scratch_shapes=scratch, input_output_aliases=aliases,
        compiler_params=_compiler_params(("arbitrary", "arbitrary")),
        name="even_mix")(*args)


def _in_odd_kernel(*refs, starts, steps):
    ng = len(steps)
    g_ref, w_ref = refs[3 * ng:3 * ng + 2]
    outs = refs[3 * ng + 2:]
    dq = RET_HEADS * RET_QK_DIM
    dv = RET_HEADS * RET_V_DIM

    def body(x_ref, cos_ref, sin_ref, q_ref, k_ref, v_ref, sg_ref):
        h = _rmsnorm(x_ref[...], g_ref[...]).astype(BF16)

        def proj(lo, hi):
            return jnp.dot(h, w_ref[:, lo:hi], preferred_element_type=F32)

        cos2 = cos_ref[...]
        sin2 = sin_ref[...]

        def rotary(t, out_ref, scale):
            c2, s2 = (cos2, sin2) if scale is None else (cos2 * scale, sin2 * scale)
            for hh in range(RET_HEADS):
                sl = slice(hh * RET_QK_DIM, (hh + 1) * RET_QK_DIM)
                th = t[:, sl]
                r = th * c2 + pltpu.roll(th, RET_QK_DIM // 2, 1) * s2
                out_ref[:, sl] = r.astype(out_ref.dtype)

        q = proj(0, dq)
        k = proj(dq, 2 * dq)
        rotary(q, q_ref, None)
        v = proj(2 * dq, 2 * dq + dv)
        rotary(k, k_ref, RET_QK_DIM ** -0.5)
        gate = proj(2 * dq + dv, 2 * dq + 2 * dv)
        v_ref[...] = v.astype(v_ref.dtype)
        sg_ref[...] = _silu_of_twice(gate).astype(sg_ref.dtype)

    _for_each_group(starts, steps, [
        functools.partial(body, *refs[3 * g:3 * g + 3], *outs[4 * g:4 * g + 4]) for g in range(ng)])


def _in_odd(xs, g, w, layer, poss, seq_lens, tms):
    d = xs[0].shape[1]
    dq = RET_HEADS * RET_QK_DIM
    dv = RET_HEADS * RET_V_DIM
    half = RET_QK_DIM // 2
    inv = ROPE_BASE ** (-jnp.arange(half, dtype=F32) / half)
    steps = [x.shape[0] // tm for x, tm in zip(xs, tms)]
    starts = [sum(steps[:i]) for i in range(len(steps))]
    args, in_specs, out_specs, out_shape = [], [], [], []
    for x, pos, seq_len, tm, start, n in zip(xs, poss, seq_lens, tms, starts, steps):
        ang = pos.astype(F32)[:, None] * inv[None, :]
        cos = jnp.cos(ang)
        sin = jnp.sin(ang)
        cos2 = jnp.concatenate([cos, cos], axis=-1)
        sin2 = jnp.concatenate([-sin, sin], axis=-1)
        if seq_len < tm:
            cos2 = jnp.tile(cos2, (tm // seq_len, 1))
            sin2 = jnp.tile(sin2, (tm // seq_len, 1))
        n_tab = cos2.shape[0] // tm
        index, row = _group_rows(start, n, tm)
        tab = pl.BlockSpec((tm, RET_QK_DIM),
                           lambda i, index=index, n_tab=n_tab: (index(i) % n_tab, 0))
        args += [x, cos2, sin2]
        in_specs += [row(d), tab, tab]
        out_specs += [row(dq), row(dq), row(dv), row(dv)]
        out_shape += [jax.ShapeDtypeStruct((x.shape[0], width), BF16) for width in (dq, dq, dv, dv)]
    outs = pl.pallas_call(
        functools.partial(_in_odd_kernel, starts=tuple(starts), steps=tuple(steps)),
        grid=(sum(steps),), in_specs=in_specs + [_resident((1, d)), _layer_rows(w, layer)],
        out_specs=out_specs, out_shape=out_shape,
        compiler_params=_compiler_params(("arbitrary",)),
        name="in_odd")(*args, g.reshape(1, d), w)
    return [outs[4 * i:4 * i + 4] for i in range(len(xs))]


def _odd_mix_kernel(*refs, bt, lt, has_state, single_step, tok3d, n_alias):
    n_in = 6 + (1 if has_state else 0)
    (q_ref, k_ref, v_ref, sg_ref, gng_ref, gnb_ref) = refs[:6]
    if has_state:
        r_prev_ref = refs[6]
    (y_ref, r_ref, o_s, dec_s, inn_s, kdec_s) = refs[n_in + n_alias:]
    c = lt
    l = pl.program_id(1)

    lgs = [math.log(1.0 - 2.0 ** (-5.0 - h)) for h in range(RET_HEADS)]

    @pl.when((pl.program_id(0) == 0) & (l == 0))
    def _():
        ri = lax.broadcasted_iota(jnp.int32, (c, c), 0)
        ci = lax.broadcasted_iota(jnp.int32, (c, c), 1)
        diff = (ri - ci).astype(F32)
        idx = lax.broadcasted_iota(jnp.int32, (c, RET_QK_DIM), 0).astype(F32)
        for h in range(RET_HEADS):
            dec_s[h] = jnp.where(diff >= 0, jnp.exp(jnp.maximum(diff, 0.0) * lgs[h]), 0.0)
            inn_s[h] = jnp.exp((idx + 1.0) * lgs[h])
            kdec_s[h] = jnp.exp((c - 1.0 - idx) * lgs[h])

    @pl.when(l == 0)
    def _():
        if not has_state:
            r_ref[...] = jnp.zeros(r_ref.shape, F32)
        elif not single_step:
            r_ref[...] = r_prev_ref[...]

    hg = 1 if c >= LANES else RET_HEADS
    r_in = r_prev_ref if (has_state and single_step) else r_ref
    def norm_gate(o, h, b=None):
        sl = slice(h * RET_V_DIM, (h + 1) * RET_V_DIM)
        mu = jnp.mean(o, axis=-1, keepdims=True)
        var = jnp.mean(jnp.square(o - mu), axis=-1, keepdims=True)
        o = (o - mu) * lax.rsqrt(var + GN_EPS) * gng_ref[:, sl] + gnb_ref[:, sl]
        if tok3d:
            y_ref[b, :, sl] = (sg_ref[b, :, sl].astype(F32) * o).astype(y_ref.dtype)
        else:
            y_ref[:, sl] = (sg_ref[:, sl].astype(F32) * o).astype(y_ref.dtype)

    def seq_rows(ref, b, lo, hi):
        if tok3d:
            return ref[b, :, lo:hi]
        if bt == 1:
            return ref[:, lo:hi]
        return ref[:, lo:hi].astype(F32)[b * c:(b + 1) * c]

    for b in range(bt):
        rs = slice(b * c, (b + 1) * c)
        for h0 in range(0, RET_HEADS, hg):
            heads = range(h0, h0 + hg)
            qs = {h: seq_rows(q_ref, b, h * RET_QK_DIM, (h + 1) * RET_QK_DIM) for h in heads}
            ks = {h: seq_rows(k_ref, b, h * RET_QK_DIM, (h + 1) * RET_QK_DIM) for h in heads}
            vs = {h: seq_rows(v_ref, b, h * RET_V_DIM, (h + 1) * RET_V_DIM) for h in heads}
            rr = {h: r_in[b, h] for h in heads}
            att = {h: _mm_nt(qs[h], ks[h]) * dec_s[h] for h in heads}
            qr = {h: _mm(qs[h].astype(F32) * inn_s[h], rr[h]) for h in heads}
            kv = {h: _mm_tn(ks[h].astype(F32) * kdec_s[h], vs[h]) for h in heads}
            for h in heads:
                r_ref[b, h] = rr[h] * math.exp(c * lgs[h]) + kv[h]
            for h in heads:
                o = _mm(att[h], vs[h]) + qr[h]
                if tok3d or bt == 1:
                    norm_gate(o, h, b)
                else:
                    o_s[rs, h * RET_V_DIM:(h + 1) * RET_V_DIM] = o

    if bt > 1 and not tok3d:
        for h in range(RET_HEADS):
            norm_gate(o_s[:, h * RET_V_DIM:(h + 1) * RET_V_DIM], h)


def _odd_mix(q, k, v, sg, gn_g, gn_b, r_prev, prev_out, layer, n_layers, batch, length, bt, lt):
    dq = RET_HEADS * RET_QK_DIM
    dv = RET_HEADS * RET_V_DIM
    nl = length // lt
    rows = bt * lt
    grid = (batch // bt, nl)
    has_state = r_prev is not None
    tok3d = nl > 1
    if tok3d:
        row = lambda width: pl.BlockSpec((bt, lt, width), lambda b, l: (b, l, 0))
        q, k, v, sg = [t.reshape(batch, length, t.shape[-1]) for t in (q, k, v, sg)]
        y_shape = (batch, length, dv)
    else:
        row = lambda width: pl.BlockSpec((rows, width), lambda b, l: (b, 0))
        y_shape = (batch * length, dv)
    in_specs = [row(dq), row(dq), row(dv), row(dv), _resident((1, dv)), _resident((1, dv))]
    args = [q, k, v, sg, gn_g.reshape(1, dv), gn_b.reshape(1, dv)]
    state_block = pl.BlockSpec((None, bt, RET_HEADS, RET_QK_DIM, RET_V_DIM),
                               lambda b, l: (layer, b, 0, 0, 0))
    if has_state:
        in_specs.append(state_block)
        args.append(r_prev)
    aliases = {}
    if prev_out is not None:
        aliases[len(args)] = 1
        in_specs.append(pl.BlockSpec(memory_space=pl.ANY))
        args.append(prev_out)
    out_specs = [row(dv), state_block]
    out_shape = [jax.ShapeDtypeStruct(y_shape, BF16),
                 jax.ShapeDtypeStruct((n_layers, batch, RET_HEADS, RET_QK_DIM, RET_V_DIM), F32)]
    scratch = [pltpu.VMEM((rows, dv), F32),
               pltpu.VMEM((RET_HEADS, lt, lt), F32),
               pltpu.VMEM((RET_HEADS, lt, RET_QK_DIM), F32),
               pltpu.VMEM((RET_HEADS, lt, RET_QK_DIM), F32)]
    y, r_new = pl.pallas_call(
        functools.partial(_odd_mix_kernel, bt=bt, lt=lt, has_state=has_state,
                          single_step=nl == 1, tok3d=tok3d, n_alias=len(aliases)),
        grid=grid, in_specs=in_specs, out_specs=out_specs, out_shape=out_shape,
        scratch_shapes=scratch, input_output_aliases=aliases,
        compiler_params=_compiler_params(("arbitrary", "arbitrary")),
        name="odd_mix")(*args)
    return y.reshape(batch * length, dv), r_new


def _out_ffn_kernel(*refs, n_y, starts, steps, final):
    ng = len(steps)
    per = 1 + n_y
    rest = refs[ng * per:]
    wo_refs = rest[:n_y]
    g_ref, wg_ref, wu_ref, wd_ref, gf_ref = rest[n_y:n_y + 5]
    o_refs = rest[n_y + 5:]

    def body(x_ref, y_refs, o_ref):
        x1 = x_ref[...]
        for y_ref, wo_ref in zip(y_refs, wo_refs):
            x1 = x1 + jnp.dot(y_ref[...].astype(BF16), wo_ref[...], preferred_element_type=F32)
        h = _rmsnorm(x1, g_ref[...]).astype(BF16)
        gate = jnp.dot(h, wg_ref[...], preferred_element_type=F32)
        up = jnp.dot(h, wu_ref[...], preferred_element_type=F32)
        a = (_silu_of_twice(gate) * up).astype(BF16)
        x2 = x1 + jnp.dot(a, wd_ref[...], preferred_element_type=F32)
        if final:
            x2 = _rmsnorm(x2, gf_ref[...])
        o_ref[...] = x2

    _for_each_group(starts, steps, [
        functools.partial(body, refs[g * per], refs[g * per + 1:(g + 1) * per], o_refs[g])
        for g in range(ng)])


def _out_ffn(xs, yss, wo, wo_layer, g, wg, wu, wd, ffn_layer, gf, final, tms):
    d = xs[0].shape[1]
    n_y = len(yss[0])
    assert len({y.shape[1] for ys in yss for y in ys}) == 1
    steps = [x.shape[0] // tm for x, tm in zip(xs, tms)]
    starts = [sum(steps[:i]) for i in range(len(steps))]
    args, in_specs, out_specs, out_shape = [], [], [], []
    for x, ys, tm, start, n in zip(xs, yss, tms, starts, steps):
        _, row = _group_rows(start, n, tm)
        args += [x] + list(ys)
        in_specs += [row(d)] + [row(y.shape[1]) for y in ys]
        out_specs.append(row(d))
        out_shape.append(jax.ShapeDtypeStruct(x.shape, F32))
    in_specs += ([_layer_rows(wo, wo_layer, j, n_y) for j in range(n_y)]
                 + [_resident((1, d)), _layer_rows(wg, ffn_layer), _layer_rows(wu, ffn_layer),
                    _layer_rows(wd, ffn_layer), _resident((1, d))])
    return pl.pallas_call(
        functools.partial(_out_ffn_kernel, n_y=n_y, starts=tuple(starts), steps=tuple(steps),
                          final=final),
        grid=(sum(steps),), in_specs=in_specs, out_specs=out_specs, out_shape=out_shape,
        compiler_params=_compiler_params(("arbitrary",)),
        name="out_ffn")(*args, *([wo] * n_y), g.reshape(1, d), wg, wu, wd, gf.reshape(1, d))


def _tiles(batch, length):
    tm = min(batch * length, ROW_TILE)
    if length % CHUNK == 0:
        lt = min(length, 256)
        return dict(tm=tm, in_even=(1, min(length, tm)), even=(math.gcd(batch, 2), min(length, 2 * lt)),
                    odd=(math.gcd(batch, 2), lt))
    return dict(tm=tm, in_even=(max(tm // length, 1), length), even=(min(batch, 32), length),
                odd=(min(batch, 8), length))


def _run_trunk(groups, w):
    depth = w["norm_mix"].shape[0]
    n_even = (depth + 1) // 2
    n_odd = depth // 2
    gs = []
    for x, states, pos in groups:
        batch, length, d = x.shape
        gs.append(dict(batch=batch, length=length, d=d, states=states, pos=pos,
                       t=_tiles(batch, length), x=x.reshape(batch * length, d),
                       conv_out=None,
                       delta_out=None, ret_out=None))
    tms = [g["t"]["tm"] for g in gs]
    for i in range(depth):
        if i % 2 == 0:
            e = i // 2
            for g in gs:
                st = g["states"]
                (ya, q, k, v, sz, gb), g["conv_out"] = _in_even(
                    g["x"].reshape(g["batch"], g["length"], g["d"]), w["norm_mix"][i],
                    w["even_w_main"], w["even_w_ab"], w["even_conv_a"][e], w["even_conv_qkv"][e],
                    w["even_a_log"][e], w["even_dt_bias"][e],
                    None if st is None else (st[0], st[1]), g["conv_out"], e, n_even,
                    *g["t"]["in_even"])
                yb, g["delta_out"] = _even_mix(q, k, v, sz, gb, w["even_dn_norm"][e],
                                               None if st is None else st[2], g["delta_out"],
                                               e, n_even, *g["t"]["even"])
                rows = g["batch"] * g["length"]
                g["ys"] = [ya.reshape(rows, -1), yb.reshape(rows, -1)]
            wo, wo_layer = w["even_w_out"], e
        else:
            o = i // 2
            qkvs = _in_odd([g["x"] for g in gs], w["norm_mix"][i], w["odd_w_in"], o,
                           [g["pos"] for g in gs], [g["length"] for g in gs], tms)
            for g, (q, k, v, sg) in zip(gs, qkvs):
                st = g["states"]
                y, g["ret_out"] = _odd_mix(q, k, v, sg, w["odd_gn_g"][o], w["odd_gn_b"][o],
                                           None if st is None else st[3], g["ret_out"], o, n_odd,
                                           g["batch"], g["length"], *g["t"]["odd"])
                g["ys"] = [y]
            wo, wo_layer = w["odd_w_out"], o
        xs = _out_ffn([g["x"] for g in gs], [g["ys"] for g in gs], wo, wo_layer, w["norm_ffn"][i],
                      w["ffn_w_gate"], w["ffn_w_up"], w["ffn_w_down"], i, w["final_norm"],
                      i == depth - 1, tms)
        for g, x in zip(gs, xs):
            g["x"] = x
    return [(g["x"].reshape(g["batch"], g["length"], g["d"]),) + g["conv_out"]
            + (g["delta_out"], g["ret_out"]) for g in gs]


def kernel(x_prompt, x_sample, state_conv_a, state_conv_qkv, state_delta, state_ret, norm_mix,
           norm_ffn, final_norm, even_w_in, even_conv_a, even_conv_qkv, even_a_log, even_dt_bias,
           even_dn_norm, even_w_out, odd_w_in, odd_gn_g, odd_gn_b, odd_w_out, ffn_w_gate, ffn_w_up,
           ffn_w_down):
    n_main = even_w_in.shape[-1] - 2 * DN_HEADS
    dv = RET_HEADS * RET_V_DIM
    odd_cols = jnp.arange(odd_w_in.shape[-1])
    odd_half = jnp.where(odd_cols >= odd_w_in.shape[-1] - dv, 0.5, 1.0)
    w = dict(norm_mix=norm_mix, norm_ffn=norm_ffn, final_norm=final_norm,
             even_w_main=even_w_in.astype(BF16),
             even_w_ab=jnp.pad(even_w_in[:, :, n_main:],
                               ((0, 0), (0, 0), (0, LANES - 2 * DN_HEADS))).astype(BF16),
             even_conv_a=even_conv_a, even_conv_qkv=0.5 * even_conv_qkv, even_a_log=even_a_log,
             even_dt_bias=even_dt_bias, even_dn_norm=even_dn_norm,
             even_w_out=even_w_out.astype(BF16), odd_w_in=(odd_w_in * odd_half).astype(BF16),
             odd_gn_g=odd_gn_g, odd_gn_b=odd_gn_b, odd_w_out=odd_w_out.astype(BF16),
             ffn_w_gate=(0.5 * ffn_w_gate).astype(BF16), ffn_w_up=ffn_w_up.astype(BF16),
             ffn_w_down=ffn_w_down.astype(BF16))
    lp = x_prompt.shape[1]
    ls = x_sample.shape[1]
    pos_p = jnp.arange(lp, dtype=jnp.int32)
    pos_s = PAST_LEN + jnp.arange(ls, dtype=jnp.int32)
    (y_prompt, ca_p, cq_p, d_p, r_p), (y_sample, ca_s, cq_s, d_s, r_s) = _run_trunk(
        [(x_prompt, None, pos_p),
         (x_sample, (state_conv_a, state_conv_qkv, state_delta, state_ret), pos_s)], w)
    return (y_prompt, y_sample, ca_p, cq_p, d_p, r_p, ca_s, cq_s, d_s, r_s)
```
